```python
import math
import jax, jax.numpy as jnp
from jax import lax
import numpy as np

D_MODEL = 1024
BATCH = 2
SEQ = 8192
DEPTH = 4
DEC_BATCH = 128
DEC_SEQ = 8
PAST_LEN = 8192
PAGE_SIZE = 128

N_MIXERS = 2
N_RWKV_LAYERS = (DEPTH + N_MIXERS - 1) // N_MIXERS
N_SWA_LAYERS = DEPTH // N_MIXERS
N_VRES = max(N_RWKV_LAYERS - 1, 0)
N_META = 16
RWKV_HEAD = 64
RWKV_HEADS = D_MODEL // RWKV_HEAD
DECAY_LORA = 64
AAA_LORA = 64
MV_LORA = 32
GATE_LORA = 128
GN_EPS = 6.4e-4
HEAD_DIM = 64
N_HEADS = D_MODEL // HEAD_DIM
N_KV_HEADS = 4
GROUP = N_HEADS // N_KV_HEADS
Q_WIDTH = N_HEADS * HEAD_DIM
KV_WIDTH = N_KV_HEADS * HEAD_DIM
WINDOW = 128
BLOCK = WINDOW
WIN_BUF = min(WINDOW, PAST_LEN)
D_FF = 4 * D_MODEL
RMS_EPS = 1e-6
NEG_INF = -1e30

kernel_name = 'rwkv7_swa_sink_hybrid_step'

F32 = jnp.float32


def _rms(x, g):
    xf = x.astype(F32)
    y = xf * lax.rsqrt(jnp.mean(xf * xf, axis=-1, keepdims=True) + RMS_EPS)
    return (y * g.astype(F32)).astype(x.dtype)


def _mlp(h, w_up, w_down):
    a = jax.nn.relu(h @ w_up)
    return (a * a) @ w_down


def _alibi_slopes():
    s = 2.0 ** (-8.0 * jnp.arange(1, N_HEADS + 1, dtype=F32) / N_HEADS)
    return s.reshape(N_KV_HEADS, GROUP)


def _rwkv7_time_mix(xn, shift_prev, state0, v_first, vres, mu, w_r, w_k, w_v, w_o, w0, w1, w2,
                    a0, a1, a2, g1, g2, k_k, k_a, r_k, lnx_w, lnx_b):
    n, t, d = xn.shape
    dt = xn.dtype
    x_prev = jnp.concatenate([shift_prev[:, None, :].astype(dt), xn[:, :-1]], axis=1)
    xx = x_prev - xn
    xr, xw, xk, xv, xa, xg = (xn + xx * mu[c].astype(dt) for c in range(6))
    r = xr @ w_r
    k = xk @ w_k
    v = xv @ w_v
    w_log = -jax.nn.softplus(-(w0 + jnp.tanh(xw @ w1) @ w2).astype(F32)) - 0.5
    if vres is None:
        v_first = v
    else:
        v0, v1, v2 = vres
        v = v + (v_first - v) * jax.nn.sigmoid(v0 + (xv @ v1) @ v2)
    a = jax.nn.sigmoid(a0 + (xa @ a1) @ a2)
    g = jax.nn.sigmoid(xg @ g1) @ g2

    def heads(z):
        return z.reshape(n, t, RWKV_HEADS, RWKV_HEAD).astype(F32)

    kk = heads(k * k_k)
    kk = kk / jnp.maximum(jnp.sqrt(jnp.sum(kk * kk, axis=-1, keepdims=True)), 1e-12)
    k = k * (1 + (a - 1) * k_a)
    rh, kh, vh, ah = heads(r), heads(k), heads(v), heads(a)
    decay = jnp.exp(-jnp.exp(heads(w_log)))

    def step(S, inp):
        r_t, w_t, k_t, v_t, kk_t, a_t = inp
        sk = jnp.einsum('nhvk,nhk->nhv', S, kk_t)
        S = (S * w_t[:, :, None, :] - sk[..., None] * (kk_t * a_t)[:, :, None, :]
             + v_t[..., None] * k_t[:, :, None, :])
        return S, jnp.einsum('nhvk,nhk->nhv', S, r_t)

    seq = (jnp.moveaxis(rh, 1, 0), jnp.moveaxis(decay, 1, 0), jnp.moveaxis(kh, 1, 0),
           jnp.moveaxis(vh, 1, 0), jnp.moveaxis(kk, 1, 0), jnp.moveaxis(ah, 1, 0))
    S_T, y = lax.scan(step, state0.astype(F32), seq)
    y = jnp.moveaxis(y, 0, 1)
    mean = jnp.mean(y, axis=-1, keepdims=True)
    var = jnp.mean(jnp.square(y - mean), axis=-1, keepdims=True)
    y = ((y - mean) * lax.rsqrt(var + GN_EPS)).reshape(n, t, d)
    y = y * lnx_w.astype(F32) + lnx_b.astype(F32)
    bonus = (jnp.sum(rh * kh * r_k.astype(F32), axis=-1, keepdims=True) * vh).reshape(n, t, d)
    out = ((y + bonus).astype(dt) * g) @ w_o
    return out, S_T, xn[:, -1], v_first


def _split_qkv(h, w_qkv):
    n, t, _ = h.shape
    qkv = h @ w_qkv
    q = qkv[..., :Q_WIDTH].reshape(n, t, N_KV_HEADS, GROUP, HEAD_DIM)
    k = qkv[..., Q_WIDTH:Q_WIDTH + KV_WIDTH].reshape(n, t, N_KV_HEADS, HEAD_DIM)
    v = qkv[..., Q_WIDTH + KV_WIDTH:].reshape(n, t, N_KV_HEADS, HEAD_DIM)
    return q, k, v


def _attend(q, k, v, qpos, kpos, kvalid, kmeta, sinks):
    slopes = _alibi_slopes()
    s = jnp.einsum('...qhgd,...shd->...hgqs', q, k).astype(F32) * (HEAD_DIM ** -0.5)
    dist = qpos[..., :, None] - kpos[..., None, :]
    ok = kvalid[..., None, :] & (dist >= 0) & (kmeta[..., None, :] | (dist < WINDOW))
    bias = slopes[:, :, None, None] * jnp.minimum(dist, WINDOW).astype(F32)[..., None, None, :, :]
    logits = jnp.where(ok[..., None, None, :, :], s - bias, NEG_INF)
    sk = sinks.reshape(N_KV_HEADS, GROUP, 1).astype(F32)
    m = jnp.maximum(jnp.max(logits, axis=-1), sk)
    p = jnp.exp(logits - m[..., None])
    p = p / (jnp.sum(p, axis=-1) + jnp.exp(sk - m))[..., None]
    return jnp.einsum('...hgqs,...shd->...qhgd', p.astype(v.dtype), v)


def _swa_prompt(h, w_qkv, w_o, sinks):
    n, l, _ = h.shape
    nblk = (l - N_META) // BLOCK
    q, k, v = _split_qkv(h, w_qkv)
    meta_pos = jnp.arange(N_META)
    all_true = jnp.ones((N_META,), bool)
    o_meta = _attend(q[:, :N_META], k[:, :N_META], v[:, :N_META], meta_pos, meta_pos,
                     all_true, all_true, sinks)
    qb = q[:, N_META:].reshape(n, nblk, BLOCK, N_KV_HEADS, GROUP, HEAD_DIM)

    def band_keys(z):
        zb = z[:, N_META:].reshape(n, nblk, BLOCK, N_KV_HEADS, HEAD_DIM)
        prev = jnp.concatenate([jnp.zeros_like(zb[:, :1]), zb[:, :-1]], axis=1)
        meta = jnp.broadcast_to(z[:, None, :N_META], (n, nblk, N_META, N_KV_HEADS, HEAD_DIM))
        return jnp.concatenate([meta, prev, zb], axis=2)

    blk = jnp.arange(nblk)[:, None]
    qpos = N_META + blk * BLOCK + jnp.arange(BLOCK)[None, :]
    kpos = jnp.concatenate([jnp.broadcast_to(meta_pos[None], (nblk, N_META)), qpos - BLOCK, qpos], axis=1)
    kvalid = jnp.concatenate([jnp.ones((nblk, N_META), bool),
                              jnp.broadcast_to(blk >= 1, (nblk, BLOCK)),
                              jnp.ones((nblk, BLOCK), bool)], axis=1)
    kmeta = jnp.arange(N_META + 2 * BLOCK) < N_META
    o_band = _attend(qb, band_keys(k), band_keys(v), qpos, kpos, kvalid, kmeta, sinks)
    o = jnp.concatenate([o_meta.reshape(n, N_META, Q_WIDTH), o_band.reshape(n, l - N_META, Q_WIDTH)], axis=1)
    return o @ w_o, k[:, :N_META], v[:, :N_META], k[:, -WIN_BUF:], v[:, -WIN_BUF:]


def _swa_sample(h, meta_k, meta_v, win_k, win_v, w_qkv, w_o, sinks):
    n, t, _ = h.shape
    w = win_k.shape[1]
    q, k, v = _split_qkv(h, w_qkv)
    qpos = PAST_LEN + jnp.arange(t)
    kpos = jnp.concatenate([jnp.arange(N_META), PAST_LEN - w + jnp.arange(w), qpos])
    kmeta = jnp.arange(N_META + w + t) < N_META
    kvalid = kmeta | (kpos >= N_META)
    win_k_all = jnp.concatenate([win_k.astype(k.dtype), k], axis=1)
    win_v_all = jnp.concatenate([win_v.astype(v.dtype), v], axis=1)
    keys = jnp.concatenate([meta_k.astype(k.dtype), win_k_all], axis=1)
    vals = jnp.concatenate([meta_v.astype(v.dtype), win_v_all], axis=1)
    o = _attend(q, keys, vals, qpos, kpos, kvalid, kmeta, sinks)
    return o.reshape(n, t, Q_WIDTH) @ w_o, win_k_all[:, -w:], win_v_all[:, -w:]


def setup_inputs(seed: int = 0) -> dict:
    key = jax.random.key(seed)
    ks = iter(jax.random.split(key, 48))

    def nrm(shape, scale):
        return scale * jax.random.normal(next(ks), shape, F32)

    D = D_MODEL
    NA, NB, NV = N_RWKV_LAYERS, N_SWA_LAYERS, N_VRES
    return {
        'x_prompt': nrm((BATCH, SEQ, D), 1.0),
        'x_sample': nrm((DEC_BATCH, DEC_SEQ, D), 1.0),
        'state_wkv': nrm((DEC_BATCH, NA, RWKV_HEADS, RWKV_HEAD, RWKV_HEAD), 0.1),
        'state_shift': nrm((DEC_BATCH, NA, D), 1.0),
        'cache_win_k': nrm((DEC_BATCH, NB, WIN_BUF, N_KV_HEADS, HEAD_DIM), 1.0),
        'cache_win_v': nrm((DEC_BATCH, NB, WIN_BUF, N_KV_HEADS, HEAD_DIM), 1.0),
        'cache_meta_k': nrm((DEC_BATCH, NB, N_META, N_KV_HEADS, HEAD_DIM), 1.0),
        'cache_meta_v': nrm((DEC_BATCH, NB, N_META, N_KV_HEADS, HEAD_DIM), 1.0),
        'meta_tokens': nrm((N_META, D), 1.0),
        'norm_gains': 1.0 + nrm((DEPTH, 4, D), 0.05),
        'rwkv_mu': jax.random.uniform(next(ks), (NA, 6, D), F32),
        'rwkv_w_r': nrm((NA, D, D), D ** -0.5),
        'rwkv_w_k': nrm((NA, D, D), D ** -0.5),
        'rwkv_w_v': nrm((NA, D, D), D ** -0.5),
        'rwkv_w_o': nrm((NA, D, D), D ** -0.5),
        'rwkv_w0': jax.random.uniform(next(ks), (NA, D), F32, minval=-4.0, maxval=1.0),
        'rwkv_w1': nrm((NA, D, DECAY_LORA), D ** -0.5),
        'rwkv_w2': nrm((NA, DECAY_LORA, D), 0.5 * DECAY_LORA ** -0.5),
        'rwkv_a0': nrm((NA, D), 0.5),
        'rwkv_a1': nrm((NA, D, AAA_LORA), D ** -0.5),
        'rwkv_a2': nrm((NA, AAA_LORA, D), 0.5 * AAA_LORA ** -0.5),
        'rwkv_v0': nrm((NV, D), 0.5),
        'rwkv_v1': nrm((NV, D, MV_LORA), D ** -0.5),
        'rwkv_v2': nrm((NV, MV_LORA, D), 0.5 * MV_LORA ** -0.5),
        'rwkv_g1': nrm((NA, D, GATE_LORA), D ** -0.5),
        'rwkv_g2': nrm((NA, GATE_LORA, D), GATE_LORA ** -0.5),
        'rwkv_k_k': 0.85 + nrm((NA, D), 0.05),
        'rwkv_k_a': 1.0 + nrm((NA, D), 0.05),
        'rwkv_r_k': nrm((NA, RWKV_HEADS, RWKV_HEAD), 0.1),
        'rwkv_lnx_w': 1.0 + nrm((NA, D), 0.05),
        'rwkv_lnx_b': nrm((NA, D), 0.02),
        'attn_w_qkv': nrm((NB, D, Q_WIDTH + 2 * KV_WIDTH), D ** -0.5),
        'attn_w_o': nrm((NB, Q_WIDTH, D), Q_WIDTH ** -0.5),
        'attn_sinks': nrm((NB, N_HEADS), 0.5),
        'mlp_w_up': nrm((DEPTH, D, D_FF), D ** -0.5),
        'mlp_w_down': nrm((DEPTH, D_FF, D), D_FF ** -0.5),
    }


def reference(x_prompt, x_sample, state_wkv, state_shift, cache_win_k, cache_win_v, cache_meta_k,
              cache_meta_v, meta_tokens, norm_gains, rwkv_mu, rwkv_w_r, rwkv_w_k, rwkv_w_v, rwkv_w_o,
              rwkv_w0, rwkv_w1, rwkv_w2, rwkv_a0, rwkv_a1, rwkv_a2, rwkv_v0, rwkv_v1, rwkv_v2,
              rwkv_g1, rwkv_g2, rwkv_k_k, rwkv_k_a, rwkv_r_k, rwkv_lnx_w, rwkv_lnx_b,
              attn_w_qkv, attn_w_o, attn_sinks, mlp_w_up, mlp_w_down):
    b = x_prompt.shape[0]
    dt = x_prompt.dtype
    xp = jnp.concatenate([jnp.broadcast_to(meta_tokens.astype(dt)[None], (b, N_META, D_MODEL)), x_prompt], axis=1)
    xs = x_sample
    vfirst_p = None
    vfirst_s = None
    p_wkv, p_shift, s_wkv, s_shift = [], [], [], []
    p_wk, p_wv, p_mk, p_mv, s_wk, s_wv = [], [], [], [], [], []
    for i in range(DEPTH):
        gains = norm_gains[i]
        j = i // N_MIXERS
        hp = _rms(xp, gains[0])
        hs = _rms(xs, gains[0])
        if i % N_MIXERS == 0:
            prm = (rwkv_mu[j], rwkv_w_r[j], rwkv_w_k[j], rwkv_w_v[j], rwkv_w_o[j], rwkv_w0[j],
                   rwkv_w1[j], rwkv_w2[j], rwkv_a0[j], rwkv_a1[j], rwkv_a2[j], rwkv_g1[j], rwkv_g2[j],
                   rwkv_k_k[j], rwkv_k_a[j], rwkv_r_k[j], rwkv_lnx_w[j], rwkv_lnx_b[j])
            vres = None if j == 0 else (rwkv_v0[j - 1], rwkv_v1[j - 1], rwkv_v2[j - 1])
            zero_state = jnp.zeros((b, RWKV_HEADS, RWKV_HEAD, RWKV_HEAD), F32)
            zero_shift = jnp.zeros((b, D_MODEL), dt)
            mp, st_p, sh_p, vf_p = _rwkv7_time_mix(hp, zero_shift, zero_state, vfirst_p, vres, *prm)
            ms, st_s, sh_s, vf_s = _rwkv7_time_mix(hs, state_shift[:, j], state_wkv[:, j], vfirst_s, vres, *prm)
            vfirst_p, vfirst_s = vf_p, vf_s
            p_wkv.append(st_p)
            p_shift.append(sh_p)
            s_wkv.append(st_s)
            s_shift.append(sh_s)
        else:
            mp, mk, mv, wk, wv = _swa_prompt(hp, attn_w_qkv[j], attn_w_o[j], attn_sinks[j])
            ms, wk_s, wv_s = _swa_sample(hs, cache_meta_k[:, j], cache_meta_v[:, j], cache_win_k[:, j],
                                         cache_win_v[:, j], attn_w_qkv[j], attn_w_o[j], attn_sinks[j])
            p_mk.append(mk)
            p_mv.append(mv)
            p_wk.append(wk)
            p_wv.append(wv)
            s_wk.append(wk_s)
            s_wv.append(wv_s)
        xp = xp + _rms(mp, gains[1])
        xs = xs + _rms(ms, gains[1])
        xp = xp + _rms(_mlp(_rms(xp, gains[2]), mlp_w_up[i], mlp_w_down[i]), gains[3])
        xs = xs + _rms(_mlp(_rms(xs, gains[2]), mlp_w_up[i], mlp_w_down[i]), gains[3])
    y_prompt = xp[:, N_META:]
    y_sample = xs
    return (y_prompt, y_sample,
            jnp.stack(p_wkv, axis=1), jnp.stack(p_shift, axis=1),
            jnp.stack(p_wk, axis=1), jnp.stack(p_wv, axis=1),
            jnp.stack(p_mk, axis=1), jnp.stack(p_mv, axis=1),
            jnp.stack(s_wkv, axis=1), jnp.stack(s_shift, axis=1),
            jnp.stack(s_wk, axis=1), jnp.stack(s_wv, axis=1))
```

```python
import functools

import jax
import jax.numpy as jnp
from jax import lax
from jax.experimental import pallas as pl
from jax.experimental.pallas import tpu as pltpu

F32 = jnp.float32
BF16 = jnp.bfloat16

D_MODEL = 1024
BATCH = 2
SEQ = 8192
DEPTH = 4
DEC_BATCH = 128
DEC_SEQ = 8
PAST_LEN = 8192
N_META = 16
HEADS = 16
HEAD = 64
N_KV_HEADS = 4
GROUP = HEADS // N_KV_HEADS
KV_WIDTH = N_KV_HEADS * HEAD
WINDOW = 128
D_FF = 4 * D_MODEL
RMS_EPS = 1e-6
GN_EPS = 6.4e-4
NEG_INF = -1e30
LORA_PAD = 128

PAD_ROWS = WINDOW - N_META
PROMPT_ROWS = PAD_ROWS + N_META + SEQ
PROMPT_TOTAL = BATCH * PROMPT_ROWS
SAMPLE_TOTAL = DEC_BATCH * DEC_SEQ
TOTAL_ROWS = PROMPT_TOTAL + SAMPLE_TOTAL

CHUNK = 64
SUB = 16
SEQ_GROUP = CHUNK // DEC_SEQ
TM_ROWS = 768
TM_DUAL = 256
TM_PREP = 320
TF = 512
VMEM_LIMIT = 56 * 1024 * 1024
SMALL_MM_PASSES = 3


def _cparams(sem):
    return pltpu.CompilerParams(dimension_semantics=sem, vmem_limit_bytes=VMEM_LIMIT)


def _rms(x, g):
    return x * lax.rsqrt(jnp.mean(x * x, axis=-1, keepdims=True) + RMS_EPS) * g


def _sigmoid(x):
    return 1.0 / (1.0 + jnp.exp(-x))


def _bdot(a, b):
    return jnp.dot(a.astype(BF16), b.astype(BF16), preferred_element_type=F32)


def _split(x):
    hi = x.astype(BF16)
    lo = (x - hi.astype(F32)).astype(BF16)
    return hi, lo


_NN = (((1,), (0,)), ((), ()))
_NT = (((1,), (1,)), ((), ()))
_TN = (((0,), (0,)), ((), ()))


def _mm(a, b, dims=_NN, passes=SMALL_MM_PASSES):
    dg = functools.partial(lax.dot_general, dimension_numbers=dims, preferred_element_type=F32)
    if passes == 1:
        return dg(a.astype(BF16), b.astype(BF16))
    ah, al = _split(a)
    bh, bl = _split(b)
    return dg(ah, bh) + (dg(ah, bl) + dg(al, bh))


def _div_pow2(x, n):
    shift = n.bit_length() - 1
    assert 1 << shift == n
    return jnp.right_shift(x, shift)


def _dot_exact_lhs(m01, x):
    hi, lo = _split(x)
    return (jnp.dot(m01, hi, preferred_element_type=F32)
            + jnp.dot(m01, lo, preferred_element_type=F32))


def _mlp_body(x_ref, g2_ref, g3_ref, wup_ref, wdn_ref, o_ref, xn_ref, acc_ref):
    f = pl.program_id(1)

    @pl.when(f == 0)
    def _():
        xn_ref[...] = _rms(x_ref[...], g2_ref[...]).astype(BF16)
        acc_ref[...] = jnp.zeros_like(acc_ref)

    h = jnp.dot(xn_ref[...], wup_ref[...], preferred_element_type=F32)
    a = jnp.maximum(h, 0.0)
    acc_ref[...] += jnp.dot((a * a).astype(BF16), wdn_ref[...], preferred_element_type=F32)

    @pl.when(f == pl.num_programs(1) - 1)
    def _():
        o_ref[...] = x_ref[...] + _rms(acc_ref[...], g3_ref[...])


def _mlp(x, g2, g3, wup, wdn):
    nt, nf = TOTAL_ROWS // TM_ROWS, D_FF // TF
    return pl.pallas_call(
        _mlp_body,
        grid=(nt, nf),
        in_specs=[
            pl.BlockSpec((TM_ROWS, D_MODEL), lambda i, f: (i, 0)),
            pl.BlockSpec((1, D_MODEL), lambda i, f: (0, 0)),
            pl.BlockSpec((1, D_MODEL), lambda i, f: (0, 0)),
            pl.BlockSpec((D_MODEL, TF), lambda i, f: (0, f)),
            pl.BlockSpec((TF, D_MODEL), lambda i, f: (f, 0)),
        ],
        out_specs=pl.BlockSpec((TM_ROWS, D_MODEL), lambda i, f: (i, 0)),
        out_shape=jax.ShapeDtypeStruct((TOTAL_ROWS, D_MODEL), F32),
        scratch_shapes=[pltpu.VMEM((TM_ROWS, D_MODEL), BF16), pltpu.VMEM((TM_ROWS, D_MODEL), F32)],
        compiler_params=_cparams(("parallel", "arbitrary")),
        name="mlp",
    )(x, g2, g3, wup, wdn)


_DUAL_PROMPT_TILES = PROMPT_TOTAL // TM_DUAL
_DUAL_TILES = TOTAL_ROWS // TM_DUAL


def _oproj_body(zp_ref, zs_ref, x_ref, w_ref, g_ref, o_ref):
    i = pl.program_id(0)

    def finish(z):
        m = jnp.dot(z, w_ref[...], preferred_element_type=F32)
        o_ref[...] = x_ref[...] + _rms(m, g_ref[...])

    @pl.when(i < _DUAL_PROMPT_TILES)
    def _():
        finish(zp_ref[...])

    @pl.when(i >= _DUAL_PROMPT_TILES)
    def _():
        finish(zs_ref[...])


def _oproj(zp, zs, x, w, g):
    return pl.pallas_call(
        _oproj_body,
        grid=(_DUAL_TILES,),
        in_specs=[
            pl.BlockSpec((TM_DUAL, D_MODEL), lambda i: (jnp.minimum(i, _DUAL_PROMPT_TILES - 1), 0)),
            pl.BlockSpec((TM_DUAL, D_MODEL), lambda i: (jnp.maximum(i - _DUAL_PROMPT_TILES, 0), 0)),
            pl.BlockSpec((TM_DUAL, D_MODEL), lambda i: (i, 0)),
            pl.BlockSpec((D_MODEL, D_MODEL), lambda i: (0, 0)),
            pl.BlockSpec((1, D_MODEL), lambda i: (0, 0)),
        ],
        out_specs=pl.BlockSpec((TM_DUAL, D_MODEL), lambda i: (i, 0)),
        out_shape=jax.ShapeDtypeStruct((TOTAL_ROWS, D_MODEL), F32),
        compiler_params=_cparams(("parallel",)),
        name="oproj",
    )(zp, zs, x, w, g)


def _qkv_body(x_ref, g_ref, w_ref, q_ref, kv_ref):
    xn = _rms(x_ref[...], g_ref[...]).astype(BF16)
    res = jnp.dot(xn, w_ref[...], preferred_element_type=F32)
    q_ref[...] = (res[:, :D_MODEL] * (HEAD ** -0.5)).astype(BF16)
    kv_ref[...] = res[:, D_MODEL:]


def _qkv(x, g, w):
    nt = TOTAL_ROWS // TM_ROWS
    return pl.pallas_call(
        _qkv_body,
        grid=(nt,),
        in_specs=[
            pl.BlockSpec((TM_ROWS, D_MODEL), lambda i: (i, 0)),
            pl.BlockSpec((1, D_MODEL), lambda i: (0, 0)),
            pl.BlockSpec((D_MODEL, D_MODEL + 2 * KV_WIDTH), lambda i: (0, 0)),
        ],
        out_specs=[
            pl.BlockSpec((TM_ROWS, D_MODEL), lambda i: (i, 0)),
            pl.BlockSpec((TM_ROWS, 2 * KV_WIDTH), lambda i: (i, 0)),
        ],
        out_shape=[
            jax.ShapeDtypeStruct((TOTAL_ROWS, D_MODEL), BF16),
            jax.ShapeDtypeStruct((TOTAL_ROWS, 2 * KV_WIDTH), F32),
        ],
        compiler_params=_cparams(("parallel",)),
        name="qkv",
    )(x, g, w)


def _alibi_slope(h):
    return 2.0 ** (-8.0 * (h + 1) / HEADS)


def _attend_heads(q, k, v, ok, mind, sink_ref, o_ref):
    for h in range(HEADS):
        kvh = h // GROUP
        qh = q[:, h * HEAD:(h + 1) * HEAD]
        kh = k[:, kvh * HEAD:(kvh + 1) * HEAD]
        vh = v[:, kvh * HEAD:(kvh + 1) * HEAD]
        s = lax.dot_general(qh, kh, _NT, preferred_element_type=F32)
        logits = jnp.where(ok, s - _alibi_slope(h) * mind, NEG_INF)
        sink = sink_ref[h]
        m = jnp.maximum(jnp.max(logits, axis=-1, keepdims=True), sink)
        p = jnp.exp(logits - m)
        den = jnp.sum(p, axis=-1, keepdims=True) + jnp.exp(sink - m)
        p = p / den
        oh = jnp.dot(p.astype(BF16), vh, preferred_element_type=F32)
        o_ref[:, h * HEAD:(h + 1) * HEAD] = oh.astype(o_ref.dtype)


_BLOCKS_PER_PROMPT = PROMPT_ROWS // WINDOW


def _attn_prompt_body(sink_ref, q_ref, kv0_ref, kvp_ref, kvc_ref, o_ref):
    i = pl.program_id(1)
    kv = jnp.concatenate([kv0_ref[...], kvp_ref[...], kvc_ref[...]], axis=0).astype(BF16)
    k, v = kv[:, :KV_WIDTH], kv[:, KV_WIDTH:]
    shp = (WINDOW, 3 * WINDOW)
    r = lax.broadcasted_iota(jnp.int32, shp, 0)
    c = lax.broadcasted_iota(jnp.int32, shp, 1)
    kblk = _div_pow2(c, WINDOW)
    cc = c - kblk * WINDOW
    qpos = jnp.where(i == 0, r - PAD_ROWS, N_META + (i - 1) * WINDOW + r)
    kpos = jnp.where(kblk == 0, cc - PAD_ROWS, N_META + (i - 3 + kblk) * WINDOW + cc)
    kmeta = kblk == 0
    kvalid = jnp.where(kmeta, cc - PAD_ROWS, jnp.where(kblk == 1, i - 2, i - 1)) >= 0
    dist = qpos - kpos
    ok = kvalid & (dist >= 0) & (kmeta | (dist < WINDOW))
    mind = jnp.minimum(dist, WINDOW).astype(F32)
    _attend_heads(q_ref[...], k, v, ok, mind, sink_ref, o_ref)


def _attn_prompt(sinks, q, kv):
    nb = _BLOCKS_PER_PROMPT
    return pl.pallas_call(
        _attn_prompt_body,
        grid=(BATCH, nb),
        in_specs=[
            pl.BlockSpec(memory_space=pltpu.SMEM),
            pl.BlockSpec((WINDOW, D_MODEL), lambda b, i: (b * nb + i, 0)),
            pl.BlockSpec((WINDOW, 2 * KV_WIDTH), lambda b, i: (b * nb, 0)),
            pl.BlockSpec((WINDOW, 2 * KV_WIDTH), lambda b, i: (b * nb + jnp.maximum(i - 1, 0), 0)),
            pl.BlockSpec((WINDOW, 2 * KV_WIDTH), lambda b, i: (b * nb + i, 0)),
        ],
        out_specs=pl.BlockSpec((WINDOW, D_MODEL), lambda b, i: (b * nb + i, 0)),
        out_shape=jax.ShapeDtypeStruct((PROMPT_TOTAL, D_MODEL), BF16),
        compiler_params=_cparams(("parallel", "parallel")),
        name="attn_prompt",
    )(sinks, q, kv, kv, kv)


_ATT_SEQS = 8
_SAMPLE_KEYS = N_META + WINDOW + DEC_SEQ


def _attn_sample_body(sink_ref, q_ref, kv_ref, mk_ref, mv_ref, wk_ref, wv_ref, o_ref, nwk_ref, nwv_ref):
    shp = (DEC_SEQ, _SAMPLE_KEYS)
    t = lax.broadcasted_iota(jnp.int32, shp, 0)
    c = lax.broadcasted_iota(jnp.int32, shp, 1)
    qpos = PAST_LEN + t
    kpos = jnp.where(c < N_META, c,
                     jnp.where(c < N_META + WINDOW, PAST_LEN - WINDOW + (c - N_META),
                               PAST_LEN + (c - N_META - WINDOW)))
    kmeta = c < N_META
    kvalid = kmeta | (kpos >= N_META)
    dist = qpos - kpos
    ok = kvalid & (dist >= 0) & (kmeta | (dist < WINDOW))
    mind = jnp.minimum(dist, WINDOW).astype(F32)
    for s in range(_ATT_SEQS):
        rows = slice(s * DEC_SEQ, (s + 1) * DEC_SEQ)
        knew = kv_ref[rows, :KV_WIDTH]
        vnew = kv_ref[rows, KV_WIDTH:]
        wk = wk_ref[s]
        wv = wv_ref[s]
        kall = jnp.concatenate([mk_ref[s], wk, knew], axis=0).astype(BF16)
        vall = jnp.concatenate([mv_ref[s], wv, vnew], axis=0).astype(BF16)
        _attend_heads(q_ref[rows, :], kall, vall, ok, mind, sink_ref, o_ref.at[rows, :])
        nwk_ref[s] = jnp.concatenate([wk[DEC_SEQ:], knew], axis=0)
        nwv_ref[s] = jnp.concatenate([wv[DEC_SEQ:], vnew], axis=0)


def _attn_sample(sinks, q, kv, cmk, cmv, cwk, cwv, layer):
    rows = _ATT_SEQS * DEC_SEQ
    off = PROMPT_TOTAL // rows
    ng = DEC_BATCH // _ATT_SEQS
    cache_spec = lambda n: pl.BlockSpec((_ATT_SEQS, None, n, KV_WIDTH), lambda g: (g, layer, 0, 0))
    return pl.pallas_call(
        _attn_sample_body,
        grid=(ng,),
        in_specs=[
            pl.BlockSpec(memory_space=pltpu.SMEM),
            pl.BlockSpec((rows, D_MODEL), lambda g: (off + g, 0)),
            pl.BlockSpec((rows, 2 * KV_WIDTH), lambda g: (off + g, 0)),
            cache_spec(N_META), cache_spec(N_META), cache_spec(WINDOW), cache_spec(WINDOW),
        ],
        out_specs=[
            pl.BlockSpec((rows, D_MODEL), lambda g: (g, 0)),
            pl.BlockSpec((_ATT_SEQS, WINDOW, KV_WIDTH), lambda g: (g, 0, 0)),
            pl.BlockSpec((_ATT_SEQS, WINDOW, KV_WIDTH), lambda g: (g, 0, 0)),
        ],
        out_shape=[
            jax.ShapeDtypeStruct((SAMPLE_TOTAL, D_MODEL), BF16),
            jax.ShapeDtypeStruct((DEC_BATCH, WINDOW, KV_WIDTH), F32),
            jax.ShapeDtypeStruct((DEC_BATCH, WINDOW, KV_WIDTH), F32),
        ],
        compiler_params=_cparams(("parallel",)),
        name="attn_sample",
    )(sinks, q, kv, cmk, cmv, cwk, cwv)


_PREP_TILES_PER_PROMPT = PROMPT_ROWS // TM_PREP
_DECAY_SCALE = 0.6065306597126334


def _prep_body(*refs, prompt, vres):
    it = iter(refs)
    x_ref = next(it)
    side_ref = next(it)
    vf_ref = next(it) if vres else None
    g0_ref, mu_ref = next(it), next(it)
    wr_ref, wk_ref, wv_ref = next(it), next(it), next(it)
    w0_ref, w1_ref, w2_ref = next(it), next(it), next(it)
    a0_ref, a1_ref, a2_ref = next(it), next(it), next(it)
    if vres:
        v0_ref, v1_ref, v2_ref = next(it), next(it), next(it)
    g1_ref, g2_ref = next(it), next(it)
    r_ref, k_ref, v_ref, a_ref, lw_ref, g_ref, xn_ref = (next(it) for _ in range(7))

    tm = x_ref.shape[0]
    g0 = g0_ref[...]
    xn = _rms(x_ref[...], g0)
    row = lax.broadcasted_iota(jnp.int32, (tm, 1), 0)
    if prompt:
        ib = lax.rem(pl.program_id(0), _PREP_TILES_PER_PROMPT)
        xn = jnp.where(ib * tm + row < PAD_ROWS, 0.0, xn)
        before = _rms(side_ref[...], g0)[7:8, :]
        before = jnp.where(ib == 0, 0.0, before)
        x_prev = jnp.where(row == 0, before, pltpu.roll(xn, 1, 0))
        xn_ref[...] = xn[tm - 8:, :]
    else:
        x_prev = jnp.where(jnp.bitwise_and(row, DEC_SEQ - 1) == 0, side_ref[...], pltpu.roll(xn, 1, 0))
        xn_ref[...] = xn
    xx = x_prev - xn
    xr, xw, xk, xv, xa, xg = (xn + xx * mu_ref[c:c + 1, :] for c in range(6))

    r_ref[...] = _bdot(xr, wr_ref[...])
    k_ref[...] = _bdot(xk, wk_ref[...])
    v = _bdot(xv, wv_ref[...])
    wl = w0_ref[...] + _bdot(jnp.tanh(_bdot(xw, w1_ref[...])), w2_ref[...])
    lw_ref[...] = -_DECAY_SCALE * _sigmoid(wl)
    if vres:
        mix = _sigmoid(v0_ref[...] + _bdot(_bdot(xv, v1_ref[...]), v2_ref[...]))
        v = v + (vf_ref[...] - v) * mix
    v_ref[...] = v
    a_ref[...] = _sigmoid(a0_ref[...] + _bdot(_bdot(xa, a1_ref[...]), a2_ref[...]))
    g_ref[...] = _bdot(_sigmoid(_bdot(xg, g1_ref[...])), g2_ref[...])


def _prep(x, side, vfirst, p, prompt):
    vres = vfirst is not None
    if prompt:
        tm, nt, off, nrows = TM_PREP, PROMPT_TOTAL // TM_PREP, 0, PROMPT_TOTAL
        side_spec = pl.BlockSpec((8, D_MODEL), lambda i: (jnp.maximum(i * (TM_PREP // 8) - 1, 0), 0))
        xn_spec = pl.BlockSpec((8, D_MODEL), lambda i: (i, 0))
        xn_shape = jax.ShapeDtypeStruct((nt * 8, D_MODEL), F32)
    else:
        tm, nt, off, nrows = TM_DUAL, SAMPLE_TOTAL // TM_DUAL, PROMPT_TOTAL // TM_DUAL, SAMPLE_TOTAL
        side_spec = pl.BlockSpec((tm, D_MODEL), lambda i: (i, 0))
        xn_spec = pl.BlockSpec((tm, D_MODEL), lambda i: (i, 0))
        xn_shape = jax.ShapeDtypeStruct((nrows, D_MODEL), F32)
    row_spec = pl.BlockSpec((tm, D_MODEL), lambda i: (i, 0))
    const = lambda a: pl.BlockSpec(a.shape, lambda i: (0,) * a.ndim)

    args = [x, side]
    specs = [pl.BlockSpec((tm, D_MODEL), lambda i: (off + i, 0)), side_spec]
    if vres:
        args.append(vfirst)
        specs.append(row_spec)
    names = ["g0", "mu", "w_r", "w_k", "w_v", "w0", "w1", "w2", "a0", "a1", "a2"]
    if vres:
        names += ["v0", "v1", "v2"]
    names += ["g1", "g2"]
    for n in names:
        args.append(p[n])
        specs.append(const(p[n]))
    big = jax.ShapeDtypeStruct((nrows, D_MODEL), F32)
    return pl.pallas_call(
        functools.partial(_prep_body, prompt=prompt, vres=vres),
        grid=(nt,),
        in_specs=specs,
        out_specs=[row_spec] * 6 + [xn_spec],
        out_shape=[big] * 6 + [xn_shape],
        compiler_params=_cparams(("parallel",)),
        name="rwkv_prep_prompt" if prompt else "rwkv_prep_sample",
    )(*args)


def _eye(n):
    r = lax.broadcasted_iota(jnp.int32, (n, n), 0)
    c = lax.broadcasted_iota(jnp.int32, (n, n), 1)
    return (r == c).astype(F32)


def _unit_lower_inverse(n_strict, sub, nblocks):
    n = n_strict.shape[0]
    eye = _eye(n)
    r = lax.broadcasted_iota(jnp.int32, (n, n), 0)
    c = lax.broadcasted_iota(jnp.int32, (n, n), 1)
    diag_blk = _div_pow2(r, sub) == _div_pow2(c, sub)
    m = jnp.where(diag_blk, -n_strict, 0.0)
    td = eye + m
    span = 1
    while 2 * span < sub:
        m = _mm(m, m)
        td = td + _mm(td, m)
        span *= 2
    if nblocks == 1:
        return td
    x = _mm(td, jnp.where(diag_blk, 0.0, n_strict))
    z = eye - x
    if nblocks > 2:
        assert nblocks <= 4
        z = z + _mm(z, _mm(x, x))
    return _mm(z, td)


def _chunk_prologue(refs, seq_len):
    r_ref, k_ref, v_ref, a_ref, lw_ref, g_ref, kk_ref, ka_ref = refs
    c = CHUNK
    row = lax.broadcasted_iota(jnp.int32, (c, c), 0)
    col = lax.broadcasted_iota(jnp.int32, (c, c), 1)
    same = _div_pow2(row, seq_len) == _div_pow2(col, seq_len)
    strict = same & (row > col)
    incl = same & (row >= col)
    lw = lw_ref[...]
    one01 = lambda m: jnp.where(m, 1.0, 0.0).astype(BF16)
    cum = _dot_exact_lhs(one01(incl), lw)
    cum_end = _dot_exact_lhs(one01(same), lw)
    k_raw = k_ref[...]
    a = a_ref[...]
    q = dict(
        strict=strict, incl=incl,
        r=r_ref[...], v=v_ref[...], a=a, g=g_ref[...],
        k=k_raw * (1.0 + (a - 1.0) * ka_ref[...]),
        kk_un=k_raw * kk_ref[...],
        e_cum=jnp.exp(cum), e_neg=jnp.exp(-cum), e_excl=jnp.exp(cum - lw),
        e_end=jnp.exp(cum_end - cum), p_end=jnp.exp(cum_end),
    )
    return q


def _head_terms(q, h, sub, nblocks):
    sl = slice(h * HEAD, (h + 1) * HEAD)
    rh, kh, vh, ah = q["r"][:, sl], q["k"][:, sl], q["v"][:, sl], q["a"][:, sl]
    kk = q["kk_un"][:, sl]
    kk = kk / jnp.maximum(jnp.sqrt(jnp.sum(kk * kk, axis=-1, keepdims=True)), 1e-12)
    bh = kk * ah
    at = kk * q["e_excl"][:, sl]
    rt = rh * q["e_cum"][:, sl]
    bt = bh * q["e_neg"][:, sl]
    kt = kh * q["e_neg"][:, sl]
    b_end = bh * q["e_end"][:, sl]
    k_end = kh * q["e_end"][:, sl]
    strict, incl = q["strict"], q["incl"]
    a_ab = jnp.where(strict, _mm(at, bt, _NT), 0.0)
    a_ak = jnp.where(strict, _mm(at, kt, _NT), 0.0)
    a_rb = jnp.where(incl, _mm(rt, bt, _NT), 0.0)
    a_rk = jnp.where(incl, _mm(rt, kt, _NT), 0.0)
    t = _unit_lower_inverse(a_ab, sub, nblocks)
    w = _mm(t, at)
    ubar = -_mm(t, _mm(a_ak, vh))
    return dict(rh=rh, kh=kh, vh=vh, rt=rt, w=w, ubar=ubar, a_rb=a_rb, a_rk=a_rk,
                b_end=b_end, k_end=k_end, p_end=q["p_end"][:, sl], g=q["g"][:, sl])


def _head_output(t, y, rk_h, lnw_h, lnb_h):
    mean = jnp.mean(y, axis=-1, keepdims=True)
    d = y - mean
    var = jnp.mean(d * d, axis=-1, keepdims=True)
    yn = d * lax.rsqrt(var + GN_EPS) * lnw_h + lnb_h
    bonus = jnp.sum(t["rh"] * t["kh"] * rk_h, axis=-1, keepdims=True) * t["vh"]
    return ((yn + bonus) * t["g"]).astype(BF16)


def _chunk_prompt_body(r_ref, k_ref, v_ref, a_ref, lw_ref, g_ref, kk_ref, ka_ref, rk_ref, lnw_ref, lnb_ref,
                       z_ref, sout_ref, s_ref):
    ci = pl.program_id(1)

    @pl.when(ci == 0)
    def _():
        s_ref[...] = jnp.zeros_like(s_ref)

    q = _chunk_prologue((r_ref, k_ref, v_ref, a_ref, lw_ref, g_ref, kk_ref, ka_ref), CHUNK)
    for h in range(HEADS):
        sl = slice(h * HEAD, (h + 1) * HEAD)
        t = _head_terms(q, h, SUB, CHUNK // SUB)
        s0 = s_ref[h]
        u = t["ubar"] - _mm(t["w"], s0, _NT)
        y = _mm(t["rt"], s0, _NT) + _mm(t["a_rb"], u) + _mm(t["a_rk"], t["vh"])
        s_ref[h] = s0 * t["p_end"][0:1, :] + _mm(u, t["b_end"], _TN) + _mm(t["vh"], t["k_end"], _TN)
        z_ref[:, sl] = _head_output(t, y, rk_ref[:, sl], lnw_ref[:, sl], lnb_ref[:, sl])

    @pl.when(ci == pl.num_programs(1) - 1)
    def _():
        sout_ref[...] = s_ref[...]


def _chunk_prompt(proj, p):
    nc = PROMPT_ROWS // CHUNK
    row_spec = pl.BlockSpec((CHUNK, D_MODEL), lambda b, c: (b * nc + c, 0))
    vec_spec = pl.BlockSpec((1, D_MODEL), lambda b, c: (0, 0))
    return pl.pallas_call(
        _chunk_prompt_body,
        grid=(BATCH, nc),
        in_specs=[row_spec] * 6 + [vec_spec] * 5,
        out_specs=[
            row_spec,
            pl.BlockSpec((None, HEADS, HEAD, HEAD), lambda b, c: (b, 0, 0, 0)),
        ],
        out_shape=[
            jax.ShapeDtypeStruct((PROMPT_TOTAL, D_MODEL), BF16),
            jax.ShapeDtypeStruct((BATCH, HEADS, HEAD, HEAD), F32),
        ],
        scratch_shapes=[pltpu.VMEM((HEADS, HEAD, HEAD), F32)],
        compiler_params=_cparams(("parallel", "arbitrary")),
        name="rwkv_chunk_prompt",
    )(*proj, p["k_k"], p["k_a"], p["r_k"], p["lnx_w"], p["lnx_b"])


def _chunk_sample_body(r_ref, k_ref, v_ref, a_ref, lw_ref, g_ref, kk_ref, ka_ref, rk_ref, lnw_ref, lnb_ref,
                       sin_ref, z_ref, sout_ref):
    q = _chunk_prologue((r_ref, k_ref, v_ref, a_ref, lw_ref, g_ref, kk_ref, ka_ref), DEC_SEQ)
    seq_of_row = _div_pow2(lax.broadcasted_iota(jnp.int32, (CHUNK, 1), 0), DEC_SEQ)
    for h in range(HEADS):
        sl = slice(h * HEAD, (h + 1) * HEAD)
        t = _head_terms(q, h, DEC_SEQ, 1)
        ws, rs = [], []
        for s in range(SEQ_GROUP):
            rows = slice(s * DEC_SEQ, (s + 1) * DEC_SEQ)
            s0 = sin_ref[s, h]
            ws.append(_mm(t["w"][rows], s0, _NT))
            rs.append(_mm(t["rt"][rows], s0, _NT))
        u = t["ubar"] - jnp.concatenate(ws, axis=0)
        y = jnp.concatenate(rs, axis=0) + _mm(t["a_rb"], u) + _mm(t["a_rk"], t["vh"])
        for s in range(SEQ_GROUP):
            mine = seq_of_row == s
            us = jnp.where(mine, u, 0.0)
            vs = jnp.where(mine, t["vh"], 0.0)
            p_end = t["p_end"][s * DEC_SEQ:s * DEC_SEQ + 1, :]
            sout_ref[s, h] = sin_ref[s, h] * p_end + _mm(us, t["b_end"], _TN) + _mm(vs, t["k_end"], _TN)
        z_ref[:, sl] = _head_output(t, y, rk_ref[:, sl], lnw_ref[:, sl], lnb_ref[:, sl])


def _chunk_sample(proj, p, state_wkv, layer):
    ng = DEC_BATCH // SEQ_GROUP
    row_spec = pl.BlockSpec((CHUNK, D_MODEL), lambda g: (g, 0))
    vec_spec = pl.BlockSpec((1, D_MODEL), lambda g: (0, 0))
    return pl.pallas_call(
        _chunk_sample_body,
        grid=(ng,),
        in_specs=[row_spec] * 6 + [vec_spec] * 5 + [
            pl.BlockSpec((SEQ_GROUP, None, HEADS, HEAD, HEAD), lambda g: (g, layer, 0, 0, 0)),
        ],
        out_specs=[
            row_spec,
            pl.BlockSpec((SEQ_GROUP, HEADS, HEAD, HEAD), lambda g: (g, 0, 0, 0)),
        ],
        out_shape=[
            jax.ShapeDtypeStruct((SAMPLE_TOTAL, D_MODEL), BF16),
            jax.ShapeDtypeStruct((DEC_BATCH, HEADS, HEAD, HEAD), F32),
        ],
        compiler_params=_cparams(("parallel",)),
        name="rwkv_chunk_sample",
    )(*proj, p["k_k"], p["k_a"], p["r_k"], p["lnx_w"], p["lnx_b"], state_wkv)


def _pad_cols(w):
    return jnp.pad(w, ((0, 0), (0, LORA_PAD - w.shape[1]))).astype(BF16)


def _pad_rows(w):
    return jnp.pad(w, ((0, LORA_PAD - w.shape[0]), (0, 0))).astype(BF16)


def kernel(x_prompt, x_sample, state_wkv, state_shift, cache_win_k, cache_win_v, cache_meta_k,
           cache_meta_v, meta_tokens, norm_gains, rwkv_mu, rwkv_w_r, rwkv_w_k, rwkv_w_v, rwkv_w_o,
           rwkv_w0, rwkv_w1, rwkv_w2, rwkv_a0, rwkv_a1, rwkv_a2, rwkv_v0, rwkv_v1, rwkv_v2,
           rwkv_g1, rwkv_g2, rwkv_k_k, rwkv_k_a, rwkv_r_k, rwkv_lnx_w, rwkv_lnx_b,
           attn_w_qkv, attn_w_o, attn_sinks, mlp_w_up, mlp_w_down):
    assert x_prompt.shape == (BATCH, SEQ, D_MODEL) and x_sample.shape == (DEC_BATCH, DEC_SEQ, D_MODEL)
    n_swa = cache_win_k.shape[1]
    row = lambda v: v.reshape(1, D_MODEL).astype(F32)

    head = jnp.concatenate([jnp.zeros((PAD_ROWS, D_MODEL), F32), meta_tokens.astype(F32)], axis=0)
    x = jnp.concatenate(
        [jnp.concatenate([jnp.broadcast_to(head[None], (BATCH, WINDOW, D_MODEL)), x_prompt], axis=1)
         .reshape(PROMPT_TOTAL, D_MODEL), x_sample.reshape(SAMPLE_TOTAL, D_MODEL)], axis=0)

    cmk = cache_meta_k.reshape(DEC_BATCH, n_swa, N_META, KV_WIDTH)
    cmv = cache_meta_v.reshape(DEC_BATCH, n_swa, N_META, KV_WIDTH)
    cwk = cache_win_k.reshape(DEC_BATCH, n_swa, WINDOW, KV_WIDTH)
    cwv = cache_win_v.reshape(DEC_BATCH, n_swa, WINDOW, KV_WIDTH)

    p_wkv, p_shift, s_wkv, s_shift = [], [], [], []
    p_wk, p_wv, p_mk, p_mv, s_wk, s_wv = [], [], [], [], [], []
    vfirst_p = vfirst_s = None
    for i in range(DEPTH):
        gains = norm_gains[i]
        j = i // 2
        if i % 2 == 0:
            p = dict(
                g0=row(gains[0]), mu=rwkv_mu[j],
                w_r=rwkv_w_r[j].astype(BF16), w_k=rwkv_w_k[j].astype(BF16), w_v=rwkv_w_v[j].astype(BF16),
                w0=row(rwkv_w0[j]), w1=_pad_cols(rwkv_w1[j]), w2=_pad_rows(rwkv_w2[j]),
                a0=row(rwkv_a0[j]), a1=_pad_cols(rwkv_a1[j]), a2=_pad_rows(rwkv_a2[j]),
                g1=_pad_cols(rwkv_g1[j]), g2=_pad_rows(rwkv_g2[j]),
                k_k=row(rwkv_k_k[j]), k_a=row(rwkv_k_a[j]), r_k=row(rwkv_r_k[j]),
                lnx_w=row(rwkv_lnx_w[j]), lnx_b=row(rwkv_lnx_b[j]),
            )
            if j > 0:
                p.update(v0=row(rwkv_v0[j - 1]), v1=_pad_cols(rwkv_v1[j - 1]), v2=_pad_rows(rwkv_v2[j - 1]))
            shift_rows = jnp.zeros((DEC_BATCH, DEC_SEQ, D_MODEL), F32).at[:, 0].set(state_shift[:, j])
            outs_p = _prep(x, x, vfirst_p if j > 0 else None, p, prompt=True)
            outs_s = _prep(x, shift_rows.reshape(SAMPLE_TOTAL, D_MODEL), vfirst_s if j > 0 else None, p,
                           prompt=False)
            if j == 0:
                vfirst_p, vfirst_s = outs_p[2], outs_s[2]
            zp, st_p = _chunk_prompt(outs_p[:6], p)
            zs, st_s = _chunk_sample(outs_s[:6], p, state_wkv, j)
            xn_tail = outs_p[6].reshape(BATCH, _PREP_TILES_PER_PROMPT, 8, D_MODEL)
            p_wkv.append(st_p)
            p_shift.append(xn_tail[:, -1, -1])
            s_wkv.append(st_s)
            s_shift.append(outs_s[6].reshape(DEC_BATCH, DEC_SEQ, D_MODEL)[:, -1])
            w_o = rwkv_w_o[j].astype(BF16)
        else:
            q, kv = _qkv(x, row(gains[0]), attn_w_qkv[j].astype(BF16))
            sinks = attn_sinks[j].astype(F32)
            zp = _attn_prompt(sinks, q, kv)
            zs, nwk, nwv = _attn_sample(sinks, q, kv, cmk, cmv, cwk, cwv, j)
            kvp = kv[:PROMPT_TOTAL].reshape(BATCH, PROMPT_ROWS, 2, N_KV_HEADS, HEAD)
            p_mk.append(kvp[:, PAD_ROWS:WINDOW, 0])
            p_mv.append(kvp[:, PAD_ROWS:WINDOW, 1])
            p_wk.append(kvp[:, -WINDOW:, 0])
            p_wv.append(kvp[:, -WINDOW:, 1])
            s_wk.append(nwk.reshape(DEC_BATCH, WINDOW, N_KV_HEADS, HEAD))
            s_wv.append(nwv.reshape(DEC_BATCH, WINDOW, N_KV_HEADS, HEAD))
            w_o = attn_w_o[j].astype(BF16)
        x = _oproj(zp, zs, x, w_o, row(gains[1]))
        x = _mlp(x, row(gains[2]), row(gains[3]), mlp_w_up[i].astype(BF16), mlp_w_down[i].astype(BF16))

    y_prompt = x[:PROMPT_TOTAL].reshape(BATCH, PROMPT_ROWS, D_MODEL)[:, WINDOW:]
    y_sample = x[PROMPT_TOTAL:].reshape(DEC_BATCH, DEC_SEQ, D_MODEL)
    st = lambda xs: jnp.stack(xs, axis=1)
    return (y_prompt, y_sample, st(p_wkv), st(p_shift), st(p_wk), st(p_wv), st(p_mk), st(p_mv),
            st(s_wkv), st(s_shift), st(s_wk), st(s_wv))
```

```python
import functools

import jax
import jax.numpy as jnp
from jax import lax
from jax.experimental import pallas as pl
from jax.experimental.pallas import tpu as pltpu

F32 = jnp.float32
BF16 = jnp.bfloat16

D_MODEL = 1024
BATCH = 2
SEQ = 8192
DEPTH = 4
DEC_BATCH = 128
DEC_SEQ = 8
PAST_LEN = 8192
N_META = 16
HEADS = 16
HEAD = 64
N_KV_HEADS = 4
GROUP = HEADS // N_KV_HEADS
KV_WIDTH = N_KV_HEADS * HEAD
WINDOW = 128
D_FF = 4 * D_MODEL
RMS_EPS = 1e-6
GN_EPS = 6.4e-4
NEG_INF = -1e30
LORA_PAD = 128

PAD_ROWS = WINDOW - N_META
PROMPT_ROWS = PAD_ROWS + N_META + SEQ
PROMPT_TOTAL = BATCH * PROMPT_ROWS
SAMPLE_TOTAL = DEC_BATCH * DEC_SEQ
TOTAL_ROWS = PROMPT_TOTAL + SAMPLE_TOTAL

CHUNK = 64
SUB = 16
SEQ_GROUP = CHUNK // DEC_SEQ
TM_ROWS = 768
TM_DUAL = 256
TM_PREP = 320
TF = 512
VMEM_LIMIT = 56 * 1024 * 1024
SMALL_MM_PASSES = 3


def _cparams(sem):
    return pltpu.CompilerParams(dimension_semantics=sem, vmem_limit_bytes=VMEM_LIMIT)


def _rms(x, g):
    return x * lax.rsqrt(jnp.mean(x * x, axis=-1, keepdims=True) + RMS_EPS) * g


def _sigmoid(x):
    return 1.0 / (1.0 + jnp.exp(-x))


def _bdot(a, b):
    return jnp.dot(a.astype(BF16), b.astype(BF16), preferred_element_type=F32)


def _split(x):
    hi = x.astype(BF16)
    lo = (x - hi.astype(F32)).astype(BF16)
    return hi, lo


_NN = (((1,), (0,)), ((), ()))
_NT = (((1,), (1,)), ((), ()))
_TN = (((0,), (0,)), ((), ()))


def _mm(a, b, dims=_NN, passes=SMALL_MM_PASSES):
    dg = functools.partial(lax.dot_general, dimension_numbers=dims, preferred_element_type=F32)
    if passes == 1:
        return dg(a.astype(BF16), b.astype(BF16))
    ah, al = _split(a)
    bh, bl = _split(b)
    return dg(ah, bh) + (dg(ah, bl) + dg(al, bh))


def _div_pow2(x, n):
    shift = n.bit_length() - 1
    assert 1 << shift == n
    return jnp.right_shift(x, shift)


def _dot_exact_lhs(m01, x):
    hi, lo = _split(x)
    return (jnp.dot(m01, hi, preferred_element_type=F32)
            + jnp.dot(m01, lo, preferred_element_type=F32))


def _mlp_body(x_ref, g2_ref, g3_ref, wup_ref, wdn_ref, o_ref, xn_ref, acc_ref):
    f = pl.program_id(1)

    @pl.when(f == 0)
    def _():
        xn_ref[...] = _rms(x_ref[...], g2_ref[...]).astype(BF16)
        acc_ref[...] = jnp.zeros_like(acc_ref)

    h = jnp.dot(xn_ref[...], wup_ref[...], preferred_element_type=F32)
    a = jnp.maximum(h, 0.0)
    acc_ref[...] += jnp.dot((a * a).astype(BF16), wdn_ref[...], preferred_element_type=F32)

    @pl.when(f == pl.num_programs(1) - 1)
    def _():
        o_ref[...] = x_ref[...] + _rms(acc_ref[...], g3_ref[...])


def _mlp(x, g2, g3, wup, wdn):
    nt, nf = TOTAL_ROWS // TM_ROWS, D_FF // TF
    return pl.pallas_call(
        _mlp_body,
        grid=(nt, nf),
        in_specs=[
            pl.BlockSpec((TM_ROWS, D_MODEL), lambda i, f: (i, 0)),
            pl.BlockSpec((1, D_MODEL), lambda i, f: (0, 0)),
            pl.BlockSpec((1, D_MODEL), lambda i, f: (0, 0)),
            pl.BlockSpec((D_MODEL, TF), lambda i, f: (0, f)),
            pl.BlockSpec((TF, D_MODEL), lambda i, f: (f, 0)),
        ],
        out_specs=pl.BlockSpec((TM_ROWS, D_MODEL), lambda i, f: (i, 0)),
        out_shape=jax.ShapeDtypeStruct((TOTAL_ROWS, D_MODEL), F32),
        scratch_shapes=[pltpu.VMEM((TM_ROWS, D_MODEL), BF16), pltpu.VMEM((TM_ROWS, D_MODEL), F32)],
        compiler_params=_cparams(("parallel", "arbitrary")),
        name="mlp",
    )(x, g2, g3, wup, wdn)


_DUAL_PROMPT_TILES = PROMPT_TOTAL // TM_DUAL
_DUAL_TILES = TOTAL_ROWS // TM_DUAL


def _oproj_body(zp_ref, zs_ref, x_ref, w_ref, g_ref, o_ref):
    i = pl.program_id(0)

    def finish(z):
        m = jnp.dot(z, w_ref[...], preferred_element_type=F32)
        o_ref[...] = x_ref[...] + _rms(m, g_ref[...])

    @pl.when(i < _DUAL_PROMPT_TILES)
    def _():
        finish(zp_ref[...])

    @pl.when(i >= _DUAL_PROMPT_TILES)
    def _():
        finish(zs_ref[...])


def _oproj(zp, zs, x, w, g):
    return pl.pallas_call(
        _oproj_body,
        grid=(_DUAL_TILES,),
        in_specs=[
            pl.BlockSpec((TM_DUAL, D_MODEL), lambda i: (jnp.minimum(i, _DUAL_PROMPT_TILES - 1), 0)),
            pl.BlockSpec((TM_DUAL, D_MODEL), lambda i: (jnp.maximum(i - _DUAL_PROMPT_TILES, 0), 0)),
            pl.BlockSpec((TM_DUAL, D_MODEL), lambda i: (i, 0)),
            pl.BlockSpec((D_MODEL, D_MODEL), lambda i: (0, 0)),
            pl.BlockSpec((1, D_MODEL), lambda i: (0, 0)),
        ],
        out_specs=pl.BlockSpec((TM_DUAL, D_MODEL), lambda i: (i, 0)),
        out_shape=jax.ShapeDtypeStruct((TOTAL_ROWS, D_MODEL), F32),
        compiler_params=_cparams(("parallel",)),
        name="oproj",
    )(zp, zs, x, w, g)


def _qkv_body(x_ref, g_ref, w_ref, q_ref, kv_ref):
    xn = _rms(x_ref[...], g_ref[...]).astype(BF16)
    res = jnp.dot(xn, w_ref[...], preferred_element_type=F32)
    q_ref[...] = (res[:, :D_MODEL] * (HEAD ** -0.5)).astype(BF16)
    kv_ref[...] = res[:, D_MODEL:]


def _qkv(x, g, w):
    nt = TOTAL_ROWS // TM_ROWS
    return pl.pallas_call(
        _qkv_body,
        grid=(nt,),
        in_specs=[
            pl.BlockSpec((TM_ROWS, D_MODEL), lambda i: (i, 0)),
            pl.BlockSpec((1, D_MODEL), lambda i: (0, 0)),
            pl.BlockSpec((D_MODEL, D_MODEL + 2 * KV_WIDTH), lambda i: (0, 0)),
        ],
        out_specs=[
            pl.BlockSpec((TM_ROWS, D_MODEL), lambda i: (i, 0)),
            pl.BlockSpec((TM_ROWS, 2 * KV_WIDTH), lambda i: (i, 0)),
        ],
        out_shape=[
            jax.ShapeDtypeStruct((TOTAL_ROWS, D_MODEL), BF16),
            jax.ShapeDtypeStruct((TOTAL_ROWS, 2 * KV_WIDTH), F32),
        ],
        compiler_params=_cparams(("parallel",)),
        name="qkv",
    )(x, g, w)


def _alibi_slope(h):
    return 2.0 ** (-8.0 * (h + 1) / HEADS)


def _group_queries(q, kvh):
    return jnp.concatenate([q[:, h * HEAD:(h + 1) * HEAD] for h in range(kvh * GROUP, (kvh + 1) * GROUP)], axis=0)


def _group_columns(sink_ref, kvh, tq):
    g = _div_pow2(lax.broadcasted_iota(jnp.int32, (GROUP * tq, 1), 0), tq)
    slope = jnp.zeros((GROUP * tq, 1), F32)
    sink = jnp.zeros((GROUP * tq, 1), F32)
    for i in range(GROUP):
        h = kvh * GROUP + i
        slope = jnp.where(g == i, _alibi_slope(h), slope)
        sink = jnp.where(g == i, sink_ref[h], sink)
    return slope, sink


def _attend_groups(qs, ks, vs, cols, ok, mind):
    scores = [lax.dot_general(q, k, _NT, preferred_element_type=F32) for q, k in zip(qs, ks)]
    probs = []
    for s, (slope, sink) in zip(scores, cols):
        logits = jnp.where(ok, s - slope * mind, NEG_INF)
        m = jnp.maximum(jnp.max(logits, axis=-1, keepdims=True), sink)
        p = jnp.exp(logits - m)
        den = jnp.sum(p, axis=-1, keepdims=True) + jnp.exp(sink - m)
        probs.append((p / den).astype(BF16))
    return [jnp.dot(p, v, preferred_element_type=F32) for p, v in zip(probs, vs)]


def _store_group(o_ref, rows, kvh, o, tq):
    for i in range(GROUP):
        h = kvh * GROUP + i
        o_ref[rows, h * HEAD:(h + 1) * HEAD] = o[i * tq:(i + 1) * tq].astype(o_ref.dtype)


_BLOCKS_PER_PROMPT = PROMPT_ROWS // WINDOW


def _attn_prompt_body(sink_ref, q_ref, kv0_ref, kvp_ref, kvc_ref, o_ref):
    i = pl.program_id(1)
    kv = jnp.concatenate([kv0_ref[...], kvp_ref[...], kvc_ref[...]], axis=0).astype(BF16)
    shp = (GROUP * WINDOW, 3 * WINDOW)
    r = jnp.bitwise_and(lax.broadcasted_iota(jnp.int32, shp, 0), WINDOW - 1)
    c = lax.broadcasted_iota(jnp.int32, shp, 1)
    kblk = _div_pow2(c, WINDOW)
    cc = c - kblk * WINDOW
    qpos = jnp.where(i == 0, r - PAD_ROWS, N_META + (i - 1) * WINDOW + r)
    kpos = jnp.where(kblk == 0, cc - PAD_ROWS, N_META + (i - 3 + kblk) * WINDOW + cc)
    kmeta = kblk == 0
    kvalid = jnp.where(kmeta, cc - PAD_ROWS, jnp.where(kblk == 1, i - 2, i - 1)) >= 0
    dist = qpos - kpos
    ok = kvalid & (dist >= 0) & (kmeta | (dist < WINDOW))
    mind = jnp.minimum(dist, WINDOW).astype(F32)
    q = q_ref[...]
    kvhs = range(N_KV_HEADS)
    outs = _attend_groups(
        [_group_queries(q, j) for j in kvhs],
        [kv[:, j * HEAD:(j + 1) * HEAD] for j in kvhs],
        [kv[:, KV_WIDTH + j * HEAD:KV_WIDTH + (j + 1) * HEAD] for j in kvhs],
        [_group_columns(sink_ref, j, WINDOW) for j in kvhs], ok, mind)
    for j in kvhs:
        _store_group(o_ref, slice(None), j, outs[j], WINDOW)


def _attn_prompt(sinks, q, kv):
    nb = _BLOCKS_PER_PROMPT
    return pl.pallas_call(
        _attn_prompt_body,
        grid=(BATCH, nb),
        in_specs=[
            pl.BlockSpec(memory_space=pltpu.SMEM),
            pl.BlockSpec((WINDOW, D_MODEL), lambda b, i: (b * nb + i, 0)),
            pl.BlockSpec((WINDOW, 2 * KV_WIDTH), lambda b, i: (b * nb, 0)),
            pl.BlockSpec((WINDOW, 2 * KV_WIDTH), lambda b, i: (b * nb + jnp.maximum(i - 1, 0), 0)),
            pl.BlockSpec((WINDOW, 2 * KV_WIDTH), lambda b, i: (b * nb + i, 0)),
        ],
        out_specs=pl.BlockSpec((WINDOW, D_MODEL), lambda b, i: (b * nb + i, 0)),
        out_shape=jax.ShapeDtypeStruct((PROMPT_TOTAL, D_MODEL), BF16),
        compiler_params=_cparams(("parallel", "parallel")),
        name="attn_prompt",
    )(sinks, q, kv, kv, kv)


_ATT_SEQS = 8
_SAMPLE_KEYS = N_META + WINDOW + DEC_SEQ


def _attn_sample_body(sink_ref, q_ref, kv_ref, mk_ref, mv_ref, wk_ref, wv_ref, o_ref, nwk_ref, nwv_ref):
    shp = (GROUP * DEC_SEQ, _SAMPLE_KEYS)
    t = jnp.bitwise_and(lax.broadcasted_iota(jnp.int32, shp, 0), DEC_SEQ - 1)
    c = lax.broadcasted_iota(jnp.int32, shp, 1)
    qpos = PAST_LEN + t
    kpos = jnp.where(c < N_META, c,
                     jnp.where(c < N_META + WINDOW, PAST_LEN - WINDOW + (c - N_META),
                               PAST_LEN + (c - N_META - WINDOW)))
    kmeta = c < N_META
    kvalid = kmeta | (kpos >= N_META)
    dist = qpos - kpos
    ok = kvalid & (dist >= 0) & (kmeta | (dist < WINDOW))
    mind = jnp.minimum(dist, WINDOW).astype(F32)
    kvhs = range(N_KV_HEADS)
    cols = [_group_columns(sink_ref, j, DEC_SEQ) for j in kvhs]
    qs, ks, vs, cs = [], [], [], []
    for s in range(_ATT_SEQS):
        rows = slice(s * DEC_SEQ, (s + 1) * DEC_SEQ)
        knew = kv_ref[rows, :KV_WIDTH]
        vnew = kv_ref[rows, KV_WIDTH:]
        wk = wk_ref[s]
        wv = wv_ref[s]
        nwk_ref[s] = jnp.concatenate([wk[DEC_SEQ:], knew], axis=0)
        nwv_ref[s] = jnp.concatenate([wv[DEC_SEQ:], vnew], axis=0)
        kall = jnp.concatenate([mk_ref[s], wk, knew], axis=0).astype(BF16)
        vall = jnp.concatenate([mv_ref[s], wv, vnew], axis=0).astype(BF16)
        q = q_ref[rows, :]
        for j in kvhs:
            qs.append(_group_queries(q, j))
            ks.append(kall[:, j * HEAD:(j + 1) * HEAD])
            vs.append(vall[:, j * HEAD:(j + 1) * HEAD])
            cs.append(cols[j])
    outs = _attend_groups(qs, ks, vs, cs, ok, mind)
    for s in range(_ATT_SEQS):
        for j in kvhs:
            _store_group(o_ref, slice(s * DEC_SEQ, (s + 1) * DEC_SEQ), j, outs[s * N_KV_HEADS + j], DEC_SEQ)


def _attn_sample(sinks, q, kv, cmk, cmv, cwk, cwv, layer):
    rows = _ATT_SEQS * DEC_SEQ
    off = PROMPT_TOTAL // rows
    ng = DEC_BATCH // _ATT_SEQS
    cache_spec = lambda n: pl.BlockSpec((_ATT_SEQS, None, n, KV_WIDTH), lambda g: (g, layer, 0, 0))
    return pl.pallas_call(
        _attn_sample_body,
        grid=(ng,),
        in_specs=[
            pl.BlockSpec(memory_space=pltpu.SMEM),
            pl.BlockSpec((rows, D_MODEL), lambda g: (off + g, 0)),
            pl.BlockSpec((rows, 2 * KV_WIDTH), lambda g: (off + g, 0)),
            cache_spec(N_META), cache_spec(N_META), cache_spec(WINDOW), cache_spec(WINDOW),
        ],
        out_specs=[
            pl.BlockSpec((rows, D_MODEL), lambda g: (g, 0)),
            pl.BlockSpec((_ATT_SEQS, WINDOW, KV_WIDTH), lambda g: (g, 0, 0)),
            pl.BlockSpec((_ATT_SEQS, WINDOW, KV_WIDTH), lambda g: (g, 0, 0)),
        ],
        out_shape=[
            jax.ShapeDtypeStruct((SAMPLE_TOTAL, D_MODEL), BF16),
            jax.ShapeDtypeStruct((DEC_BATCH, WINDOW, KV_WIDTH), F32),
            jax.ShapeDtypeStruct((DEC_BATCH, WINDOW, KV_WIDTH), F32),
        ],
        compiler_params=_cparams(("parallel",)),
        name="attn_sample",
    )(sinks, q, kv, cmk, cmv, cwk, cwv)


_PREP_TILES_PER_PROMPT = PROMPT_ROWS // TM_PREP
_DECAY_SCALE = 0.6065306597126334


def _prep_body(*refs, prompt, vres):
    it = iter(refs)
    x_ref = next(it)
    side_ref = next(it)
    vf_ref = next(it) if vres else None
    g0_ref, mu_ref = next(it), next(it)
    wr_ref, wk_ref, wv_ref = next(it), next(it), next(it)
    w0_ref, w1_ref, w2_ref = next(it), next(it), next(it)
    a0_ref, a1_ref, a2_ref = next(it), next(it), next(it)
    if vres:
        v0_ref, v1_ref, v2_ref = next(it), next(it), next(it)
    g1_ref, g2_ref = next(it), next(it)
    r_ref, k_ref, v_ref, a_ref, lw_ref, g_ref, xn_ref = (next(it) for _ in range(7))

    tm = x_ref.shape[0]
    g0 = g0_ref[...]
    xn = _rms(x_ref[...], g0)
    row = lax.broadcasted_iota(jnp.int32, (tm, 1), 0)
    if prompt:
        ib = lax.rem(pl.program_id(0), _PREP_TILES_PER_PROMPT)
        xn = jnp.where(ib * tm + row < PAD_ROWS, 0.0, xn)
        before = _rms(side_ref[...], g0)[7:8, :]
        before = jnp.where(ib == 0, 0.0, before)
        x_prev = jnp.where(row == 0, before, pltpu.roll(xn, 1, 0))
        xn_ref[...] = xn[tm - 8:, :]
    else:
        x_prev = jnp.where(jnp.bitwise_and(row, DEC_SEQ - 1) == 0, side_ref[...], pltpu.roll(xn, 1, 0))
        xn_ref[...] = xn
    xx = x_prev - xn
    xr, xw, xk, xv, xa, xg = (xn + xx * mu_ref[c:c + 1, :] for c in range(6))

    r_ref[...] = _bdot(xr, wr_ref[...])
    k_ref[...] = _bdot(xk, wk_ref[...])
    v = _bdot(xv, wv_ref[...])
    wl = w0_ref[...] + _bdot(jnp.tanh(_bdot(xw, w1_ref[...])), w2_ref[...])
    lw_ref[...] = -_DECAY_SCALE * _sigmoid(wl)
    if vres:
        mix = _sigmoid(v0_ref[...] + _bdot(_bdot(xv, v1_ref[...]), v2_ref[...]))
        v = v + (vf_ref[...] - v) * mix
    v_ref[...] = v
    a_ref[...] = _sigmoid(a0_ref[...] + _bdot(_bdot(xa, a1_ref[...]), a2_ref[...]))
    g_ref[...] = _bdot(_sigmoid(_bdot(xg, g1_ref[...])), g2_ref[...])


def _prep(x, side, vfirst, p, prompt):
    vres = vfirst is not None
    if prompt:
        tm, nt, off, nrows = TM_PREP, PROMPT_TOTAL // TM_PREP, 0, PROMPT_TOTAL
        side_spec = pl.BlockSpec((8, D_MODEL), lambda i: (jnp.maximum(i * (TM_PREP // 8) - 1, 0), 0))
        xn_spec = pl.BlockSpec((8, D_MODEL), lambda i: (i, 0))
        xn_shape = jax.ShapeDtypeStruct((nt * 8, D_MODEL), F32)
    else:
        tm, nt, off, nrows = TM_DUAL, SAMPLE_TOTAL // TM_DUAL, PROMPT_TOTAL // TM_DUAL, SAMPLE_TOTAL
        side_spec = pl.BlockSpec((tm, D_MODEL), lambda i: (i, 0))
        xn_spec = pl.BlockSpec((tm, D_MODEL), lambda i: (i, 0))
        xn_shape = jax.ShapeDtypeStruct((nrows, D_MODEL), F32)
    row_spec = pl.BlockSpec((tm, D_MODEL), lambda i: (i, 0))
    const = lambda a: pl.BlockSpec(a.shape, lambda i: (0,) * a.ndim)

    args = [x, side]
    specs = [pl.BlockSpec((tm, D_MODEL), lambda i: (off + i, 0)), side_spec]
    if vres:
        args.append(vfirst)
        specs.append(row_spec)
    names = ["g0", "mu", "w_r", "w_k", "w_v", "w0", "w1", "w2", "a0", "a1", "a2"]
    if vres:
        names += ["v0", "v1", "v2"]
    names += ["g1", "g2"]
    for n in names:
        args.append(p[n])
        specs.append(const(p[n]))
    big = jax.ShapeDtypeStruct((nrows, D_MODEL), F32)
    return pl.pallas_call(
        functools.partial(_prep_body, prompt=prompt, vres=vres),
        grid=(nt,),
        in_specs=specs,
        out_specs=[row_spec] * 6 + [xn_spec],
        out_shape=[big] * 6 + [xn_shape],
        compiler_params=_cparams(("parallel",)),
        name="rwkv_prep_prompt" if prompt else "rwkv_prep_sample",
    )(*args)


def _eye(n):
    r = lax.broadcasted_iota(jnp.int32, (n, n), 0)
    c = lax.broadcasted_iota(jnp.int32, (n, n), 1)
    return (r == c).astype(F32)


def _mms(a_list, b_list, dims=_NN):
    return [_mm(a, b, dims) for a, b in zip(a_list, b_list)]


def _unit_lower_inverse(n_list, sub, nblocks):
    n = n_list[0].shape[0]
    eye = _eye(n)
    r = lax.broadcasted_iota(jnp.int32, (n, n), 0)
    c = lax.broadcasted_iota(jnp.int32, (n, n), 1)
    diag_blk = _div_pow2(r, sub) == _div_pow2(c, sub)
    ms = [jnp.where(diag_blk, -x, 0.0) for x in n_list]
    tds = [eye + m for m in ms]
    span = 1
    while 2 * span < sub:
        ms = _mms(ms, ms)
        tds = [td + p for td, p in zip(tds, _mms(tds, ms))]
        span *= 2
    if nblocks == 1:
        return tds
    xs = _mms(tds, [jnp.where(diag_blk, 0.0, x) for x in n_list])
    zs = [eye - x for x in xs]
    if nblocks > 2:
        assert nblocks <= 4
        zs = [z + p for z, p in zip(zs, _mms(zs, _mms(xs, xs)))]
    return _mms(zs, tds)


def _chunk_prologue(refs, seq_len):
    r_ref, k_ref, v_ref, a_ref, lw_ref, g_ref, kk_ref, ka_ref = refs
    c = CHUNK
    row = lax.broadcasted_iota(jnp.int32, (c, c), 0)
    col = lax.broadcasted_iota(jnp.int32, (c, c), 1)
    same = _div_pow2(row, seq_len) == _div_pow2(col, seq_len)
    strict = same & (row > col)
    incl = same & (row >= col)
    lw = lw_ref[...]
    one01 = lambda m: jnp.where(m, 1.0, 0.0).astype(BF16)
    cum = _dot_exact_lhs(one01(incl), lw)
    cum_end = _dot_exact_lhs(one01(same), lw)
    k_raw = k_ref[...]
    a = a_ref[...]
    q = dict(
        strict=strict, incl=incl,
        r=r_ref[...], v=v_ref[...], a=a, g=g_ref[...],
        k=k_raw * (1.0 + (a - 1.0) * ka_ref[...]),
        kk_un=k_raw * kk_ref[...],
        e_cum=jnp.exp(cum), e_neg=jnp.exp(-cum), e_excl=jnp.exp(cum - lw),
        e_end=jnp.exp(cum_end - cum), p_end=jnp.exp(cum_end),
    )
    return q


def _head_terms(q, sub, nblocks):
    hs = range(HEADS)
    col = lambda name: [q[name][:, h * HEAD:(h + 1) * HEAD] for h in hs]
    rh, kh, vh, ah, kk = col("r"), col("k"), col("v"), col("a"), col("kk_un")
    e_excl, e_cum, e_neg, e_end = col("e_excl"), col("e_cum"), col("e_neg"), col("e_end")
    kk = [x / jnp.maximum(jnp.sqrt(jnp.sum(x * x, axis=-1, keepdims=True)), 1e-12) for x in kk]
    bh = [kk[h] * ah[h] for h in hs]
    at = [kk[h] * e_excl[h] for h in hs]
    rt = [rh[h] * e_cum[h] for h in hs]
    bt = [bh[h] * e_neg[h] for h in hs]
    kt = [kh[h] * e_neg[h] for h in hs]
    b_end = [bh[h] * e_end[h] for h in hs]
    k_end = [kh[h] * e_end[h] for h in hs]
    strict, incl = q["strict"], q["incl"]
    a_ab = [jnp.where(strict, x, 0.0) for x in _mms(at, bt, _NT)]
    a_ak = [jnp.where(strict, x, 0.0) for x in _mms(at, kt, _NT)]
    a_rb = [jnp.where(incl, x, 0.0) for x in _mms(rt, bt, _NT)]
    a_rk = [jnp.where(incl, x, 0.0) for x in _mms(rt, kt, _NT)]
    t = _unit_lower_inverse(a_ab, sub, nblocks)
    w = _mms(t, at)
    ubar = [-x for x in _mms(t, _mms(a_ak, vh))]
    return dict(rh=rh, kh=kh, vh=vh, rt=rt, w=w, ubar=ubar, a_rb=a_rb, a_rk=a_rk,
                b_end=b_end, k_end=k_end, p_end=col("p_end"), g=col("g"))


def _head_output(t, h, y, rk_ref, lnw_ref, lnb_ref, z_ref):
    sl = slice(h * HEAD, (h + 1) * HEAD)
    mean = jnp.mean(y, axis=-1, keepdims=True)
    d = y - mean
    var = jnp.mean(d * d, axis=-1, keepdims=True)
    yn = d * lax.rsqrt(var + GN_EPS) * lnw_ref[:, sl] + lnb_ref[:, sl]
    bonus = jnp.sum(t["rh"][h] * t["kh"][h] * rk_ref[:, sl], axis=-1, keepdims=True) * t["vh"][h]
    z_ref[:, sl] = ((yn + bonus) * t["g"][h]).astype(BF16)


def _chunk_prompt_body(r_ref, k_ref, v_ref, a_ref, lw_ref, g_ref, kk_ref, ka_ref, rk_ref, lnw_ref, lnb_ref,
                       z_ref, sout_ref, s_ref):
    ci = pl.program_id(1)

    @pl.when(ci == 0)
    def _():
        s_ref[...] = jnp.zeros_like(s_ref)

    q = _chunk_prologue((r_ref, k_ref, v_ref, a_ref, lw_ref, g_ref, kk_ref, ka_ref), CHUNK)
    hs = range(HEADS)
    t = _head_terms(q, SUB, CHUNK // SUB)
    s0 = [s_ref[h] for h in hs]
    ws = _mms(t["w"], s0, _NT)
    rs = _mms(t["rt"], s0, _NT)
    yv = _mms(t["a_rk"], t["vh"])
    sv = _mms(t["vh"], t["k_end"], _TN)
    u = [t["ubar"][h] - ws[h] for h in hs]
    yu = _mms(t["a_rb"], u)
    su = _mms(u, t["b_end"], _TN)
    for h in hs:
        s_ref[h] = s0[h] * t["p_end"][h][0:1, :] + su[h] + sv[h]
        _head_output(t, h, rs[h] + yu[h] + yv[h], rk_ref, lnw_ref, lnb_ref, z_ref)

    @pl.when(ci == pl.num_programs(1) - 1)
    def _():
        sout_ref[...] = s_ref[...]


def _chunk_prompt(proj, p):
    nc = PROMPT_ROWS // CHUNK
    row_spec = pl.BlockSpec((CHUNK, D_MODEL), lambda b, c: (b * nc + c, 0))
    vec_spec = pl.BlockSpec((1, D_MODEL), lambda b, c: (0, 0))
    return pl.pallas_call(
        _chunk_prompt_body,
        grid=(BATCH, nc),
        in_specs=[row_spec] * 6 + [vec_spec] * 5,
        out_specs=[
            row_spec,
            pl.BlockSpec((None, HEADS, HEAD, HEAD), lambda b, c: (b, 0, 0, 0)),
        ],
        out_shape=[
            jax.ShapeDtypeStruct((PROMPT_TOTAL, D_MODEL), BF16),
            jax.ShapeDtypeStruct((BATCH, HEADS, HEAD, HEAD), F32),
        ],
        scratch_shapes=[pltpu.VMEM((HEADS, HEAD, HEAD), F32)],
        compiler_params=_cparams(("parallel", "arbitrary")),
        name="rwkv_chunk_prompt",
    )(*proj, p["k_k"], p["k_a"], p["r_k"], p["lnx_w"], p["lnx_b"])


def _chunk_sample_body(r_ref, k_ref, v_ref, a_ref, lw_ref, g_ref, kk_ref, ka_ref, rk_ref, lnw_ref, lnb_ref,
                       sin_ref, z_ref, sout_ref):
    q = _chunk_prologue((r_ref, k_ref, v_ref, a_ref, lw_ref, g_ref, kk_ref, ka_ref), DEC_SEQ)
    seq_of_row = _div_pow2(lax.broadcasted_iota(jnp.int32, (CHUNK, 1), 0), DEC_SEQ)
    hs = range(HEADS)
    seqs = range(SEQ_GROUP)
    rows = [slice(s * DEC_SEQ, (s + 1) * DEC_SEQ) for s in seqs]
    t = _head_terms(q, DEC_SEQ, 1)
    yv = _mms(t["a_rk"], t["vh"])
    ws = [jnp.concatenate([_mm(t["w"][h][rows[s]], sin_ref[s, h], _NT) for s in seqs], axis=0) for h in hs]
    rs = [jnp.concatenate([_mm(t["rt"][h][rows[s]], sin_ref[s, h], _NT) for s in seqs], axis=0) for h in hs]
    u = [t["ubar"][h] - ws[h] for h in hs]
    yu = _mms(t["a_rb"], u)
    for h in hs:
        for s in seqs:
            mine = seq_of_row == s
            us = jnp.where(mine, u[h], 0.0)
            vs = jnp.where(mine, t["vh"][h], 0.0)
            p_end = t["p_end"][h][s * DEC_SEQ:s * DEC_SEQ + 1, :]
            sout_ref[s, h] = (sin_ref[s, h] * p_end + _mm(us, t["b_end"][h], _TN)
                              + _mm(vs, t["k_end"][h], _TN))
        _head_output(t, h, rs[h] + yu[h] + yv[h], rk_ref, lnw_ref, lnb_ref, z_ref)


def _chunk_sample(proj, p, state_wkv, layer):
    ng = DEC_BATCH // SEQ_GROUP
    row_spec = pl.BlockSpec((CHUNK, D_MODEL), lambda g: (g, 0))
    vec_spec = pl.BlockSpec((1, D_MODEL), lambda g: (0, 0))
    return pl.pallas_call(
        _chunk_sample_body,
        grid=(ng,),
        in_specs=[row_spec] * 6 + [vec_spec] * 5 + [
            pl.BlockSpec((SEQ_GROUP, None, HEADS, HEAD, HEAD), lambda g: (g, layer, 0, 0, 0)),
        ],
        out_specs=[
            row_spec,
            pl.BlockSpec((SEQ_GROUP, HEADS, HEAD, HEAD), lambda g: (g, 0, 0, 0)),
        ],
        out_shape=[
            jax.ShapeDtypeStruct((SAMPLE_TOTAL, D_MODEL), BF16),
            jax.ShapeDtypeStruct((DEC_BATCH, HEADS, HEAD, HEAD), F32),
        ],
        compiler_params=_cparams(("parallel",)),
        name="rwkv_chunk_sample",
    )(*proj, p["k_k"], p["k_a"], p["r_k"], p["lnx_w"], p["lnx_b"], state_wkv)


def _pad_cols(w):
    return jnp.pad(w, ((0, 0), (0, LORA_PAD - w.shape[1]))).astype(BF16)


def _pad_rows(w):
    return jnp.pad(w, ((0, LORA_PAD - w.shape[0]), (0, 0))).astype(BF16)


def kernel(x_prompt, x_sample, state_wkv, state_shift, cache_win_k, cache_win_v, cache_meta_k,
           cache_meta_v, meta_tokens, norm_gains, rwkv_mu, rwkv_w_r, rwkv_w_k, rwkv_w_v, rwkv_w_o,
           rwkv_w0, rwkv_w1, rwkv_w2, rwkv_a0, rwkv_a1, rwkv_a2, rwkv_v0, rwkv_v1, rwkv_v2,
           rwkv_g1, rwkv_g2, rwkv_k_k, rwkv_k_a, rwkv_r_k, rwkv_lnx_w, rwkv_lnx_b,
           attn_w_qkv, attn_w_o, attn_sinks, mlp_w_up, mlp_w_down):
    assert x_prompt.shape == (BATCH, SEQ, D_MODEL) and x_sample.shape == (DEC_BATCH, DEC_SEQ, D_MODEL)
    n_swa = cache_win_k.shape[1]
    row = lambda v: v.reshape(1, D_MODEL).astype(F32)

    head = jnp.concatenate([jnp.zeros((PAD_ROWS, D_MODEL), F32), meta_tokens.astype(F32)], axis=0)
    x = jnp.concatenate(
        [jnp.concatenate([jnp.broadcast_to(head[None], (BATCH, WINDOW, D_MODEL)), x_prompt], axis=1)
         .reshape(PROMPT_TOTAL, D_MODEL), x_sample.reshape(SAMPLE_TOTAL, D_MODEL)], axis=0)

    cmk = cache_meta_k.reshape(DEC_BATCH, n_swa, N_META, KV_WIDTH)
    cmv = cache_meta_v.reshape(DEC_BATCH, n_swa, N_META, KV_WIDTH)
    cwk = cache_win_k.reshape(DEC_BATCH, n_swa, WINDOW, KV_WIDTH)
    cwv = cache_win_v.reshape(DEC_BATCH, n_swa, WINDOW, KV_WIDTH)

    p_wkv, p_shift, s_wkv, s_shift = [], [], [], []
    p_wk, p_wv, p_mk, p_mv, s_wk, s_wv = [], [], [], [], [], []
    vfirst_p = vfirst_s = None
    for i in range(DEPTH):
        gains = norm_gains[i]
        j = i // 2
        if i % 2 == 0:
            p = dict(
                g0=row(gains[0]), mu=rwkv_mu[j],
                w_r=rwkv_w_r[j].astype(BF16), w_k=rwkv_w_k[j].astype(BF16), w_v=rwkv_w_v[j].astype(BF16),
                w0=row(rwkv_w0[j]), w1=_pad_cols(rwkv_w1[j]), w2=_pad_rows(rwkv_w2[j]),
                a0=row(rwkv_a0[j]), a1=_pad_cols(rwkv_a1[j]), a2=_pad_rows(rwkv_a2[j]),
                g1=_pad_cols(rwkv_g1[j]), g2=_pad_rows(rwkv_g2[j]),
                k_k=row(rwkv_k_k[j]), k_a=row(rwkv_k_a[j]), r_k=row(rwkv_r_k[j]),
                lnx_w=row(rwkv_lnx_w[j]), lnx_b=row(rwkv_lnx_b[j]),
            )
            if j > 0:
                p.update(v0=row(rwkv_v0[j - 1]), v1=_pad_cols(rwkv_v1[j - 1]), v2=_pad_rows(rwkv_v2[j - 1]))
            shift_rows = jnp.zeros((DEC_BATCH, DEC_SEQ, D_MODEL), F32).at[:, 0].set(state_shift[:, j])
            outs_p = _prep(x, x, vfirst_p if j > 0 else None, p, prompt=True)
            outs_s = _prep(x, shift_rows.reshape(SAMPLE_TOTAL, D_MODEL), vfirst_s if j > 0 else None, p,
                           prompt=False)
            if j == 0:
                vfirst_p, vfirst_s = outs_p[2], outs_s[2]
            zp, st_p = _chunk_prompt(outs_p[:6], p)
            zs, st_s = _chunk_sample(outs_s[:6], p, state_wkv, j)
            xn_tail = outs_p[6].reshape(BATCH, _PREP_TILES_PER_PROMPT, 8, D_MODEL)
            p_wkv.append(st_p)
            p_shift.append(xn_tail[:, -1, -1])
            s_wkv.append(st_s)
            s_shift.append(outs_s[6].reshape(DEC_BATCH, DEC_SEQ, D_MODEL)[:, -1])
            w_o = rwkv_w_o[j].astype(BF16)
        else:
            q, kv = _qkv(x, row(gains[0]), attn_w_qkv[j].astype(BF16))
            sinks = attn_sinks[j].astype(F32)
            zp = _attn_prompt(sinks, q, kv)
            zs, nwk, nwv = _attn_sample(sinks, q, kv, cmk, cmv, cwk, cwv, j)
            kvp = kv[:PROMPT_TOTAL].reshape(BATCH, PROMPT_ROWS, 2, N_KV_HEADS, HEAD)
            p_mk.append(kvp[:, PAD_ROWS:WINDOW, 0])
            p_mv.append(kvp[:, PAD_ROWS:WINDOW, 1])
            p_wk.append(kvp[:, -WINDOW:, 0])
            p_wv.append(kvp[:, -WINDOW:, 1])
            s_wk.append(nwk.reshape(DEC_BATCH, WINDOW, N_KV_HEADS, HEAD))
            s_wv.append(nwv.reshape(DEC_BATCH, WINDOW, N_KV_HEADS, HEAD))
            w_o = attn_w_o[j].astype(BF16)
        x = _oproj(zp, zs, x, w_o, row(gains[1]))
        x = _mlp(x, row(gains[2]), row(gains[3]), mlp_w_up[i].astype(BF16), mlp_w_down[i].astype(BF16))

    y_prompt = x[:PROMPT_TOTAL].reshape(BATCH, PROMPT_ROWS, D_MODEL)[:, WINDOW:]
    y_sample = x[PROMPT_TOTAL:].reshape(DEC_BATCH, DEC_SEQ, D_MODEL)
    st = lambda xs: jnp.stack(xs, axis=1)
    return (y_prompt, y_sample, st(p_wkv), st(p_shift), st(p_wk), st(p_wv), st(p_mk), st(p_mv),
            st(s_wkv), st(s_shift), st(s_wk), st(s_wv))
```

```python
import functools

import jax
import jax.numpy as jnp
from jax import lax
from jax.experimental import pallas as pl
from jax.experimental.pallas import tpu as pltpu

F32 = jnp.float32
BF16 = jnp.bfloat16

D_MODEL = 1024
BATCH = 2
SEQ = 8192
DEPTH = 4
DEC_BATCH = 128
DEC_SEQ = 8
PAST_LEN = 8192
N_META = 16
HEADS = 16
HEAD = 64
N_KV_HEADS = 4
GROUP = HEADS // N_KV_HEADS
KV_WIDTH = N_KV_HEADS * HEAD
WINDOW = 128
D_FF = 4 * D_MODEL
RMS_EPS = 1e-6
GN_EPS = 6.4e-4
NEG_INF = -1e30
LORA_PAD = 128

PAD_ROWS = WINDOW - N_META
PROMPT_ROWS = PAD_ROWS + N_META + SEQ
PROMPT_TOTAL = BATCH * PROMPT_ROWS
SAMPLE_TOTAL = DEC_BATCH * DEC_SEQ
TOTAL_ROWS = PROMPT_TOTAL + SAMPLE_TOTAL

CHUNK = 64
SUB = 16
SEQ_GROUP = CHUNK // DEC_SEQ
TM_ROWS = 768
TM_DUAL = 256
TM_PREP = 320
TF = 512
VMEM_LIMIT = 56 * 1024 * 1024
SMALL_MM_PASSES = 1


def _cparams(sem):
    return pltpu.CompilerParams(dimension_semantics=sem, vmem_limit_bytes=VMEM_LIMIT)


def _rms(x, g):
    return x * lax.rsqrt(jnp.mean(x * x, axis=-1, keepdims=True) + RMS_EPS) * g


def _sigmoid(x):
    return 1.0 / (1.0 + jnp.exp(-x))


def _bdot(a, b):
    return jnp.dot(a.astype(BF16), b.astype(BF16), preferred_element_type=F32)


def _split(x):
    hi = x.astype(BF16)
    lo = (x - hi.astype(F32)).astype(BF16)
    return hi, lo


_NN = (((1,), (0,)), ((), ()))
_NT = (((1,), (1,)), ((), ()))
_TN = (((0,), (0,)), ((), ()))


def _mm(a, b, dims=_NN, passes=SMALL_MM_PASSES):
    dg = functools.partial(lax.dot_general, dimension_numbers=dims, preferred_element_type=F32)
    if passes == 1:
        return dg(a.astype(BF16), b.astype(BF16))
    ah, al = _split(a)
    bh, bl = _split(b)
    return dg(ah, bh) + (dg(ah, bl) + dg(al, bh))


def _div_pow2(x, n):
    shift = n.bit_length() - 1
    assert 1 << shift == n
    return jnp.right_shift(x, shift)


def _dot_exact_lhs(m01, x):
    hi, lo = _split(x)
    return (jnp.dot(m01, hi, preferred_element_type=F32)
            + jnp.dot(m01, lo, preferred_element_type=F32))


def _mlp_body(x_ref, g2_ref, g3_ref, wup_ref, wdn_ref, o_ref, xn_ref, acc_ref):
    f = pl.program_id(1)

    @pl.when(f == 0)
    def _():
        xn_ref[...] = _rms(x_ref[...], g2_ref[...]).astype(BF16)
        acc_ref[...] = jnp.zeros_like(acc_ref)

    h = jnp.dot(xn_ref[...], wup_ref[...], preferred_element_type=F32)
    a = jnp.maximum(h, 0.0)
    acc_ref[...] += jnp.dot((a * a).astype(BF16), wdn_ref[...], preferred_element_type=F32)

    @pl.when(f == pl.num_programs(1) - 1)
    def _():
        o_ref[...] = x_ref[...] + _rms(acc_ref[...], g3_ref[...])


def _mlp(x, g2, g3, wup, wdn):
    nt, nf = TOTAL_ROWS // TM_ROWS, D_FF // TF
    return pl.pallas_call(
        _mlp_body,
        grid=(nt, nf),
        in_specs=[
            pl.BlockSpec((TM_ROWS, D_MODEL), lambda i, f: (i, 0)),
            pl.BlockSpec((1, D_MODEL), lambda i, f: (0, 0)),
            pl.BlockSpec((1, D_MODEL), lambda i, f: (0, 0)),
            pl.BlockSpec((D_MODEL, TF), lambda i, f: (0, f)),
            pl.BlockSpec((TF, D_MODEL), lambda i, f: (f, 0)),
        ],
        out_specs=pl.BlockSpec((TM_ROWS, D_MODEL), lambda i, f: (i, 0)),
        out_shape=jax.ShapeDtypeStruct((TOTAL_ROWS, D_MODEL), F32),
        scratch_shapes=[pltpu.VMEM((TM_ROWS, D_MODEL), BF16), pltpu.VMEM((TM_ROWS, D_MODEL), F32)],
        compiler_params=_cparams(("parallel", "arbitrary")),
        name="mlp",
    )(x, g2, g3, wup, wdn)


_DUAL_PROMPT_TILES = PROMPT_TOTAL // TM_DUAL
_DUAL_TILES = TOTAL_ROWS // TM_DUAL


def _oproj_body(zp_ref, zs_ref, x_ref, w_ref, g_ref, o_ref):
    i = pl.program_id(0)

    def finish(z):
        m = jnp.dot(z, w_ref[...], preferred_element_type=F32)
        o_ref[...] = x_ref[...] + _rms(m, g_ref[...])

    @pl.when(i < _DUAL_PROMPT_TILES)
    def _():
        finish(zp_ref[...])

    @pl.when(i >= _DUAL_PROMPT_TILES)
    def _():
        finish(zs_ref[...])


def _oproj(zp, zs, x, w, g):
    return pl.pallas_call(
        _oproj_body,
        grid=(_DUAL_TILES,),
        in_specs=[
            pl.BlockSpec((TM_DUAL, D_MODEL), lambda i: (jnp.minimum(i, _DUAL_PROMPT_TILES - 1), 0)),
            pl.BlockSpec((TM_DUAL, D_MODEL), lambda i: (jnp.maximum(i - _DUAL_PROMPT_TILES, 0), 0)),
            pl.BlockSpec((TM_DUAL, D_MODEL), lambda i: (i, 0)),
            pl.BlockSpec((D_MODEL, D_MODEL), lambda i: (0, 0)),
            pl.BlockSpec((1, D_MODEL), lambda i: (0, 0)),
        ],
        out_specs=pl.BlockSpec((TM_DUAL, D_MODEL), lambda i: (i, 0)),
        out_shape=jax.ShapeDtypeStruct((TOTAL_ROWS, D_MODEL), F32),
        compiler_params=_cparams(("parallel",)),
        name="oproj",
    )(zp, zs, x, w, g)


def _qkv_body(x_ref, g_ref, w_ref, q_ref, kv_ref):
    xn = _rms(x_ref[...], g_ref[...]).astype(BF16)
    res = jnp.dot(xn, w_ref[...], preferred_element_type=F32)
    q_ref[...] = (res[:, :D_MODEL] * (HEAD ** -0.5)).astype(BF16)
    kv_ref[...] = res[:, D_MODEL:]


def _qkv(x, g, w):
    nt = TOTAL_ROWS // TM_ROWS
    return pl.pallas_call(
        _qkv_body,
        grid=(nt,),
        in_specs=[
            pl.BlockSpec((TM_ROWS, D_MODEL), lambda i: (i, 0)),
            pl.BlockSpec((1, D_MODEL), lambda i: (0, 0)),
            pl.BlockSpec((D_MODEL, D_MODEL + 2 * KV_WIDTH), lambda i: (0, 0)),
        ],
        out_specs=[
            pl.BlockSpec((TM_ROWS, D_MODEL), lambda i: (i, 0)),
            pl.BlockSpec((TM_ROWS, 2 * KV_WIDTH), lambda i: (i, 0)),
        ],
        out_shape=[
            jax.ShapeDtypeStruct((TOTAL_ROWS, D_MODEL), BF16),
            jax.ShapeDtypeStruct((TOTAL_ROWS, 2 * KV_WIDTH), F32),
        ],
        compiler_params=_cparams(("parallel",)),
        name="qkv",
    )(x, g, w)


def _alibi_slope(h):
    return 2.0 ** (-8.0 * (h + 1) / HEADS)


def _group_queries(q, kvh):
    return jnp.concatenate([q[:, h * HEAD:(h + 1) * HEAD] for h in range(kvh * GROUP, (kvh + 1) * GROUP)], axis=0)


def _group_columns(sink_ref, kvh, tq):
    g = _div_pow2(lax.broadcasted_iota(jnp.int32, (GROUP * tq, 1), 0), tq)
    slope = jnp.zeros((GROUP * tq, 1), F32)
    sink = jnp.zeros((GROUP * tq, 1), F32)
    for i in range(GROUP):
        h = kvh * GROUP + i
        slope = jnp.where(g == i, _alibi_slope(h), slope)
        sink = jnp.where(g == i, sink_ref[h], sink)
    return slope, sink


def _attend_groups(qs, ks, vs, cols, ok, mind):
    scores = [lax.dot_general(q, k, _NT, preferred_element_type=F32) for q, k in zip(qs, ks)]
    probs = []
    for s, (slope, sink) in zip(scores, cols):
        logits = jnp.where(ok, s - slope * mind, NEG_INF)
        m = jnp.maximum(jnp.max(logits, axis=-1, keepdims=True), sink)
        p = jnp.exp(logits - m)
        den = jnp.sum(p, axis=-1, keepdims=True) + jnp.exp(sink - m)
        probs.append((p / den).astype(BF16))
    return [jnp.dot(p, v, preferred_element_type=F32) for p, v in zip(probs, vs)]


def _store_group(o_ref, rows, kvh, o, tq):
    for i in range(GROUP):
        h = kvh * GROUP + i
        o_ref[rows, h * HEAD:(h + 1) * HEAD] = o[i * tq:(i + 1) * tq].astype(o_ref.dtype)


_BLOCKS_PER_PROMPT = PROMPT_ROWS // WINDOW


def _attn_prompt_body(sink_ref, q_ref, kv0_ref, kvp_ref, kvc_ref, o_ref):
    i = pl.program_id(1)
    kv = jnp.concatenate([kv0_ref[...], kvp_ref[...], kvc_ref[...]], axis=0).astype(BF16)
    shp = (GROUP * WINDOW, 3 * WINDOW)
    r = jnp.bitwise_and(lax.broadcasted_iota(jnp.int32, shp, 0), WINDOW - 1)
    c = lax.broadcasted_iota(jnp.int32, shp, 1)
    kblk = _div_pow2(c, WINDOW)
    cc = c - kblk * WINDOW
    qpos = jnp.where(i == 0, r - PAD_ROWS, N_META + (i - 1) * WINDOW + r)
    kpos = jnp.where(kblk == 0, cc - PAD_ROWS, N_META + (i - 3 + kblk) * WINDOW + cc)
    kmeta = kblk == 0
    kvalid = jnp.where(kmeta, cc - PAD_ROWS, jnp.where(kblk == 1, i - 2, i - 1)) >= 0
    dist = qpos - kpos
    ok = kvalid & (dist >= 0) & (kmeta | (dist < WINDOW))
    mind = jnp.minimum(dist, WINDOW).astype(F32)
    q = q_ref[...]
    kvhs = range(N_KV_HEADS)
    outs = _attend_groups(
        [_group_queries(q, j) for j in kvhs],
        [kv[:, j * HEAD:(j + 1) * HEAD] for j in kvhs],
        [kv[:, KV_WIDTH + j * HEAD:KV_WIDTH + (j + 1) * HEAD] for j in kvhs],
        [_group_columns(sink_ref, j, WINDOW) for j in kvhs], ok, mind)
    for j in kvhs:
        _store_group(o_ref, slice(None), j, outs[j], WINDOW)


def _attn_prompt(sinks, q, kv):
    nb = _BLOCKS_PER_PROMPT
    return pl.pallas_call(
        _attn_prompt_body,
        grid=(BATCH, nb),
        in_specs=[
            pl.BlockSpec(memory_space=pltpu.SMEM),
            pl.BlockSpec((WINDOW, D_MODEL), lambda b, i: (b * nb + i, 0)),
            pl.BlockSpec((WINDOW, 2 * KV_WIDTH), lambda b, i: (b * nb, 0)),
            pl.BlockSpec((WINDOW, 2 * KV_WIDTH), lambda b, i: (b * nb + jnp.maximum(i - 1, 0), 0)),
            pl.BlockSpec((WINDOW, 2 * KV_WIDTH), lambda b, i: (b * nb + i, 0)),
        ],
        out_specs=pl.BlockSpec((WINDOW, D_MODEL), lambda b, i: (b * nb + i, 0)),
        out_shape=jax.ShapeDtypeStruct((PROMPT_TOTAL, D_MODEL), BF16),
        compiler_params=_cparams(("parallel", "parallel")),
        name="attn_prompt",
    )(sinks, q, kv, kv, kv)


_ATT_SEQS = 8
_SAMPLE_KEYS = N_META + WINDOW + DEC_SEQ


def _attn_sample_body(sink_ref, q_ref, kv_ref, mk_ref, mv_ref, wk_ref, wv_ref, o_ref, nwk_ref, nwv_ref):
    shp = (GROUP * DEC_SEQ, _SAMPLE_KEYS)
    t = jnp.bitwise_and(lax.broadcasted_iota(jnp.int32, shp, 0), DEC_SEQ - 1)
    c = lax.broadcasted_iota(jnp.int32, shp, 1)
    qpos = PAST_LEN + t
    kpos = jnp.where(c < N_META, c,
                     jnp.where(c < N_META + WINDOW, PAST_LEN - WINDOW + (c - N_META),
                               PAST_LEN + (c - N_META - WINDOW)))
    kmeta = c < N_META
    kvalid = kmeta | (kpos >= N_META)
    dist = qpos - kpos
    ok = kvalid & (dist >= 0) & (kmeta | (dist < WINDOW))
    mind = jnp.minimum(dist, WINDOW).astype(F32)
    kvhs = range(N_KV_HEADS)
    cols = [_group_columns(sink_ref, j, DEC_SEQ) for j in kvhs]
    qs, ks, vs, cs = [], [], [], []
    for s in range(_ATT_SEQS):
        rows = slice(s * DEC_SEQ, (s + 1) * DEC_SEQ)
        knew = kv_ref[rows, :KV_WIDTH]
        vnew = kv_ref[rows, KV_WIDTH:]
        wk = wk_ref[s]
        wv = wv_ref[s]
        nwk_ref[s] = jnp.concatenate([wk[DEC_SEQ:], knew], axis=0)
        nwv_ref[s] = jnp.concatenate([wv[DEC_SEQ:], vnew], axis=0)
        kall = jnp.concatenate([mk_ref[s], wk, knew], axis=0).astype(BF16)
        vall = jnp.concatenate([mv_ref[s], wv, vnew], axis=0).astype(BF16)
        q = q_ref[rows, :]
        for j in kvhs:
            qs.append(_group_queries(q, j))
            ks.append(kall[:, j * HEAD:(j + 1) * HEAD])
            vs.append(vall[:, j * HEAD:(j + 1) * HEAD])
            cs.append(cols[j])
    outs = _attend_groups(qs, ks, vs, cs, ok, mind)
    for s in range(_ATT_SEQS):
        for j in kvhs:
            _store_group(o_ref, slice(s * DEC_SEQ, (s + 1) * DEC_SEQ), j, outs[s * N_KV_HEADS + j], DEC_SEQ)


def _attn_sample(sinks, q, kv, cmk, cmv, cwk, cwv, layer):
    rows = _ATT_SEQS * DEC_SEQ
    off = PROMPT_TOTAL // rows
    ng = DEC_BATCH // _ATT_SEQS
    cache_spec = lambda n: pl.BlockSpec((_ATT_SEQS, None, n, KV_WIDTH), lambda g: (g, layer, 0, 0))
    return pl.pallas_call(
        _attn_sample_body,
        grid=(ng,),
        in_specs=[
            pl.BlockSpec(memory_space=pltpu.SMEM),
            pl.BlockSpec((rows, D_MODEL), lambda g: (off + g, 0)),
            pl.BlockSpec((rows, 2 * KV_WIDTH), lambda g: (off + g, 0)),
            cache_spec(N_META), cache_spec(N_META), cache_spec(WINDOW), cache_spec(WINDOW),
        ],
        out_specs=[
            pl.BlockSpec((rows, D_MODEL), lambda g: (g, 0)),
            pl.BlockSpec((_ATT_SEQS, WINDOW, KV_WIDTH), lambda g: (g, 0, 0)),
            pl.BlockSpec((_ATT_SEQS, WINDOW, KV_WIDTH), lambda g: (g, 0, 0)),
        ],
        out_shape=[
            jax.ShapeDtypeStruct((SAMPLE_TOTAL, D_MODEL), BF16),
            jax.ShapeDtypeStruct((DEC_BATCH, WINDOW, KV_WIDTH), F32),
            jax.ShapeDtypeStruct((DEC_BATCH, WINDOW, KV_WIDTH), F32),
        ],
        compiler_params=_cparams(("parallel",)),
        name="attn_sample",
    )(sinks, q, kv, cmk, cmv, cwk, cwv)


_PREP_TILES_PER_PROMPT = PROMPT_ROWS // TM_PREP
_DECAY_SCALE = 0.6065306597126334


def _prep_body(*refs, prompt, vres):
    it = iter(refs)
    x_ref = next(it)
    side_ref = next(it)
    vf_ref = next(it) if vres else None
    g0_ref, mu_ref = next(it), next(it)
    wr_ref, wk_ref, wv_ref = next(it), next(it), next(it)
    w0_ref, w1_ref, w2_ref = next(it), next(it), next(it)
    a0_ref, a1_ref, a2_ref = next(it), next(it), next(it)
    if vres:
        v0_ref, v1_ref, v2_ref = next(it), next(it), next(it)
    g1_ref, g2_ref = next(it), next(it)
    r_ref, k_ref, v_ref, a_ref, lw_ref, g_ref, xn_ref = (next(it) for _ in range(7))

    tm = x_ref.shape[0]
    g0 = g0_ref[...]
    xn = _rms(x_ref[...], g0)
    row = lax.broadcasted_iota(jnp.int32, (tm, 1), 0)
    if prompt:
        ib = lax.rem(pl.program_id(0), _PREP_TILES_PER_PROMPT)
        xn = jnp.where(ib * tm + row < PAD_ROWS, 0.0, xn)
        before = _rms(side_ref[...], g0)[7:8, :]
        before = jnp.where(ib == 0, 0.0, before)
        x_prev = jnp.where(row == 0, before, pltpu.roll(xn, 1, 0))
        xn_ref[...] = xn[tm - 8:, :]
    else:
        x_prev = jnp.where(jnp.bitwise_and(row, DEC_SEQ - 1) == 0, side_ref[...], pltpu.roll(xn, 1, 0))
        xn_ref[...] = xn
    xx = x_prev - xn
    xr, xw, xk, xv, xa, xg = (xn + xx * mu_ref[c:c + 1, :] for c in range(6))

    r_ref[...] = _bdot(xr, wr_ref[...])
    k_ref[...] = _bdot(xk, wk_ref[...])
    v = _bdot(xv, wv_ref[...])
    wl = w0_ref[...] + _bdot(jnp.tanh(_bdot(xw, w1_ref[...])), w2_ref[...])
    lw_ref[...] = -_DECAY_SCALE * _sigmoid(wl)
    if vres:
        mix = _sigmoid(v0_ref[...] + _bdot(_bdot(xv, v1_ref[...]), v2_ref[...]))
        v = v + (vf_ref[...] - v) * mix
    v_ref[...] = v
    a_ref[...] = _sigmoid(a0_ref[...] + _bdot(_bdot(xa, a1_ref[...]), a2_ref[...]))
    g_ref[...] = _bdot(_sigmoid(_bdot(xg, g1_ref[...])), g2_ref[...])


def _prep(x, side, vfirst, p, prompt):
    vres = vfirst is not None
    if prompt:
        tm, nt, off, nrows = TM_PREP, PROMPT_TOTAL // TM_PREP, 0, PROMPT_TOTAL
        side_spec = pl.BlockSpec((8, D_MODEL), lambda i: (jnp.maximum(i * (TM_PREP // 8) - 1, 0), 0))
        xn_spec = pl.BlockSpec((8, D_MODEL), lambda i: (i, 0))
        xn_shape = jax.ShapeDtypeStruct((nt * 8, D_MODEL), F32)
    else:
        tm, nt, off, nrows = TM_DUAL, SAMPLE_TOTAL // TM_DUAL, PROMPT_TOTAL // TM_DUAL, SAMPLE_TOTAL
        side_spec = pl.BlockSpec((tm, D_MODEL), lambda i: (i, 0))
        xn_spec = pl.BlockSpec((tm, D_MODEL), lambda i: (i, 0))
        xn_shape = jax.ShapeDtypeStruct((nrows, D_MODEL), F32)
    row_spec = pl.BlockSpec((tm, D_MODEL), lambda i: (i, 0))
    const = lambda a: pl.BlockSpec(a.shape, lambda i: (0,) * a.ndim)

    args = [x, side]
    specs = [pl.BlockSpec((tm, D_MODEL), lambda i: (off + i, 0)), side_spec]
    if vres:
        args.append(vfirst)
        specs.append(row_spec)
    names = ["g0", "mu", "w_r", "w_k", "w_v", "w0", "w1", "w2", "a0", "a1", "a2"]
    if vres:
        names += ["v0", "v1", "v2"]
    names += ["g1", "g2"]
    for n in names:
        args.append(p[n])
        specs.append(const(p[n]))
    big = jax.ShapeDtypeStruct((nrows, D_MODEL), F32)
    return pl.pallas_call(
        functools.partial(_prep_body, prompt=prompt, vres=vres),
        grid=(nt,),
        in_specs=specs,
        out_specs=[row_spec] * 6 + [xn_spec],
        out_shape=[big] * 6 + [xn_shape],
        compiler_params=_cparams(("parallel",)),
        name="rwkv_prep_prompt" if prompt else "rwkv_prep_sample",
    )(*args)


def _eye(n):
    r = lax.broadcasted_iota(jnp.int32, (n, n), 0)
    c = lax.broadcasted_iota(jnp.int32, (n, n), 1)
    return (r == c).astype(F32)


def _mms(a_list, b_list, dims=_NN):
    return [_mm(a, b, dims) for a, b in zip(a_list, b_list)]


def _unit_lower_inverse(n_list, sub, nblocks):
    n = n_list[0].shape[0]
    eye = _eye(n)
    r = lax.broadcasted_iota(jnp.int32, (n, n), 0)
    c = lax.broadcasted_iota(jnp.int32, (n, n), 1)
    diag_blk = _div_pow2(r, sub) == _div_pow2(c, sub)
    ms = [jnp.where(diag_blk, -x, 0.0) for x in n_list]
    tds = [eye + m for m in ms]
    span = 1
    while 2 * span < sub:
        ms = _mms(ms, ms)
        tds = [td + p for td, p in zip(tds, _mms(tds, ms))]
        span *= 2
    if nblocks == 1:
        return tds
    xs = _mms(tds, [jnp.where(diag_blk, 0.0, x) for x in n_list])
    zs = [eye - x for x in xs]
    if nblocks > 2:
        assert nblocks <= 4
        zs = [z + p for z, p in zip(zs, _mms(zs, _mms(xs, xs)))]
    return _mms(zs, tds)


def _stacked_causal_mask(c, seq_len):
    row = lax.broadcasted_iota(jnp.int32, (2 * c, 2 * c), 0)
    t = jnp.bitwise_and(row, c - 1)
    j = jnp.bitwise_and(lax.broadcasted_iota(jnp.int32, (2 * c, 2 * c), 1), c - 1)
    same = _div_pow2(t, seq_len) == _div_pow2(j, seq_len)
    return same & (t - j >= jnp.where(row < c, 1, 0))


def _chunk_prologue(refs, seq_len, lead=None):
    r_ref, k_ref, v_ref, a_ref, lw_ref, g_ref = (x if lead is None else x.at[lead] for x in refs[:6])
    kk_ref, ka_ref = refs[6:]
    c = CHUNK
    row = lax.broadcasted_iota(jnp.int32, (c, c), 0)
    col = lax.broadcasted_iota(jnp.int32, (c, c), 1)
    same = _div_pow2(row, seq_len) == _div_pow2(col, seq_len)
    strict = same & (row > col)
    incl = same & (row >= col)
    lw = lw_ref[...]
    one01 = lambda m: jnp.where(m, 1.0, 0.0).astype(BF16)
    cum = _dot_exact_lhs(one01(incl), lw)
    cum_end = _dot_exact_lhs(one01(same), lw)
    k_raw = k_ref[...]
    a = a_ref[...]
    q = dict(
        mask2=_stacked_causal_mask(c, seq_len),
        r=r_ref[...], v=v_ref[...], a=a, g=g_ref[...],
        k=k_raw * (1.0 + (a - 1.0) * ka_ref[...]),
        kk_un=k_raw * kk_ref[...],
        e_cum=jnp.exp(cum), e_neg=jnp.exp(-cum), e_excl=jnp.exp(cum - lw),
        e_end=jnp.exp(cum_end - cum), p_end=jnp.exp(cum_end),
    )
    return q


def _head_terms(qs, sub, nblocks):
    hs = range(len(qs) * HEADS)
    col = lambda name: [q[name][:, h * HEAD:(h + 1) * HEAD] for q in qs for h in range(HEADS)]
    rh, kh, vh, ah, kk = col("r"), col("k"), col("v"), col("a"), col("kk_un")
    e_excl, e_cum, e_neg, e_end = col("e_excl"), col("e_cum"), col("e_neg"), col("e_end")
    kk = [x / jnp.maximum(jnp.sqrt(jnp.sum(x * x, axis=-1, keepdims=True)), 1e-12) for x in kk]
    bh = [kk[h] * ah[h] for h in hs]
    at = [kk[h] * e_excl[h] for h in hs]
    rt = [rh[h] * e_cum[h] for h in hs]
    bt = [bh[h] * e_neg[h] for h in hs]
    kt = [kh[h] * e_neg[h] for h in hs]
    bk_end = [jnp.concatenate([bh[h] * e_end[h], kh[h] * e_end[h]], axis=0) for h in hs]
    a_all = _mms([jnp.concatenate([at[h], rt[h]], axis=0) for h in hs],
                 [jnp.concatenate([bt[h], kt[h]], axis=0) for h in hs], _NT)
    a_all = [jnp.where(qs[0]["mask2"], x, 0.0) for x in a_all]
    t = _unit_lower_inverse([x[:CHUNK, :CHUNK] for x in a_all], sub, nblocks)
    w = _mms(t, at)
    ubar = [-x for x in _mms(t, _mms([x[:CHUNK, CHUNK:] for x in a_all], vh))]
    return dict(rh=rh, kh=kh, vh=vh, w_rt=[jnp.concatenate([w[h], rt[h]], axis=0) for h in hs], ubar=ubar,
                a_r=[x[CHUNK:, :] for x in a_all], bk_end=bk_end, p_end=col("p_end"), g=col("g"))


def _head_outputs(t, ys, rk_ref, lnw_ref, lnb_ref, z_refs):
    ns = range(len(ys))
    sls = [slice((n % HEADS) * HEAD, (n % HEADS + 1) * HEAD) for n in ns]
    bonus = [jnp.sum(t["rh"][n] * t["kh"][n] * rk_ref[:, sls[n]], axis=-1, keepdims=True) for n in ns]
    mean = [jnp.mean(y, axis=-1, keepdims=True) for y in ys]
    d = [ys[n] - mean[n] for n in ns]
    var = [jnp.mean(x * x, axis=-1, keepdims=True) for x in d]
    for n in ns:
        yn = d[n] * lax.rsqrt(var[n] + GN_EPS) * lnw_ref[:, sls[n]] + lnb_ref[:, sls[n]]
        z_refs[n][:, sls[n]] = ((yn + bonus[n] * t["vh"][n]) * t["g"][n]).astype(BF16)


def _chunk_prompt_body(r_ref, k_ref, v_ref, a_ref, lw_ref, g_ref, kk_ref, ka_ref, rk_ref, lnw_ref, lnb_ref,
                       z_ref, sout_ref, s_ref):
    ci = pl.program_id(1)

    @pl.when(ci == 0)
    def _():
        s_ref[...] = jnp.zeros_like(s_ref)

    refs = (r_ref, k_ref, v_ref, a_ref, lw_ref, g_ref, kk_ref, ka_ref)
    nb = r_ref.shape[0]
    t = _head_terms([_chunk_prologue(refs, CHUNK, b) for b in range(nb)], SUB, CHUNK // SUB)
    ns = range(nb * HEADS)
    s0 = [s_ref[n // HEADS, n % HEADS] for n in ns]
    x = _mms(t["w_rt"], s0, _NT)
    uv = [jnp.concatenate([t["ubar"][n] - x[n][:CHUNK], t["vh"][n]], axis=0) for n in ns]
    y = _mms(t["a_r"], uv)
    ds = _mms(uv, t["bk_end"], _TN)
    for n in ns:
        s_ref[n // HEADS, n % HEADS] = s0[n] * t["p_end"][n][0:1, :] + ds[n]
    _head_outputs(t, [x[n][CHUNK:] + y[n] for n in ns], rk_ref, lnw_ref, lnb_ref,
                  [z_ref.at[n // HEADS] for n in ns])

    @pl.when(ci == pl.num_programs(1) - 1)
    def _():
        sout_ref[...] = s_ref[...]


_CHUNK_SEQS = 1


def _chunk_prompt(proj, p):
    nc = PROMPT_ROWS // CHUNK
    row_spec = pl.BlockSpec((_CHUNK_SEQS, CHUNK, D_MODEL), lambda b, c: (b, c, 0))
    vec_spec = pl.BlockSpec((1, D_MODEL), lambda b, c: (0, 0))
    z, st = pl.pallas_call(
        _chunk_prompt_body,
        grid=(BATCH // _CHUNK_SEQS, nc),
        in_specs=[row_spec] * 6 + [vec_spec] * 5,
        out_specs=[
            row_spec,
            pl.BlockSpec((_CHUNK_SEQS, HEADS, HEAD, HEAD), lambda b, c: (b, 0, 0, 0)),
        ],
        out_shape=[
            jax.ShapeDtypeStruct((BATCH, PROMPT_ROWS, D_MODEL), BF16),
            jax.ShapeDtypeStruct((BATCH, HEADS, HEAD, HEAD), F32),
        ],
        scratch_shapes=[pltpu.VMEM((_CHUNK_SEQS, HEADS, HEAD, HEAD), F32)],
        compiler_params=_cparams(("parallel", "arbitrary")),
        name="rwkv_chunk_prompt",
    )(*[a.reshape(BATCH, PROMPT_ROWS, D_MODEL) for a in proj],
      p["k_k"], p["k_a"], p["r_k"], p["lnx_w"], p["lnx_b"])
    return z.reshape(PROMPT_TOTAL, D_MODEL), st


def _chunk_sample_body(r_ref, k_ref, v_ref, a_ref, lw_ref, g_ref, kk_ref, ka_ref, rk_ref, lnw_ref, lnb_ref,
                       sin_ref, z_ref, sout_ref):
    q = _chunk_prologue((r_ref, k_ref, v_ref, a_ref, lw_ref, g_ref, kk_ref, ka_ref), DEC_SEQ)
    seq_of_row2 = _div_pow2(jnp.bitwise_and(lax.broadcasted_iota(jnp.int32, (2 * CHUNK, 1), 0), CHUNK - 1),
                            DEC_SEQ)
    hs = range(HEADS)
    seqs = range(SEQ_GROUP)
    rows = [slice(s * DEC_SEQ, (s + 1) * DEC_SEQ) for s in seqs]
    t = _head_terms([q], DEC_SEQ, 1)
    rows2 = [slice(CHUNK + s * DEC_SEQ, CHUNK + (s + 1) * DEC_SEQ) for s in seqs]
    ws = [jnp.concatenate([_mm(t["w_rt"][h][rows[s]], sin_ref[s, h], _NT) for s in seqs], axis=0) for h in hs]
    rs = [jnp.concatenate([_mm(t["w_rt"][h][rows2[s]], sin_ref[s, h], _NT) for s in seqs], axis=0) for h in hs]
    uv = [jnp.concatenate([t["ubar"][h] - ws[h], t["vh"][h]], axis=0) for h in hs]
    y = _mms(t["a_r"], uv)
    for h in hs:
        for s in seqs:
            mine = jnp.where(seq_of_row2 == s, uv[h], 0.0)
            p_end = t["p_end"][h][s * DEC_SEQ:s * DEC_SEQ + 1, :]
            sout_ref[s, h] = sin_ref[s, h] * p_end + _mm(mine, t["bk_end"][h], _TN)
    _head_outputs(t, [rs[h] + y[h] for h in hs], rk_ref, lnw_ref, lnb_ref, [z_ref] * HEADS)


def _chunk_sample(proj, p, state_wkv, layer):
    ng = DEC_BATCH // SEQ_GROUP
    row_spec = pl.BlockSpec((CHUNK, D_MODEL), lambda g: (g, 0))
    vec_spec = pl.BlockSpec((1, D_MODEL), lambda g: (0, 0))
    return pl.pallas_call(
        _chunk_sample_body,
        grid=(ng,),
        in_specs=[row_spec] * 6 + [vec_spec] * 5 + [
            pl.BlockSpec((SEQ_GROUP, None, HEADS, HEAD, HEAD), lambda g: (g, layer, 0, 0, 0)),
        ],
        out_specs=[
            row_spec,
            pl.BlockSpec((SEQ_GROUP, HEADS, HEAD, HEAD), lambda g: (g, 0, 0, 0)),
        ],
        out_shape=[
            jax.ShapeDtypeStruct((SAMPLE_TOTAL, D_MODEL), BF16),
            jax.ShapeDtypeStruct((DEC_BATCH, HEADS, HEAD, HEAD), F32),
        ],
        compiler_params=_cparams(("parallel",)),
        name="rwkv_chunk_sample",
    )(*proj, p["k_k"], p["k_a"], p["r_k"], p["lnx_w"], p["lnx_b"], state_wkv)


def _pad_cols(w):
    return jnp.pad(w, ((0, 0), (0, LORA_PAD - w.shape[1]))).astype(BF16)


def _pad_rows(w):
    return jnp.pad(w, ((0, LORA_PAD - w.shape[0]), (0, 0))).astype(BF16)


def kernel(x_prompt, x_sample, state_wkv, state_shift, cache_win_k, cache_win_v, cache_meta_k,
           cache_meta_v, meta_tokens, norm_gains, rwkv_mu, rwkv_w_r, rwkv_w_k, rwkv_w_v, rwkv_w_o,
           rwkv_w0, rwkv_w1, rwkv_w2, rwkv_a0, rwkv_a1, rwkv_a2, rwkv_v0, rwkv_v1, rwkv_v2,
           rwkv_g1, rwkv_g2, rwkv_k_k, rwkv_k_a, rwkv_r_k, rwkv_lnx_w, rwkv_lnx_b,
           attn_w_qkv, attn_w_o, attn_sinks, mlp_w_up, mlp_w_down):
    assert x_prompt.shape == (BATCH, SEQ, D_MODEL) and x_sample.shape == (DEC_BATCH, DEC_SEQ, D_MODEL)
    n_swa = cache_win_k.shape[1]
    row = lambda v: v.reshape(1, D_MODEL).astype(F32)

    head = jnp.concatenate([jnp.zeros((PAD_ROWS, D_MODEL), F32), meta_tokens.astype(F32)], axis=0)
    x = jnp.concatenate(
        [jnp.concatenate([jnp.broadcast_to(head[None], (BATCH, WINDOW, D_MODEL)), x_prompt], axis=1)
         .reshape(PROMPT_TOTAL, D_MODEL), x_sample.reshape(SAMPLE_TOTAL, D_MODEL)], axis=0)

    cmk = cache_meta_k.reshape(DEC_BATCH, n_swa, N_META, KV_WIDTH)
    cmv = cache_meta_v.reshape(DEC_BATCH, n_swa, N_META, KV_WIDTH)
    cwk = cache_win_k.reshape(DEC_BATCH, n_swa, WINDOW, KV_WIDTH)
    cwv = cache_win_v.reshape(DEC_BATCH, n_swa, WINDOW, KV_WIDTH)

    p_wkv, p_shift, s_wkv, s_shift = [], [], [], []
    p_wk, p_wv, p_mk, p_mv, s_wk, s_wv = [], [], [], [], [], []
    vfirst_p = vfirst_s = None
    for i in range(DEPTH):
        gains = norm_gains[i]
        j = i // 2
        if i % 2 == 0:
            p = dict(
                g0=row(gains[0]), mu=rwkv_mu[j],
                w_r=rwkv_w_r[j].astype(BF16), w_k=rwkv_w_k[j].astype(BF16), w_v=rwkv_w_v[j].astype(BF16),
                w0=row(rwkv_w0[j]), w1=_pad_cols(rwkv_w1[j]), w2=_pad_rows(rwkv_w2[j]),
                a0=row(rwkv_a0[j]), a1=_pad_cols(rwkv_a1[j]), a2=_pad_rows(rwkv_a2[j]),
                g1=_pad_cols(rwkv_g1[j]), g2=_pad_rows(rwkv_g2[j]),
                k_k=row(rwkv_k_k[j]), k_a=row(rwkv_k_a[j]), r_k=row(rwkv_r_k[j]),
                lnx_w=row(rwkv_lnx_w[j]), lnx_b=row(rwkv_lnx_b[j]),
            )
            if j > 0:
                p.update(v0=row(rwkv_v0[j - 1]), v1=_pad_cols(rwkv_v1[j - 1]), v2=_pad_rows(rwkv_v2[j - 1]))
            shift_rows = jnp.zeros((DEC_BATCH, DEC_SEQ, D_MODEL), F32).at[:, 0].set(state_shift[:, j])
            outs_p = _prep(x, x, vfirst_p if j > 0 else None, p, prompt=True)
            outs_s = _prep(x, shift_rows.reshape(SAMPLE_TOTAL, D_MODEL), vfirst_s if j > 0 else None, p,
                           prompt=False)
            if j == 0:
                vfirst_p, vfirst_s = outs_p[2], outs_s[2]
            zp, st_p = _chunk_prompt(outs_p[:6], p)
            zs, st_s = _chunk_sample(outs_s[:6], p, state_wkv, j)
            xn_tail = outs_p[6].reshape(BATCH, _PREP_TILES_PER_PROMPT, 8, D_MODEL)
            p_wkv.append(st_p)
            p_shift.append(xn_tail[:, -1, -1])
            s_wkv.append(st_s)
            s_shift.append(outs_s[6].reshape(DEC_BATCH, DEC_SEQ, D_MODEL)[:, -1])
            w_o = rwkv_w_o[j].astype(BF16)
        else:
            q, kv = _qkv(x, row(gains[0]), attn_w_qkv[j].astype(BF16))
            sinks = attn_sinks[j].astype(F32)
            zp = _attn_prompt(sinks, q, kv)
            zs, nwk, nwv = _attn_sample(sinks, q, kv, cmk, cmv, cwk, cwv, j)
            def kv_rows(lo, hi, which):
                parts = [kv[b * PROMPT_ROWS + lo:b * PROMPT_ROWS + hi, which * KV_WIDTH:(which + 1) * KV_WIDTH]
                         for b in range(BATCH)]
                return jnp.stack(parts).reshape(BATCH, hi - lo, N_KV_HEADS, HEAD)

            p_mk.append(kv_rows(PAD_ROWS, WINDOW, 0))
            p_mv.append(kv_rows(PAD_ROWS, WINDOW, 1))
            p_wk.append(kv_rows(PROMPT_ROWS - WINDOW, PROMPT_ROWS, 0))
            p_wv.append(kv_rows(PROMPT_ROWS - WINDOW, PROMPT_ROWS, 1))
            s_wk.append(nwk.reshape(DEC_BATCH, WINDOW, N_KV_HEADS, HEAD))
            s_wv.append(nwv.reshape(DEC_BATCH, WINDOW, N_KV_HEADS, HEAD))
            w_o = attn_w_o[j].astype(BF16)
        x = _oproj(zp, zs, x, w_o, row(gains[1]))
        x = _mlp(x, row(gains[2]), row(gains[3]), mlp_w_up[i].astype(BF16), mlp_w_down[i].astype(BF16))

    y_prompt = jnp.stack([x[b * PROMPT_ROWS + WINDOW:(b + 1) * PROMPT_ROWS] for b in range(BATCH)])
    y_sample = x[PROMPT_TOTAL:].reshape(DEC_BATCH, DEC_SEQ, D_MODEL)
    st = lambda xs: jnp.stack(xs, axis=1)
    return (y_prompt, y_sample, st(p_wkv), st(p_shift), st(p_wk), st(p_wv), st(p_mk), st(p_mv),
            st(s_wkv), st(s_shift), st(s_wk), st(s_wv))
```

```python
import functools

import jax
import jax.numpy as jnp
from jax import lax
from jax.experimental import pallas as pl
from jax.experimental.pallas import tpu as pltpu

F32 = jnp.float32
BF16 = jnp.bfloat16

D_MODEL = 1024
BATCH = 2
SEQ = 8192
DEPTH = 4
DEC_BATCH = 128
DEC_SEQ = 8
PAST_LEN = 8192
N_META = 16
HEADS = 16
HEAD = 64
N_KV_HEADS = 4
GROUP = HEADS // N_KV_HEADS
KV_WIDTH = N_KV_HEADS * HEAD
WINDOW = 128
D_FF = 4 * D_MODEL
RMS_EPS = 1e-6
GN_EPS = 6.4e-4
NEG_INF = -1e30
LORA_PAD = 128

PAD_ROWS = WINDOW - N_META
PROMPT_ROWS = PAD_ROWS + N_META + SEQ
PROMPT_TOTAL = BATCH * PROMPT_ROWS
SAMPLE_TOTAL = DEC_BATCH * DEC_SEQ
TOTAL_ROWS = PROMPT_TOTAL + SAMPLE_TOTAL

CHUNK = 64
SUB = 16
SEQ_GROUP = CHUNK // DEC_SEQ
TM_PROMPT = 640
TM_SAMPLE = 512
TM_PREP_SAMPLE = 256
TM_PREP = 320
TF = 512
VMEM_LIMIT = 56 * 1024 * 1024
SMALL_MM_PASSES = 1


def _cparams(sem):
    return pltpu.CompilerParams(dimension_semantics=sem, vmem_limit_bytes=VMEM_LIMIT)


def _rms(x, g):
    return x * lax.rsqrt(jnp.mean(x * x, axis=-1, keepdims=True) + RMS_EPS) * g


def _sigmoid(x):
    return 1.0 / (1.0 + jnp.exp(-x))


def _bdot(a, b):
    return jnp.dot(a.astype(BF16), b.astype(BF16), preferred_element_type=F32)


def _split(x):
    hi = x.astype(BF16)
    lo = (x - hi.astype(F32)).astype(BF16)
    return hi, lo


_NN = (((1,), (0,)), ((), ()))
_NT = (((1,), (1,)), ((), ()))
_TN = (((0,), (0,)), ((), ()))


def _mm(a, b, dims=_NN, passes=SMALL_MM_PASSES):
    dg = functools.partial(lax.dot_general, dimension_numbers=dims, preferred_element_type=F32)
    if passes == 1:
        return dg(a.astype(BF16), b.astype(BF16))
    ah, al = _split(a)
    bh, bl = _split(b)
    return dg(ah, bh) + (dg(ah, bl) + dg(al, bh))


def _div_pow2(x, n):
    shift = n.bit_length() - 1
    assert 1 << shift == n
    return jnp.right_shift(x, shift)


def _dot_exact_lhs(m01, x):
    hi, lo = _split(x)
    return (jnp.dot(m01, hi, preferred_element_type=F32)
            + jnp.dot(m01, lo, preferred_element_type=F32))


def _row_tile(rows):
    tm = TM_PROMPT if rows % TM_PROMPT == 0 else TM_SAMPLE
    assert rows % tm == 0
    return tm


def _mixer_mlp_body(x_ref, z_ref, wo_ref, g1_ref, g2_ref, g3_ref, wup_ref, wdn_ref, o_ref, x1_ref, xn_ref, acc_ref):
    f = pl.program_id(1)

    @pl.when(f == 0)
    def _():
        m = jnp.dot(z_ref[...], wo_ref[...], preferred_element_type=F32)
        x1 = x_ref[...] + _rms(m, g1_ref[...])
        x1_ref[...] = x1
        xn_ref[...] = _rms(x1, g2_ref[...]).astype(BF16)
        acc_ref[...] = jnp.zeros_like(acc_ref)

    h = jnp.dot(xn_ref[...], wup_ref[...], preferred_element_type=F32)
    a = jnp.maximum(h, 0.0)
    acc_ref[...] += jnp.dot((a * a).astype(BF16), wdn_ref[...], preferred_element_type=F32)

    @pl.when(f == pl.num_programs(1) - 1)
    def _():
        o_ref[...] = x1_ref[...] + _rms(acc_ref[...], g3_ref[...])


def _mixer_mlp(x, z, w_o, g1, g2, g3, wup, wdn):
    rows = x.shape[0]
    tm = _row_tile(rows)
    vec = pl.BlockSpec((1, D_MODEL), lambda i, f: (0, 0))
    return pl.pallas_call(
        _mixer_mlp_body,
        grid=(rows // tm, D_FF // TF),
        in_specs=[
            pl.BlockSpec((tm, D_MODEL), lambda i, f: (i, 0)),
            pl.BlockSpec((tm, D_MODEL), lambda i, f: (i, 0)),
            pl.BlockSpec((D_MODEL, D_MODEL), lambda i, f: (0, 0)),
            vec, vec, vec,
            pl.BlockSpec((D_MODEL, TF), lambda i, f: (0, f)),
            pl.BlockSpec((TF, D_MODEL), lambda i, f: (f, 0)),
        ],
        out_specs=pl.BlockSpec((tm, D_MODEL), lambda i, f: (i, 0)),
        out_shape=jax.ShapeDtypeStruct((rows, D_MODEL), F32),
        scratch_shapes=[pltpu.VMEM((tm, D_MODEL), F32), pltpu.VMEM((tm, D_MODEL), BF16),
                        pltpu.VMEM((tm, D_MODEL), F32)],
        compiler_params=_cparams(("parallel", "arbitrary")),
        name="mixer_mlp",
    )(x, z, w_o, g1, g2, g3, wup, wdn)


def _qkv_body(x_ref, g_ref, w_ref, q_ref, kv_ref):
    xn = _rms(x_ref[...], g_ref[...]).astype(BF16)
    res = jnp.dot(xn, w_ref[...], preferred_element_type=F32)
    q_ref[...] = (res[:, :D_MODEL] * (HEAD ** -0.5)).astype(BF16)
    kv_ref[...] = res[:, D_MODEL:]


def _qkv(x, g, w):
    rows = x.shape[0]
    tm = _row_tile(rows)
    return pl.pallas_call(
        _qkv_body,
        grid=(rows // tm,),
        in_specs=[
            pl.BlockSpec((tm, D_MODEL), lambda i: (i, 0)),
            pl.BlockSpec((1, D_MODEL), lambda i: (0, 0)),
            pl.BlockSpec((D_MODEL, D_MODEL + 2 * KV_WIDTH), lambda i: (0, 0)),
        ],
        out_specs=[
            pl.BlockSpec((tm, D_MODEL), lambda i: (i, 0)),
            pl.BlockSpec((tm, 2 * KV_WIDTH), lambda i: (i, 0)),
        ],
        out_shape=[
            jax.ShapeDtypeStruct((rows, D_MODEL), BF16),
            jax.ShapeDtypeStruct((rows, 2 * KV_WIDTH), F32),
        ],
        compiler_params=_cparams(("parallel",)),
        name="qkv",
    )(x, g, w)


def _alibi_slope(h):
    return 2.0 ** (-8.0 * (h + 1) / HEADS)


def _group_queries(q, kvh):
    return jnp.concatenate([q[:, h * HEAD:(h + 1) * HEAD] for h in range(kvh * GROUP, (kvh + 1) * GROUP)], axis=0)


def _group_columns(sink_ref, kvh, tq):
    g = _div_pow2(lax.broadcasted_iota(jnp.int32, (GROUP * tq, 1), 0), tq)
    slope = jnp.zeros((GROUP * tq, 1), F32)
    sink = jnp.zeros((GROUP * tq, 1), F32)
    for i in range(GROUP):
        h = kvh * GROUP + i
        slope = jnp.where(g == i, _alibi_slope(h), slope)
        sink = jnp.where(g == i, sink_ref[h], sink)
    return slope, sink


def _alibi_bias(ok, mind, slope):
    return jnp.where(ok, slope * mind, -NEG_INF)


def _attend_groups(qs, ks, vs, biases, sinks):
    scores = [lax.dot_general(q, k, _NT, preferred_element_type=F32) for q, k in zip(qs, ks)]
    probs = []
    for s, bias, sink in zip(scores, biases, sinks):
        logits = s - bias
        m = jnp.maximum(jnp.max(logits, axis=-1, keepdims=True), sink)
        p = jnp.exp(logits - m)
        den = jnp.sum(p, axis=-1, keepdims=True) + jnp.exp(sink - m)
        probs.append((p / den).astype(BF16))
    return [jnp.dot(p, v, preferred_element_type=F32) for p, v in zip(probs, vs)]


def _store_group(o_ref, rows, kvh, o, tq):
    for i in range(GROUP):
        h = kvh * GROUP + i
        o_ref[rows, h * HEAD:(h + 1) * HEAD] = o[i * tq:(i + 1) * tq].astype(o_ref.dtype)


_BLOCKS_PER_PROMPT = PROMPT_ROWS // WINDOW


def _attn_prompt_body(sink_ref, q_ref, kv0_ref, kvp_ref, kvc_ref, o_ref, bias_ref):
    i = pl.program_id(1)
    kvhs = range(N_KV_HEADS)

    @pl.when(i <= 2)
    def _():
        shp = (GROUP * WINDOW, 3 * WINDOW)
        r = jnp.bitwise_and(lax.broadcasted_iota(jnp.int32, shp, 0), WINDOW - 1)
        c = lax.broadcasted_iota(jnp.int32, shp, 1)
        kblk = _div_pow2(c, WINDOW)
        cc = c - kblk * WINDOW
        qpos = jnp.where(i == 0, r - PAD_ROWS, N_META + (i - 1) * WINDOW + r)
        kpos = jnp.where(kblk == 0, cc - PAD_ROWS, N_META + (i - 3 + kblk) * WINDOW + cc)
        kmeta = kblk == 0
        kvalid = jnp.where(kmeta, cc - PAD_ROWS, jnp.where(kblk == 1, i - 2, i - 1)) >= 0
        dist = qpos - kpos
        ok = kvalid & (dist >= 0) & (kmeta | (dist < WINDOW))
        mind = jnp.minimum(dist, WINDOW).astype(F32)
        for j in kvhs:
            bias_ref[j] = _alibi_bias(ok, mind, _group_columns(sink_ref, j, WINDOW)[0])

    kv = jnp.concatenate([kv0_ref[...], kvp_ref[...], kvc_ref[...]], axis=0).astype(BF16)
    q = q_ref[...]
    outs = _attend_groups(
        [_group_queries(q, j) for j in kvhs],
        [kv[:, j * HEAD:(j + 1) * HEAD] for j in kvhs],
        [kv[:, KV_WIDTH + j * HEAD:KV_WIDTH + (j + 1) * HEAD] for j in kvhs],
        [bias_ref[j] for j in kvhs],
        [_group_columns(sink_ref, j, WINDOW)[1] for j in kvhs])
    for j in kvhs:
        _store_group(o_ref, slice(None), j, outs[j], WINDOW)


def _attn_prompt(sinks, q, kv):
    nb = _BLOCKS_PER_PROMPT
    return pl.pallas_call(
        _attn_prompt_body,
        grid=(BATCH, nb),
        in_specs=[
            pl.BlockSpec(memory_space=pltpu.SMEM),
            pl.BlockSpec((WINDOW, D_MODEL), lambda b, i: (b * nb + i, 0)),
            pl.BlockSpec((WINDOW, 2 * KV_WIDTH), lambda b, i: (b * nb, 0)),
            pl.BlockSpec((WINDOW, 2 * KV_WIDTH), lambda b, i: (b * nb + jnp.maximum(i - 1, 0), 0)),
            pl.BlockSpec((WINDOW, 2 * KV_WIDTH), lambda b, i: (b * nb + i, 0)),
        ],
        out_specs=pl.BlockSpec((WINDOW, D_MODEL), lambda b, i: (b * nb + i, 0)),
        out_shape=jax.ShapeDtypeStruct((PROMPT_TOTAL, D_MODEL), BF16),
        scratch_shapes=[pltpu.VMEM((N_KV_HEADS, GROUP * WINDOW, 3 * WINDOW), F32)],
        compiler_params=_cparams(("arbitrary", "arbitrary")),
        name="attn_prompt",
    )(sinks, q, kv, kv, kv)


_ATT_SEQS = 8
_SAMPLE_KEYS = N_META + WINDOW + DEC_SEQ


def _attn_sample_body(sink_ref, q_ref, kv_ref, mk_ref, mv_ref, wk_ref, wv_ref, o_ref, nwk_ref, nwv_ref):
    shp = (GROUP * DEC_SEQ, _SAMPLE_KEYS)
    t = jnp.bitwise_and(lax.broadcasted_iota(jnp.int32, shp, 0), DEC_SEQ - 1)
    c = lax.broadcasted_iota(jnp.int32, shp, 1)
    qpos = PAST_LEN + t
    kpos = jnp.where(c < N_META, c,
                     jnp.where(c < N_META + WINDOW, PAST_LEN - WINDOW + (c - N_META),
                               PAST_LEN + (c - N_META - WINDOW)))
    kmeta = c < N_META
    kvalid = kmeta | (kpos >= N_META)
    dist = qpos - kpos
    ok = kvalid & (dist >= 0) & (kmeta | (dist < WINDOW))
    mind = jnp.minimum(dist, WINDOW).astype(F32)
    kvhs = range(N_KV_HEADS)
    cols = [_group_columns(sink_ref, j, DEC_SEQ) for j in kvhs]
    biases = [_alibi_bias(ok, mind, cols[j][0]) for j in kvhs]
    qs, ks, vs, bs, cs = [], [], [], [], []
    for s in range(_ATT_SEQS):
        rows = slice(s * DEC_SEQ, (s + 1) * DEC_SEQ)
        knew = kv_ref[rows, :KV_WIDTH]
        vnew = kv_ref[rows, KV_WIDTH:]
        wk = wk_ref[s]
        wv = wv_ref[s]
        nwk_ref[s] = jnp.concatenate([wk[DEC_SEQ:], knew], axis=0)
        nwv_ref[s] = jnp.concatenate([wv[DEC_SEQ:], vnew], axis=0)
        kall = jnp.concatenate([mk_ref[s], wk, knew], axis=0).astype(BF16)
        vall = jnp.concatenate([mv_ref[s], wv, vnew], axis=0).astype(BF16)
        q = q_ref[rows, :]
        for j in kvhs:
            qs.append(_group_queries(q, j))
            ks.append(kall[:, j * HEAD:(j + 1) * HEAD])
            vs.append(vall[:, j * HEAD:(j + 1) * HEAD])
            bs.append(biases[j])
            cs.append(cols[j][1])
    outs = _attend_groups(qs, ks, vs, bs, cs)
    for s in range(_ATT_SEQS):
        for j in kvhs:
            _store_group(o_ref, slice(s * DEC_SEQ, (s + 1) * DEC_SEQ), j, outs[s * N_KV_HEADS + j], DEC_SEQ)


def _attn_sample(sinks, q, kv, cmk, cmv, cwk, cwv, layer):
    rows = _ATT_SEQS * DEC_SEQ
    ng = DEC_BATCH // _ATT_SEQS
    cache_spec = lambda n: pl.BlockSpec((_ATT_SEQS, None, n, KV_WIDTH), lambda g: (g, layer, 0, 0))
    return pl.pallas_call(
        _attn_sample_body,
        grid=(ng,),
        in_specs=[
            pl.BlockSpec(memory_space=pltpu.SMEM),
            pl.BlockSpec((rows, D_MODEL), lambda g: (g, 0)),
            pl.BlockSpec((rows, 2 * KV_WIDTH), lambda g: (g, 0)),
            cache_spec(N_META), cache_spec(N_META), cache_spec(WINDOW), cache_spec(WINDOW),
        ],
        out_specs=[
            pl.BlockSpec((rows, D_MODEL), lambda g: (g, 0)),
            pl.BlockSpec((_ATT_SEQS, WINDOW, KV_WIDTH), lambda g: (g, 0, 0)),
            pl.BlockSpec((_ATT_SEQS, WINDOW, KV_WIDTH), lambda g: (g, 0, 0)),
        ],
        out_shape=[
            jax.ShapeDtypeStruct((SAMPLE_TOTAL, D_MODEL), BF16),
            jax.ShapeDtypeStruct((DEC_BATCH, WINDOW, KV_WIDTH), F32),
            jax.ShapeDtypeStruct((DEC_BATCH, WINDOW, KV_WIDTH), F32),
        ],
        compiler_params=_cparams(("parallel",)),
        name="attn_sample",
    )(sinks, q, kv, cmk, cmv, cwk, cwv)


_PREP_TILES_PER_PROMPT = PROMPT_ROWS // TM_PREP
_DECAY_SCALE = 0.6065306597126334


def _prep_body(*refs, prompt, vres):
    it = iter(refs)
    x_ref = next(it)
    side_ref = next(it)
    vf_ref = next(it) if vres else None
    g0_ref, mu_ref = next(it), next(it)
    wr_ref, wk_ref, wv_ref = next(it), next(it), next(it)
    w0_ref, w1_ref, w2_ref = next(it), next(it), next(it)
    a0_ref, a1_ref, a2_ref = next(it), next(it), next(it)
    if vres:
        v0_ref, v1_ref, v2_ref = next(it), next(it), next(it)
    g1_ref, g2_ref = next(it), next(it)
    r_ref, k_ref, v_ref, a_ref, lw_ref, g_ref, xn_ref = (next(it) for _ in range(7))

    tm = x_ref.shape[0]
    g0 = g0_ref[...]
    xn = _rms(x_ref[...], g0)
    row = lax.broadcasted_iota(jnp.int32, (tm, 1), 0)
    if prompt:
        ib = lax.rem(pl.program_id(0), _PREP_TILES_PER_PROMPT)
        xn = jnp.where(ib * tm + row < PAD_ROWS, 0.0, xn)
        before = _rms(side_ref[...], g0)[7:8, :]
        before = jnp.where(ib == 0, 0.0, before)
        x_prev = jnp.where(row == 0, before, pltpu.roll(xn, 1, 0))
        xn_ref[...] = xn[tm - 8:, :]
    else:
        x_prev = jnp.where(jnp.bitwise_and(row, DEC_SEQ - 1) == 0, side_ref[...], pltpu.roll(xn, 1, 0))
        xn_ref[...] = xn
    xx = x_prev - xn
    xr, xw, xk, xv, xa, xg = (xn + xx * mu_ref[c:c + 1, :] for c in range(6))

    r_ref[...] = _bdot(xr, wr_ref[...])
    k_ref[...] = _bdot(xk, wk_ref[...])
    v = _bdot(xv, wv_ref[...])
    wl = w0_ref[...] + _bdot(jnp.tanh(_bdot(xw, w1_ref[...])), w2_ref[...])
    lw_ref[...] = -_DECAY_SCALE * _sigmoid(wl)
    if vres:
        mix = _sigmoid(v0_ref[...] + _bdot(_bdot(xv, v1_ref[...]), v2_ref[...]))
        v = v + (vf_ref[...] - v) * mix
    v_ref[...] = v
    a_ref[...] = _sigmoid(a0_ref[...] + _bdot(_bdot(xa, a1_ref[...]), a2_ref[...]))
    g_ref[...] = _bdot(_sigmoid(_bdot(xg, g1_ref[...])), g2_ref[...])


def _prep(x, side, vfirst, p, prompt):
    vres = vfirst is not None
    if prompt:
        tm, nt, nrows = TM_PREP, PROMPT_TOTAL // TM_PREP, PROMPT_TOTAL
        side_spec = pl.BlockSpec((8, D_MODEL), lambda i: (jnp.maximum(i * (TM_PREP // 8) - 1, 0), 0))
        xn_spec = pl.BlockSpec((8, D_MODEL), lambda i: (i, 0))
        xn_shape = jax.ShapeDtypeStruct((nt * 8, D_MODEL), F32)
    else:
        tm, nt, nrows = TM_PREP_SAMPLE, SAMPLE_TOTAL // TM_PREP_SAMPLE, SAMPLE_TOTAL
        side_spec = pl.BlockSpec((tm, D_MODEL), lambda i: (i, 0))
        xn_spec = pl.BlockSpec((tm, D_MODEL), lambda i: (i, 0))
        xn_shape = jax.ShapeDtypeStruct((nrows, D_MODEL), F32)
    row_spec = pl.BlockSpec((tm, D_MODEL), lambda i: (i, 0))
    const = lambda a: pl.BlockSpec(a.shape, lambda i: (0,) * a.ndim)

    args = [x, side]
    specs = [row_spec, side_spec]
    if vres:
        args.append(vfirst)
        specs.append(row_spec)
    names = ["g0", "mu", "w_r", "w_k", "w_v", "w0", "w1", "w2", "a0", "a1", "a2"]
    if vres:
        names += ["v0", "v1", "v2"]
    names += ["g1", "g2"]
    for n in names:
        args.append(p[n])
        specs.append(const(p[n]))
    big = jax.ShapeDtypeStruct((nrows, D_MODEL), F32)
    return pl.pallas_call(
        functools.partial(_prep_body, prompt=prompt, vres=vres),
        grid=(nt,),
        in_specs=specs,
        out_specs=[row_spec] * 6 + [xn_spec],
        out_shape=[big] * 6 + [xn_shape],
        compiler_params=_cparams(("parallel",)),
        name="rwkv_prep_prompt" if prompt else "rwkv_prep_sample",
    )(*args)


def _eye(n):
    r = lax.broadcasted_iota(jnp.int32, (n, n), 0)
    c = lax.broadcasted_iota(jnp.int32, (n, n), 1)
    return (r == c).astype(F32)


def _mms(a_list, b_list, dims=_NN):
    return [_mm(a, b, dims) for a, b in zip(a_list, b_list)]


def _unit_lower_inverse(n_list, sub, nblocks):
    n = n_list[0].shape[0]
    eye = _eye(n)
    r = lax.broadcasted_iota(jnp.int32, (n, n), 0)
    c = lax.broadcasted_iota(jnp.int32, (n, n), 1)
    diag_blk = _div_pow2(r, sub) == _div_pow2(c, sub)
    ms = [jnp.where(diag_blk, -x, 0.0) for x in n_list]
    tds = [eye + m for m in ms]
    span = 1
    while 2 * span < sub:
        ms = _mms(ms, ms)
        tds = [td + p for td, p in zip(tds, _mms(tds, ms))]
        span *= 2
    if nblocks == 1:
        return tds
    xs = _mms(tds, [jnp.where(diag_blk, 0.0, x) for x in n_list])
    zs = [eye - x for x in xs]
    if nblocks > 2:
        assert nblocks <= 4
        zs = [z + p for z, p in zip(zs, _mms(zs, _mms(xs, xs)))]
    return _mms(zs, tds)


def _stacked_causal_mask(c, seq_len):
    row = lax.broadcasted_iota(jnp.int32, (2 * c, 2 * c), 0)
    t = jnp.bitwise_and(row, c - 1)
    j = jnp.bitwise_and(lax.broadcasted_iota(jnp.int32, (2 * c, 2 * c), 1), c - 1)
    same = _div_pow2(t, seq_len) == _div_pow2(j, seq_len)
    return same & (t - j >= jnp.where(row < c, 1, 0))


def _chunk_prologue(refs, seq_len, lead=None):
    r_ref, k_ref, v_ref, a_ref, lw_ref, g_ref = (x if lead is None else x.at[lead] for x in refs[:6])
    kk_ref, ka_ref = refs[6:]
    c = CHUNK
    row = lax.broadcasted_iota(jnp.int32, (c, c), 0)
    col = lax.broadcasted_iota(jnp.int32, (c, c), 1)
    same = _div_pow2(row, seq_len) == _div_pow2(col, seq_len)
    strict = same & (row > col)
    incl = same & (row >= col)
    lw = lw_ref[...]
    one01 = lambda m: jnp.where(m, 1.0, 0.0).astype(BF16)
    cum = _dot_exact_lhs(one01(incl), lw)
    cum_end = _dot_exact_lhs(one01(same), lw)
    k_raw = k_ref[...]
    a = a_ref[...]
    q = dict(
        mask2=_stacked_causal_mask(c, seq_len),
        r=r_ref[...], v=v_ref[...], a=a, g=g_ref[...],
        k=k_raw * (1.0 + (a - 1.0) * ka_ref[...]),
        kk_un=k_raw * kk_ref[...],
        e_cum=jnp.exp(cum), e_neg=jnp.exp(-cum), e_excl=jnp.exp(cum - lw),
        e_end=jnp.exp(cum_end - cum), p_end=jnp.exp(cum_end),
    )
    return q


def _head_terms(qs, sub, nblocks):
    hs = range(len(qs) * HEADS)
    col = lambda name: [q[name][:, h * HEAD:(h + 1) * HEAD] for q in qs for h in range(HEADS)]
    rh, kh, vh, ah, kk = col("r"), col("k"), col("v"), col("a"), col("kk_un")
    e_excl, e_cum, e_neg, e_end = col("e_excl"), col("e_cum"), col("e_neg"), col("e_end")
    kk = [x / jnp.maximum(jnp.sqrt(jnp.sum(x * x, axis=-1, keepdims=True)), 1e-12) for x in kk]
    bh = [kk[h] * ah[h] for h in hs]
    at = [kk[h] * e_excl[h] for h in hs]
    rt = [rh[h] * e_cum[h] for h in hs]
    bt = [bh[h] * e_neg[h] for h in hs]
    kt = [kh[h] * e_neg[h] for h in hs]
    bk_end = [jnp.concatenate([bh[h] * e_end[h], kh[h] * e_end[h]], axis=0) for h in hs]
    a_all = _mms([jnp.concatenate([at[h], rt[h]], axis=0) for h in hs],
                 [jnp.concatenate([bt[h], kt[h]], axis=0) for h in hs], _NT)
    a_all = [jnp.where(qs[0]["mask2"], x, 0.0) for x in a_all]
    t = _unit_lower_inverse([x[:CHUNK, :CHUNK] for x in a_all], sub, nblocks)
    w = _mms(t, at)
    ubar = [-x for x in _mms(t, _mms([x[:CHUNK, CHUNK:] for x in a_all], vh))]
    return dict(rh=rh, kh=kh, vh=vh, w_rt=[jnp.concatenate([w[h], rt[h]], axis=0) for h in hs], ubar=ubar,
                a_r=[x[CHUNK:, :] for x in a_all], bk_end=bk_end, p_end=col("p_end"), g=col("g"))


def _head_outputs(t, ys, rk_ref, lnw_ref, lnb_ref, z_refs):
    ns = range(len(ys))
    sls = [slice((n % HEADS) * HEAD, (n % HEADS + 1) * HEAD) for n in ns]
    bonus = [jnp.sum(t["rh"][n] * t["kh"][n] * rk_ref[:, sls[n]], axis=-1, keepdims=True) for n in ns]
    mean = [jnp.mean(y, axis=-1, keepdims=True) for y in ys]
    d = [ys[n] - mean[n] for n in ns]
    var = [jnp.mean(x * x, axis=-1, keepdims=True) for x in d]
    for n in ns:
        yn = d[n] * lax.rsqrt(var[n] + GN_EPS) * lnw_ref[:, sls[n]] + lnb_ref[:, sls[n]]
        z_refs[n][:, sls[n]] = ((yn + bonus[n] * t["vh"][n]) * t["g"][n]).astype(BF16)


QUAD = 4
QW = QUAD * HEAD
NQ = HEADS // QUAD
CHUNKS_PER_STEP = 2


def _bf(x):
    return x.astype(BF16)


def _dg(a, b, dims=_NN):
    return lax.dot_general(_bf(a), _bf(b), dims, preferred_element_type=F32)


def _seg_sum(x, e_seg):
    hi, lo = _split(x)
    both = jnp.dot(jnp.concatenate([hi, lo], axis=0), e_seg, preferred_element_type=F32)
    return both[:x.shape[0]] + both[x.shape[0]:]


def _chunk_prompt_body(r_ref, k_ref, v_ref, a_ref, lw_ref, g_ref, kk_ref, ka_ref, rk_ref, lnw_ref, lnb_ref,
                       z_ref, sout_ref, s_ref):
    ci = pl.program_id(1)

    @pl.when(ci == 0)
    def _():
        s_ref[...] = jnp.zeros_like(s_ref)

    rows = r_ref.shape[1]
    nsub = rows // CHUNK
    i32 = jnp.int32
    t64 = lax.broadcasted_iota(i32, (CHUNK, QW), 0)
    j64 = jnp.bitwise_and(lax.broadcasted_iota(i32, (CHUNK, QW), 1), HEAD - 1)
    strict = t64 > j64
    diag_blk = _div_pow2(t64, SUB) == _div_pow2(j64, SUB)
    eye_q = jnp.where(t64 == j64, 1.0, 0.0)
    t2 = lax.broadcasted_iota(i32, (CHUNK, 2 * QW), 0)
    j2 = jnp.bitwise_and(lax.broadcasted_iota(i32, (CHUNK, 2 * QW), 1), HEAD - 1)
    incl2 = t2 >= j2
    same_head = (_div_pow2(lax.broadcasted_iota(i32, (QW, QW), 0), HEAD)
                 == _div_pow2(lax.broadcasted_iota(i32, (QW, QW), 1), HEAD))
    bd_f32 = jnp.where(same_head, 1.0, 0.0)
    bd_bf = bd_f32.astype(BF16)

    lane = lax.broadcasted_iota(i32, (CHUNK, 2 * HEAD), 1)
    keep_lo = jnp.where(lane < HEAD, 1.0, 0.0).astype(BF16)
    keep_hi = jnp.where(lane >= HEAD, 1.0, 0.0).astype(BF16)
    zero_t = jnp.zeros((CHUNK, 2 * HEAD), BF16)

    def bd(x):
        xb = _bf(x)
        x0, x1 = xb[:, :2 * HEAD], xb[:, 2 * HEAD:]
        return jnp.concatenate([
            jnp.concatenate([x0 * keep_lo, zero_t], axis=1), jnp.concatenate([x0 * keep_hi, zero_t], axis=1),
            jnp.concatenate([zero_t, x1 * keep_lo], axis=1), jnp.concatenate([zero_t, x1 * keep_hi], axis=1),
        ], axis=0)

    rr = lax.broadcasted_iota(i32, (rows, rows), 0)
    cc = lax.broadcasted_iota(i32, (rows, rows), 1)
    same_chunk = _div_pow2(rr, CHUNK) == _div_pow2(cc, CHUNK)
    one01 = lambda m: jnp.where(m, 1.0, 0.0).astype(BF16)
    lw = lw_ref[0]
    cum = _dot_exact_lhs(one01(same_chunk & (rr >= cc)), lw)
    cum_end = _dot_exact_lhs(one01(same_chunk), lw)
    r, v, a, k_raw = r_ref[0], v_ref[0], a_ref[0], k_ref[0]
    k_mod = k_raw * (1.0 + (a - 1.0) * ka_ref[...])
    kk_un = k_raw * kk_ref[...]
    sq = kk_un * kk_un
    rkk = r * k_mod * rk_ref[...]
    qsl = [slice(q * QW, (q + 1) * QW) for q in range(NQ)]
    sums = [_seg_sum(jnp.concatenate([sq[:, s], rkk[:, s]], axis=0), bd_bf) for s in qsl]
    ss = jnp.concatenate([x[:rows] for x in sums], axis=1)
    bonus = jnp.concatenate([x[rows:] for x in sums], axis=1) * v
    kk = kk_un / jnp.maximum(jnp.sqrt(ss), 1e-12)
    b = kk * a
    e_neg = jnp.exp(-cum)
    e_end = jnp.exp(cum_end - cum)
    at = kk * jnp.exp(cum - lw)
    rt = r * jnp.exp(cum)
    bt = b * e_neg
    kt = k_mod * e_neg
    b_end = b * e_end
    k_end = k_mod * e_end
    p_end = jnp.exp(cum_end)

    probs = [(g, q) for g in range(nsub) for q in range(NQ)]
    blk = lambda x, g, q: x[g * CHUNK:(g + 1) * CHUNK, qsl[q]]
    np_ = range(len(probs))

    a_all = [_dg(jnp.concatenate([blk(at, g, q), blk(rt, g, q)], axis=0),
                 jnp.concatenate([bd(blk(bt, g, q)), bd(blk(kt, g, q))], axis=0), _NT) for g, q in probs]
    n_ab = [jnp.where(strict, x[:CHUNK, :QW], 0.0) for x in a_all]
    a_ak = [jnp.where(strict, x[:CHUNK, QW:], 0.0) for x in a_all]
    a_r = [jnp.where(incl2, x[CHUNK:, :], 0.0) for x in a_all]
    bd_v = [bd(blk(v, g, q)) for g, q in probs]
    akv = [_dg(a_ak[i], bd_v[i]) for i in np_]
    m = [jnp.where(diag_blk, -x, 0.0) for x in n_ab]
    td = [eye_q + x for x in m]
    m = [_dg(x, bd(x)) for x in m]
    for _ in range(2):
        both = [_dg(jnp.concatenate([m[i], td[i]], axis=0), bd(m[i])) for i in np_]
        m = [x[:CHUNK] for x in both]
        td = [td[i] + both[i][CHUNK:] for i in np_]
    td = [td[i] + _dg(td[i], bd(m[i])) for i in np_]
    xo = [_dg(td[i], bd(jnp.where(diag_blk, 0.0, n_ab[i]))) for i in np_]
    x2 = [_dg(x, bd(x)) for x in xo]
    zz = [eye_q - x for x in xo]
    zz = [zz[i] + _dg(zz[i], bd(x2[i])) for i in np_]
    t_inv = [_dg(zz[i], bd(td[i])) for i in np_]
    wu = [_dg(t_inv[i], jnp.concatenate([bd(blk(at, *probs[i])), bd(akv[i])], axis=1)) for i in np_]

    y_rows = []
    for g in range(nsub):
        ids = [g * NQ + q for q in range(NQ)]
        s0 = [s_ref[q] for q in range(NQ)]
        x = [_dg(jnp.concatenate([wu[i][:, :QW], blk(rt, g, q)], axis=0), s0[q], _NT)
             for q, i in enumerate(ids)]
        u = [-wu[i][:, QW:] - x[q][:CHUNK] for q, i in enumerate(ids)]
        y = [x[q][CHUNK:] + _dg(a_r[i], jnp.concatenate([bd(u[q]), bd_v[i]], axis=0))
             for q, i in enumerate(ids)]
        ds = [_dg(jnp.concatenate([u[q], blk(v, g, q)], axis=0),
                  jnp.concatenate([blk(b_end, g, q), blk(k_end, g, q)], axis=0), _TN) for q in range(NQ)]
        for q in range(NQ):
            s_ref[q] = s0[q] * p_end[g * CHUNK:g * CHUNK + 1, qsl[q]] + ds[q] * bd_f32
        y_rows.append(jnp.concatenate(y, axis=1))
    y = jnp.concatenate(y_rows, axis=0)

    inv_n = 1.0 / HEAD
    mean = jnp.concatenate([_seg_sum(y[:, s], bd_bf) for s in qsl], axis=1) * inv_n
    d = y - mean
    var = jnp.concatenate([_seg_sum((d * d)[:, s], bd_bf) for s in qsl], axis=1) * inv_n
    yn = d * lax.rsqrt(var + GN_EPS) * lnw_ref[...] + lnb_ref[...]
    z_ref[0] = ((yn + bonus) * g_ref[0]).astype(BF16)

    @pl.when(ci == pl.num_programs(1) - 1)
    def _():
        for h in range(HEADS):
            o = (h % QUAD) * HEAD
            sout_ref[0, h] = s_ref[h // QUAD, o:o + HEAD, o:o + HEAD]


def _chunk_prompt(proj, p):
    rows = CHUNKS_PER_STEP * CHUNK
    nc = PROMPT_ROWS // rows
    row_spec = pl.BlockSpec((1, rows, D_MODEL), lambda b, c: (b, c, 0))
    vec_spec = pl.BlockSpec((1, D_MODEL), lambda b, c: (0, 0))
    z, st = pl.pallas_call(
        _chunk_prompt_body,
        grid=(BATCH, nc),
        in_specs=[row_spec] * 6 + [vec_spec] * 5,
        out_specs=[
            row_spec,
            pl.BlockSpec((1, HEADS, HEAD, HEAD), lambda b, c: (b, 0, 0, 0)),
        ],
        out_shape=[
            jax.ShapeDtypeStruct((BATCH, PROMPT_ROWS, D_MODEL), BF16),
            jax.ShapeDtypeStruct((BATCH, HEADS, HEAD, HEAD), F32),
        ],
        scratch_shapes=[pltpu.VMEM((NQ, QW, QW), F32)],
        compiler_params=_cparams(("parallel", "arbitrary")),
        name="rwkv_chunk_prompt",
    )(*[x.reshape(BATCH, PROMPT_ROWS, D_MODEL) for x in proj],
      p["k_k"], p["k_a"], p["r_k"], p["lnx_w"], p["lnx_b"])
    return z.reshape(PROMPT_TOTAL, D_MODEL), st


def _chunk_sample_body(r_ref, k_ref, v_ref, a_ref, lw_ref, g_ref, kk_ref, ka_ref, rk_ref, lnw_ref, lnb_ref,
                       sin_ref, z_ref, sout_ref):
    q = _chunk_prologue((r_ref, k_ref, v_ref, a_ref, lw_ref, g_ref, kk_ref, ka_ref), DEC_SEQ)
    seq_of_row2 = _div_pow2(jnp.bitwise_and(lax.broadcasted_iota(jnp.int32, (2 * CHUNK, 1), 0), CHUNK - 1),
                            DEC_SEQ)
    hs = range(HEADS)
    seqs = range(SEQ_GROUP)
    rows = [slice(s * DEC_SEQ, (s + 1) * DEC_SEQ) for s in seqs]
    t = _head_terms([q], DEC_SEQ, 1)
    rows2 = [slice(CHUNK + s * DEC_SEQ, CHUNK + (s + 1) * DEC_SEQ) for s in seqs]
    ws = [jnp.concatenate([_mm(t["w_rt"][h][rows[s]], sin_ref[s, h], _NT) for s in seqs], axis=0) for h in hs]
    rs = [jnp.concatenate([_mm(t["w_rt"][h][rows2[s]], sin_ref[s, h], _NT) for s in seqs], axis=0) for h in hs]
    uv = [jnp.concatenate([t["ubar"][h] - ws[h], t["vh"][h]], axis=0) for h in hs]
    y = _mms(t["a_r"], uv)
    for h in hs:
        for s in seqs:
            mine = jnp.where(seq_of_row2 == s, uv[h], 0.0)
            p_end = t["p_end"][h][s * DEC_SEQ:s * DEC_SEQ + 1, :]
            sout_ref[s, h] = sin_ref[s, h] * p_end + _mm(mine, t["bk_end"][h], _TN)
    _head_outputs(t, [rs[h] + y[h] for h in hs], rk_ref, lnw_ref, lnb_ref, [z_ref] * HEADS)


def _chunk_sample(proj, p, state_wkv, layer):
    ng = DEC_BATCH // SEQ_GROUP
    row_spec = pl.BlockSpec((CHUNK, D_MODEL), lambda g: (g, 0))
    vec_spec = pl.BlockSpec((1, D_MODEL), lambda g: (0, 0))
    return pl.pallas_call(
        _chunk_sample_body,
        grid=(ng,),
        in_specs=[row_spec] * 6 + [vec_spec] * 5 + [
            pl.BlockSpec((SEQ_GROUP, None, HEADS, HEAD, HEAD), lambda g: (g, layer, 0, 0, 0)),
        ],
        out_specs=[
            row_spec,
            pl.BlockSpec((SEQ_GROUP, HEADS, HEAD, HEAD), lambda g: (g, 0, 0, 0)),
        ],
        out_shape=[
            jax.ShapeDtypeStruct((SAMPLE_TOTAL, D_MODEL), BF16),
            jax.ShapeDtypeStruct((DEC_BATCH, HEADS, HEAD, HEAD), F32),
        ],
        compiler_params=_cparams(("parallel",)),
        name="rwkv_chunk_sample",
    )(*proj, p["k_k"], p["k_a"], p["r_k"], p["lnx_w"], p["lnx_b"], state_wkv)


def _pad_cols(w):
    return jnp.pad(w, ((0, 0), (0, LORA_PAD - w.shape[1]))).astype(BF16)


def _pad_rows(w):
    return jnp.pad(w, ((0, LORA_PAD - w.shape[0]), (0, 0))).astype(BF16)


def kernel(x_prompt, x_sample, state_wkv, state_shift, cache_win_k, cache_win_v, cache_meta_k,
           cache_meta_v, meta_tokens, norm_gains, rwkv_mu, rwkv_w_r, rwkv_w_k, rwkv_w_v, rwkv_w_o,
           rwkv_w0, rwkv_w1, rwkv_w2, rwkv_a0, rwkv_a1, rwkv_a2, rwkv_v0, rwkv_v1, rwkv_v2,
           rwkv_g1, rwkv_g2, rwkv_k_k, rwkv_k_a, rwkv_r_k, rwkv_lnx_w, rwkv_lnx_b,
           attn_w_qkv, attn_w_o, attn_sinks, mlp_w_up, mlp_w_down):
    assert x_prompt.shape == (BATCH, SEQ, D_MODEL) and x_sample.shape == (DEC_BATCH, DEC_SEQ, D_MODEL)
    n_swa = cache_win_k.shape[1]
    row = lambda v: v.reshape(1, D_MODEL).astype(F32)

    head = jnp.concatenate([jnp.zeros((PAD_ROWS, D_MODEL), F32), meta_tokens.astype(F32)], axis=0)
    xp = jnp.concatenate([jnp.broadcast_to(head[None], (BATCH, WINDOW, D_MODEL)), x_prompt], axis=1)
    xp = xp.reshape(PROMPT_TOTAL, D_MODEL)
    xs = x_sample.reshape(SAMPLE_TOTAL, D_MODEL).astype(F32)

    cmk = cache_meta_k.reshape(DEC_BATCH, n_swa, N_META, KV_WIDTH)
    cmv = cache_meta_v.reshape(DEC_BATCH, n_swa, N_META, KV_WIDTH)
    cwk = cache_win_k.reshape(DEC_BATCH, n_swa, WINDOW, KV_WIDTH)
    cwv = cache_win_v.reshape(DEC_BATCH, n_swa, WINDOW, KV_WIDTH)

    p_wkv, p_shift, s_wkv, s_shift = [], [], [], []
    p_wk, p_wv, p_mk, p_mv, s_wk, s_wv = [], [], [], [], [], []
    vfirst_p = vfirst_s = None
    for i in range(DEPTH):
        gains = norm_gains[i]
        j = i // 2
        if i % 2 == 0:
            p = dict(
                g0=row(gains[0]), mu=rwkv_mu[j],
                w_r=rwkv_w_r[j].astype(BF16), w_k=rwkv_w_k[j].astype(BF16), w_v=rwkv_w_v[j].astype(BF16),
                w0=row(rwkv_w0[j]), w1=_pad_cols(rwkv_w1[j]), w2=_pad_rows(rwkv_w2[j]),
                a0=row(rwkv_a0[j]), a1=_pad_cols(rwkv_a1[j]), a2=_pad_rows(rwkv_a2[j]),
                g1=_pad_cols(rwkv_g1[j]), g2=_pad_rows(rwkv_g2[j]),
                k_k=row(rwkv_k_k[j]), k_a=row(rwkv_k_a[j]), r_k=row(rwkv_r_k[j]),
                lnx_w=row(rwkv_lnx_w[j]), lnx_b=row(rwkv_lnx_b[j]),
            )
            if j > 0:
                p.update(v0=row(rwkv_v0[j - 1]), v1=_pad_cols(rwkv_v1[j - 1]), v2=_pad_rows(rwkv_v2[j - 1]))
            shift_rows = jnp.zeros((DEC_BATCH, DEC_SEQ, D_MODEL), F32).at[:, 0].set(state_shift[:, j])
            outs_p = _prep(xp, xp, vfirst_p if j > 0 else None, p, prompt=True)
            outs_s = _prep(xs, shift_rows.reshape(SAMPLE_TOTAL, D_MODEL), vfirst_s if j > 0 else None, p,
                           prompt=False)
            if j == 0:
                vfirst_p, vfirst_s = outs_p[2], outs_s[2]
            zp, st_p = _chunk_prompt(outs_p[:6], p)
            zs, st_s = _chunk_sample(outs_s[:6], p, state_wkv, j)
            xn_tail = outs_p[6].reshape(BATCH, _PREP_TILES_PER_PROMPT, 8, D_MODEL)
            p_wkv.append(st_p)
            p_shift.append(xn_tail[:, -1, -1])
            s_wkv.append(st_s)
            s_shift.append(outs_s[6].reshape(DEC_BATCH, DEC_SEQ, D_MODEL)[:, -1])
            w_o = rwkv_w_o[j].astype(BF16)
        else:
            w_qkv = attn_w_qkv[j].astype(BF16)
            q, kv = _qkv(xp, row(gains[0]), w_qkv)
            q_s, kv_s = _qkv(xs, row(gains[0]), w_qkv)
            sinks = attn_sinks[j].astype(F32)
            zp = _attn_prompt(sinks, q, kv)
            zs, nwk, nwv = _attn_sample(sinks, q_s, kv_s, cmk, cmv, cwk, cwv, j)
            def kv_rows(lo, hi, which):
                parts = [kv[b * PROMPT_ROWS + lo:b * PROMPT_ROWS + hi, which * KV_WIDTH:(which + 1) * KV_WIDTH]
                         for b in range(BATCH)]
                return jnp.stack(parts).reshape(BATCH, hi - lo, N_KV_HEADS, HEAD)

            p_mk.append(kv_rows(PAD_ROWS, WINDOW, 0))
            p_mv.append(kv_rows(PAD_ROWS, WINDOW, 1))
            p_wk.append(kv_rows(PROMPT_ROWS - WINDOW, PROMPT_ROWS, 0))
            p_wv.append(kv_rows(PROMPT_ROWS - WINDOW, PROMPT_ROWS, 1))
            s_wk.append(nwk.reshape(DEC_BATCH, WINDOW, N_KV_HEADS, HEAD))
            s_wv.append(nwv.reshape(DEC_BATCH, WINDOW, N_KV_HEADS, HEAD))
            w_o = attn_w_o[j].astype(BF16)
        tail = (w_o, row(gains[1]), row(gains[2]), row(gains[3]), mlp_w_up[i].astype(BF16),
                mlp_w_down[i].astype(BF16))
        xp = _mixer_mlp(xp, zp, *tail)
        xs = _mixer_mlp(xs, zs, *tail)

    y_prompt = jnp.stack([xp[b * PROMPT_ROWS + WINDOW:(b + 1) * PROMPT_ROWS] for b in range(BATCH)])
    y_sample = xs.reshape(DEC_BATCH, DEC_SEQ, D_MODEL)
    st = lambda xs: jnp.stack(xs, axis=1)
    return (y_prompt, y_sample, st(p_wkv), st(p_shift), st(p_wk), st(p_wv), st(p_mk), st(p_mv),
            st(s_wkv), st(s_shift), st(s_wk), st(s_wv))
```

```python
import functools

import jax
import jax.numpy as jnp
from jax import lax
from jax.experimental import pallas as pl
from jax.experimental.pallas import tpu as pltpu

F32 = jnp.float32
BF16 = jnp.bfloat16

D_MODEL = 1024
BATCH = 2
SEQ = 8192
DEPTH = 4
DEC_BATCH = 128
DEC_SEQ = 8
PAST_LEN = 8192
N_META = 16
HEADS = 16
HEAD = 64
N_KV_HEADS = 4
GROUP = HEADS // N_KV_HEADS
KV_WIDTH = N_KV_HEADS * HEAD
WINDOW = 128
D_FF = 4 * D_MODEL
RMS_EPS = 1e-6
GN_EPS = 6.4e-4
NEG_INF = -1e30
LORA_PAD = 128

PAD_ROWS = WINDOW - N_META
PROMPT_ROWS = PAD_ROWS + N_META + SEQ
PROMPT_TOTAL = BATCH * PROMPT_ROWS
SAMPLE_TOTAL = DEC_BATCH * DEC_SEQ
TOTAL_ROWS = PROMPT_TOTAL + SAMPLE_TOTAL

CHUNK = 64
SUB = 16
SEQ_GROUP = CHUNK // DEC_SEQ
TM_PROMPT = 640
TM_SAMPLE = 512
TM_PREP_SAMPLE = 256
TM_PREP = 320
TF = 2048
VMEM_LIMIT = 56 * 1024 * 1024
SMALL_MM_PASSES = 1


def _cparams(sem):
    return pltpu.CompilerParams(dimension_semantics=sem, vmem_limit_bytes=VMEM_LIMIT)


def _rms(x, g):
    return x * lax.rsqrt(jnp.mean(x * x, axis=-1, keepdims=True) + RMS_EPS) * g


def _sigmoid(x):
    return 1.0 / (1.0 + jnp.exp(-x))


def _bdot(a, b):
    return jnp.dot(a.astype(BF16), b.astype(BF16), preferred_element_type=F32)


def _split(x):
    hi = x.astype(BF16)
    lo = (x - hi.astype(F32)).astype(BF16)
    return hi, lo


_NN = (((1,), (0,)), ((), ()))
_NT = (((1,), (1,)), ((), ()))
_TN = (((0,), (0,)), ((), ()))


def _mm(a, b, dims=_NN, passes=SMALL_MM_PASSES):
    dg = functools.partial(lax.dot_general, dimension_numbers=dims, preferred_element_type=F32)
    if passes == 1:
        return dg(a.astype(BF16), b.astype(BF16))
    ah, al = _split(a)
    bh, bl = _split(b)
    return dg(ah, bh) + (dg(ah, bl) + dg(al, bh))


def _div_pow2(x, n):
    shift = n.bit_length() - 1
    assert 1 << shift == n
    return jnp.right_shift(x, shift)


def _dot_exact_lhs(m01, x):
    hi, lo = _split(x)
    return (jnp.dot(m01, hi, preferred_element_type=F32)
            + jnp.dot(m01, lo, preferred_element_type=F32))


def _row_tile(rows):
    tm = TM_PROMPT if rows % TM_PROMPT == 0 else TM_SAMPLE
    assert rows % tm == 0
    return tm


def _mixer_mlp_body(x_ref, z_ref, wo_ref, g1_ref, g2_ref, g3_ref, wup_ref, wdn_ref, o_ref, x1_ref, xn_ref, acc_ref):
    f = pl.program_id(1)

    @pl.when(f == 0)
    def _():
        m = jnp.dot(z_ref[...], wo_ref[...], preferred_element_type=F32)
        x1 = x_ref[...] + _rms(m, g1_ref[...])
        x1_ref[...] = x1
        xn_ref[...] = _rms(x1, g2_ref[...]).astype(BF16)
        acc_ref[...] = jnp.zeros_like(acc_ref)

    h = jnp.dot(xn_ref[...], wup_ref[...], preferred_element_type=F32)
    a = jnp.maximum(h, 0.0)
    acc_ref[...] += jnp.dot((a * a).astype(BF16), wdn_ref[...], preferred_element_type=F32)

    @pl.when(f == pl.num_programs(1) - 1)
    def _():
        o_ref[...] = x1_ref[...] + _rms(acc_ref[...], g3_ref[...])


def _mixer_mlp(x, z, w_o, g1, g2, g3, wup, wdn):
    rows = x.shape[0]
    tm = _row_tile(rows)
    vec = pl.BlockSpec((1, D_MODEL), lambda i, f: (0, 0))
    return pl.pallas_call(
        _mixer_mlp_body,
        grid=(rows // tm, D_FF // TF),
        in_specs=[
            pl.BlockSpec((tm, D_MODEL), lambda i, f: (i, 0)),
            pl.BlockSpec((tm, D_MODEL), lambda i, f: (i, 0)),
            pl.BlockSpec((D_MODEL, D_MODEL), lambda i, f: (0, 0)),
            vec, vec, vec,
            pl.BlockSpec((D_MODEL, TF), lambda i, f: (0, f)),
            pl.BlockSpec((TF, D_MODEL), lambda i, f: (f, 0)),
        ],
        out_specs=pl.BlockSpec((tm, D_MODEL), lambda i, f: (i, 0)),
        out_shape=jax.ShapeDtypeStruct((rows, D_MODEL), F32),
        scratch_shapes=[pltpu.VMEM((tm, D_MODEL), F32), pltpu.VMEM((tm, D_MODEL), BF16),
                        pltpu.VMEM((tm, D_MODEL), F32)],
        compiler_params=_cparams(("parallel", "arbitrary")),
        name="mixer_mlp",
    )(x, z, w_o, g1, g2, g3, wup, wdn)


def _qkv_body(x_ref, g_ref, w_ref, q_ref, kv_ref):
    xn = _rms(x_ref[...], g_ref[...]).astype(BF16)
    res = jnp.dot(xn, w_ref[...], preferred_element_type=F32)
    q_ref[...] = (res[:, :D_MODEL] * (HEAD ** -0.5)).astype(BF16)
    kv_ref[...] = res[:, D_MODEL:]


def _qkv(x, g, w):
    rows = x.shape[0]
    tm = _row_tile(rows)
    return pl.pallas_call(
        _qkv_body,
        grid=(rows // tm,),
        in_specs=[
            pl.BlockSpec((tm, D_MODEL), lambda i: (i, 0)),
            pl.BlockSpec((1, D_MODEL), lambda i: (0, 0)),
            pl.BlockSpec((D_MODEL, D_MODEL + 2 * KV_WIDTH), lambda i: (0, 0)),
        ],
        out_specs=[
            pl.BlockSpec((tm, D_MODEL), lambda i: (i, 0)),
            pl.BlockSpec((tm, 2 * KV_WIDTH), lambda i: (i, 0)),
        ],
        out_shape=[
            jax.ShapeDtypeStruct((rows, D_MODEL), BF16),
            jax.ShapeDtypeStruct((rows, 2 * KV_WIDTH), F32),
        ],
        compiler_params=_cparams(("parallel",)),
        name="qkv",
    )(x, g, w)


def _alibi_slope(h):
    return 2.0 ** (-8.0 * (h + 1) / HEADS)


def _group_queries(q, kvh):
    return jnp.concatenate([q[:, h * HEAD:(h + 1) * HEAD] for h in range(kvh * GROUP, (kvh + 1) * GROUP)], axis=0)


def _group_columns(sink_ref, kvh, tq):
    g = _div_pow2(lax.broadcasted_iota(jnp.int32, (GROUP * tq, 1), 0), tq)
    slope = jnp.zeros((GROUP * tq, 1), F32)
    sink = jnp.zeros((GROUP * tq, 1), F32)
    for i in range(GROUP):
        h = kvh * GROUP + i
        slope = jnp.where(g == i, _alibi_slope(h), slope)
        sink = jnp.where(g == i, sink_ref[h], sink)
    return slope, sink


def _alibi_bias(ok, mind, slope):
    return jnp.where(ok, slope * mind, -NEG_INF)


def _attend_groups(qs, ks, vs, biases, sinks):
    scores = [lax.dot_general(q, k, _NT, preferred_element_type=F32) for q, k in zip(qs, ks)]
    probs = []
    for s, bias, sink in zip(scores, biases, sinks):
        logits = s - bias
        m = jnp.maximum(jnp.max(logits, axis=-1, keepdims=True), sink)
        p = jnp.exp(logits - m)
        den = jnp.sum(p, axis=-1, keepdims=True) + jnp.exp(sink - m)
        probs.append((p / den).astype(BF16))
    return [jnp.dot(p, v, preferred_element_type=F32) for p, v in zip(probs, vs)]


def _store_group(o_ref, rows, kvh, o, tq):
    for i in range(GROUP):
        h = kvh * GROUP + i
        o_ref[rows, h * HEAD:(h + 1) * HEAD] = o[i * tq:(i + 1) * tq].astype(o_ref.dtype)


_BLOCKS_PER_PROMPT = PROMPT_ROWS // WINDOW


def _attn_prompt_body(sink_ref, q_ref, kv0_ref, kvp_ref, kvc_ref, o_ref, bias_ref):
    i = pl.program_id(1)
    kvhs = range(N_KV_HEADS)

    @pl.when(i <= 2)
    def _():
        shp = (GROUP * WINDOW, 3 * WINDOW)
        r = jnp.bitwise_and(lax.broadcasted_iota(jnp.int32, shp, 0), WINDOW - 1)
        c = lax.broadcasted_iota(jnp.int32, shp, 1)
        kblk = _div_pow2(c, WINDOW)
        cc = c - kblk * WINDOW
        qpos = jnp.where(i == 0, r - PAD_ROWS, N_META + (i - 1) * WINDOW + r)
        kpos = jnp.where(kblk == 0, cc - PAD_ROWS, N_META + (i - 3 + kblk) * WINDOW + cc)
        kmeta = kblk == 0
        kvalid = jnp.where(kmeta, cc - PAD_ROWS, jnp.where(kblk == 1, i - 2, i - 1)) >= 0
        dist = qpos - kpos
        ok = kvalid & (dist >= 0) & (kmeta | (dist < WINDOW))
        mind = jnp.minimum(dist, WINDOW).astype(F32)
        for j in kvhs:
            bias_ref[j] = _alibi_bias(ok, mind, _group_columns(sink_ref, j, WINDOW)[0])

    kv = jnp.concatenate([kv0_ref[...], kvp_ref[...], kvc_ref[...]], axis=0).astype(BF16)
    q = q_ref[...]
    outs = _attend_groups(
        [_group_queries(q, j) for j in kvhs],
        [kv[:, j * HEAD:(j + 1) * HEAD] for j in kvhs],
        [kv[:, KV_WIDTH + j * HEAD:KV_WIDTH + (j + 1) * HEAD] for j in kvhs],
        [bias_ref[j] for j in kvhs],
        [_group_columns(sink_ref, j, WINDOW)[1] for j in kvhs])
    for j in kvhs:
        _store_group(o_ref, slice(None), j, outs[j], WINDOW)


def _attn_prompt(sinks, q, kv):
    nb = _BLOCKS_PER_PROMPT
    return pl.pallas_call(
        _attn_prompt_body,
        grid=(BATCH, nb),
        in_specs=[
            pl.BlockSpec(memory_space=pltpu.SMEM),
            pl.BlockSpec((WINDOW, D_MODEL), lambda b, i: (b * nb + i, 0)),
            pl.BlockSpec((WINDOW, 2 * KV_WIDTH), lambda b, i: (b * nb, 0)),
            pl.BlockSpec((WINDOW, 2 * KV_WIDTH), lambda b, i: (b * nb + jnp.maximum(i - 1, 0), 0)),
            pl.BlockSpec((WINDOW, 2 * KV_WIDTH), lambda b, i: (b * nb + i, 0)),
        ],
        out_specs=pl.BlockSpec((WINDOW, D_MODEL), lambda b, i: (b * nb + i, 0)),
        out_shape=jax.ShapeDtypeStruct((PROMPT_TOTAL, D_MODEL), BF16),
        scratch_shapes=[pltpu.VMEM((N_KV_HEADS, GROUP * WINDOW, 3 * WINDOW), F32)],
        compiler_params=_cparams(("arbitrary", "arbitrary")),
        name="attn_prompt",
    )(sinks, q, kv, kv, kv)


_ATT_SEQS = 8
_SAMPLE_KEYS = N_META + WINDOW + DEC_SEQ


def _attn_sample_body(sink_ref, q_ref, kv_ref, mk_ref, mv_ref, wk_ref, wv_ref, o_ref, nwk_ref, nwv_ref):
    shp = (GROUP * DEC_SEQ, _SAMPLE_KEYS)
    t = jnp.bitwise_and(lax.broadcasted_iota(jnp.int32, shp, 0), DEC_SEQ - 1)
    c = lax.broadcasted_iota(jnp.int32, shp, 1)
    qpos = PAST_LEN + t
    kpos = jnp.where(c < N_META, c,
                     jnp.where(c < N_META + WINDOW, PAST_LEN - WINDOW + (c - N_META),
                               PAST_LEN + (c - N_META - WINDOW)))
    kmeta = c < N_META
    kvalid = kmeta | (kpos >= N_META)
    dist = qpos - kpos
    ok = kvalid & (dist >= 0) & (kmeta | (dist < WINDOW))
    mind = jnp.minimum(dist, WINDOW).astype(F32)
    kvhs = range(N_KV_HEADS)
    cols = [_group_columns(sink_ref, j, DEC_SEQ) for j in kvhs]
    biases = [_alibi_bias(ok, mind, cols[j][0]) for j in kvhs]
    qs, ks, vs, bs, cs = [], [], [], [], []
    for s in range(_ATT_SEQS):
        rows = slice(s * DEC_SEQ, (s + 1) * DEC_SEQ)
        knew = kv_ref[rows, :KV_WIDTH]
        vnew = kv_ref[rows, KV_WIDTH:]
        wk = wk_ref[s]
        wv = wv_ref[s]
        nwk_ref[s] = jnp.concatenate([wk[DEC_SEQ:], knew], axis=0)
        nwv_ref[s] = jnp.concatenate([wv[DEC_SEQ:], vnew], axis=0)
        kall = jnp.concatenate([mk_ref[s], wk, knew], axis=0).astype(BF16)
        vall = jnp.concatenate([mv_ref[s], wv, vnew], axis=0).astype(BF16)
        q = q_ref[rows, :]
        for j in kvhs:
            qs.append(_group_queries(q, j))
            ks.append(kall[:, j * HEAD:(j + 1) * HEAD])
            vs.append(vall[:, j * HEAD:(j + 1) * HEAD])
            bs.append(biases[j])
            cs.append(cols[j][1])
    outs = _attend_groups(qs, ks, vs, bs, cs)
    for s in range(_ATT_SEQS):
        for j in kvhs:
            _store_group(o_ref, slice(s * DEC_SEQ, (s + 1) * DEC_SEQ), j, outs[s * N_KV_HEADS + j], DEC_SEQ)


def _attn_sample(sinks, q, kv, cmk, cmv, cwk, cwv, layer):
    rows = _ATT_SEQS * DEC_SEQ
    ng = DEC_BATCH // _ATT_SEQS
    cache_spec = lambda n: pl.BlockSpec((_ATT_SEQS, None, n, KV_WIDTH), lambda g: (g, layer, 0, 0))
    return pl.pallas_call(
        _attn_sample_body,
        grid=(ng,),
        in_specs=[
            pl.BlockSpec(memory_space=pltpu.SMEM),
            pl.BlockSpec((rows, D_MODEL), lambda g: (g, 0)),
            pl.BlockSpec((rows, 2 * KV_WIDTH), lambda g: (g, 0)),
            cache_spec(N_META), cache_spec(N_META), cache_spec(WINDOW), cache_spec(WINDOW),
        ],
        out_specs=[
            pl.BlockSpec((rows, D_MODEL), lambda g: (g, 0)),
            pl.BlockSpec((_ATT_SEQS, WINDOW, KV_WIDTH), lambda g: (g, 0, 0)),
            pl.BlockSpec((_ATT_SEQS, WINDOW, KV_WIDTH), lambda g: (g, 0, 0)),
        ],
        out_shape=[
            jax.ShapeDtypeStruct((SAMPLE_TOTAL, D_MODEL), BF16),
            jax.ShapeDtypeStruct((DEC_BATCH, WINDOW, KV_WIDTH), F32),
            jax.ShapeDtypeStruct((DEC_BATCH, WINDOW, KV_WIDTH), F32),
        ],
        compiler_params=_cparams(("parallel",)),
        name="attn_sample",
    )(sinks, q, kv, cmk, cmv, cwk, cwv)


_PREP_TILES_PER_PROMPT = PROMPT_ROWS // TM_PREP
_DECAY_SCALE = 0.6065306597126334


def _prep_body(*refs, prompt, vres):
    it = iter(refs)
    x_ref = next(it)
    side_ref = next(it)
    vf_ref = next(it) if vres else None
    g0_ref, mu_ref = next(it), next(it)
    wr_ref, wk_ref, wv_ref = next(it), next(it), next(it)
    w0_ref, w1_ref, w2_ref = next(it), next(it), next(it)
    a0_ref, a1_ref, a2_ref = next(it), next(it), next(it)
    if vres:
        v0_ref, v1_ref, v2_ref = next(it), next(it), next(it)
    g1_ref, g2_ref = next(it), next(it)
    r_ref, k_ref, v_ref, a_ref, lw_ref, g_ref, xn_ref = (next(it) for _ in range(7))

    tm = x_ref.shape[0]
    g0 = g0_ref[...]
    xn = _rms(x_ref[...], g0)
    row = lax.broadcasted_iota(jnp.int32, (tm, 1), 0)
    if prompt:
        ib = lax.rem(pl.program_id(0), _PREP_TILES_PER_PROMPT)
        xn = jnp.where(ib * tm + row < PAD_ROWS, 0.0, xn)
        before = _rms(side_ref[...], g0)[7:8, :]
        before = jnp.where(ib == 0, 0.0, before)
        x_prev = jnp.where(row == 0, before, pltpu.roll(xn, 1, 0))
        xn_ref[...] = xn[tm - 8:, :]
    else:
        x_prev = jnp.where(jnp.bitwise_and(row, DEC_SEQ - 1) == 0, side_ref[...], pltpu.roll(xn, 1, 0))
        xn_ref[...] = xn
    xx = x_prev - xn
    xr, xw, xk, xv, xa, xg = (xn + xx * mu_ref[c:c + 1, :] for c in range(6))

    r_ref[...] = _bdot(xr, wr_ref[...])
    k_ref[...] = _bdot(xk, wk_ref[...])
    v = _bdot(xv, wv_ref[...])
    wl = w0_ref[...] + _bdot(jnp.tanh(_bdot(xw, w1_ref[...])), w2_ref[...])
    lw_ref[...] = -_DECAY_SCALE * _sigmoid(wl)
    if vres:
        mix = _sigmoid(v0_ref[...] + _bdot(_bdot(xv, v1_ref[...]), v2_ref[...]))
        v = v + (vf_ref[...] - v) * mix
    v_ref[...] = v
    a_ref[...] = _sigmoid(a0_ref[...] + _bdot(_bdot(xa, a1_ref[...]), a2_ref[...]))
    g_ref[...] = _bdot(_sigmoid(_bdot(xg, g1_ref[...])), g2_ref[...])


def _prep(x, side, vfirst, p, prompt):
    vres = vfirst is not None
    if prompt:
        tm, nt, nrows = TM_PREP, PROMPT_TOTAL // TM_PREP, PROMPT_TOTAL
        side_spec = pl.BlockSpec((8, D_MODEL), lambda i: (jnp.maximum(i * (TM_PREP // 8) - 1, 0), 0))
        xn_spec = pl.BlockSpec((8, D_MODEL), lambda i: (i, 0))
        xn_shape = jax.ShapeDtypeStruct((nt * 8, D_MODEL), F32)
    else:
        tm, nt, nrows = TM_PREP_SAMPLE, SAMPLE_TOTAL // TM_PREP_SAMPLE, SAMPLE_TOTAL
        side_spec = pl.BlockSpec((tm, D_MODEL), lambda i: (i, 0))
        xn_spec = pl.BlockSpec((tm, D_MODEL), lambda i: (i, 0))
        xn_shape = jax.ShapeDtypeStruct((nrows, D_MODEL), F32)
    row_spec = pl.BlockSpec((tm, D_MODEL), lambda i: (i, 0))
    const = lambda a: pl.BlockSpec(a.shape, lambda i: (0,) * a.ndim)

    args = [x, side]
    specs = [row_spec, side_spec]
    if vres:
        args.append(vfirst)
        specs.append(row_spec)
    names = ["g0", "mu", "w_r", "w_k", "w_v", "w0", "w1", "w2", "a0", "a1", "a2"]
    if vres:
        names += ["v0", "v1", "v2"]
    names += ["g1", "g2"]
    for n in names:
        args.append(p[n])
        specs.append(const(p[n]))
    big = jax.ShapeDtypeStruct((nrows, D_MODEL), F32)
    return pl.pallas_call(
        functools.partial(_prep_body, prompt=prompt, vres=vres),
        grid=(nt,),
        in_specs=specs,
        out_specs=[row_spec] * 6 + [xn_spec],
        out_shape=[big] * 6 + [xn_shape],
        compiler_params=_cparams(("parallel",)),
        name="rwkv_prep_prompt" if prompt else "rwkv_prep_sample",
    )(*args)


def _eye(n):
    r = lax.broadcasted_iota(jnp.int32, (n, n), 0)
    c = lax.broadcasted_iota(jnp.int32, (n, n), 1)
    return (r == c).astype(F32)


def _mms(a_list, b_list, dims=_NN):
    return [_mm(a, b, dims) for a, b in zip(a_list, b_list)]


def _unit_lower_inverse(n_list, sub, nblocks):
    n = n_list[0].shape[0]
    eye = _eye(n)
    r = lax.broadcasted_iota(jnp.int32, (n, n), 0)
    c = lax.broadcasted_iota(jnp.int32, (n, n), 1)
    diag_blk = _div_pow2(r, sub) == _div_pow2(c, sub)
    ms = [jnp.where(diag_blk, -x, 0.0) for x in n_list]
    tds = [eye + m for m in ms]
    span = 1
    while 2 * span < sub:
        ms = _mms(ms, ms)
        tds = [td + p for td, p in zip(tds, _mms(tds, ms))]
        span *= 2
    if nblocks == 1:
        return tds
    xs = _mms(tds, [jnp.where(diag_blk, 0.0, x) for x in n_list])
    zs = [eye - x for x in xs]
    if nblocks > 2:
        assert nblocks <= 4
        zs = [z + p for z, p in zip(zs, _mms(zs, _mms(xs, xs)))]
    return _mms(zs, tds)


def _stacked_causal_mask(c, seq_len):
    row = lax.broadcasted_iota(jnp.int32, (2 * c, 2 * c), 0)
    t = jnp.bitwise_and(row, c - 1)
    j = jnp.bitwise_and(lax.broadcasted_iota(jnp.int32, (2 * c, 2 * c), 1), c - 1)
    same = _div_pow2(t, seq_len) == _div_pow2(j, seq_len)
    return same & (t - j >= jnp.where(row < c, 1, 0))


def _chunk_prologue(refs, seq_len, lead=None):
    r_ref, k_ref, v_ref, a_ref, lw_ref, g_ref = (x if lead is None else x.at[lead] for x in refs[:6])
    kk_ref, ka_ref = refs[6:]
    c = CHUNK
    row = lax.broadcasted_iota(jnp.int32, (c, c), 0)
    col = lax.broadcasted_iota(jnp.int32, (c, c), 1)
    same = _div_pow2(row, seq_len) == _div_pow2(col, seq_len)
    strict = same & (row > col)
    incl = same & (row >= col)
    lw = lw_ref[...]
    one01 = lambda m: jnp.where(m, 1.0, 0.0).astype(BF16)
    cum = _dot_exact_lhs(one01(incl), lw)
    cum_end = _dot_exact_lhs(one01(same), lw)
    k_raw = k_ref[...]
    a = a_ref[...]
    q = dict(
        mask2=_stacked_causal_mask(c, seq_len),
        r=r_ref[...], v=v_ref[...], a=a, g=g_ref[...],
        k=k_raw * (1.0 + (a - 1.0) * ka_ref[...]),
        kk_un=k_raw * kk_ref[...],
        e_cum=jnp.exp(cum), e_neg=jnp.exp(-cum), e_excl=jnp.exp(cum - lw),
        e_end=jnp.exp(cum_end - cum), p_end=jnp.exp(cum_end),
    )
    return q


def _head_terms(qs, sub, nblocks):
    hs = range(len(qs) * HEADS)
    col = lambda name: [q[name][:, h * HEAD:(h + 1) * HEAD] for q in qs for h in range(HEADS)]
    rh, kh, vh, ah, kk = col("r"), col("k"), col("v"), col("a"), col("kk_un")
    e_excl, e_cum, e_neg, e_end = col("e_excl"), col("e_cum"), col("e_neg"), col("e_end")
    kk = [x / jnp.maximum(jnp.sqrt(jnp.sum(x * x, axis=-1, keepdims=True)), 1e-12) for x in kk]
    bh = [kk[h] * ah[h] for h in hs]
    at = [kk[h] * e_excl[h] for h in hs]
    rt = [rh[h] * e_cum[h] for h in hs]
    bt = [bh[h] * e_neg[h] for h in hs]
    kt = [kh[h] * e_neg[h] for h in hs]
    bk_end = [jnp.concatenate([bh[h] * e_end[h], kh[h] * e_end[h]], axis=0) for h in hs]
    a_all = _mms([jnp.concatenate([at[h], rt[h]], axis=0) for h in hs],
                 [jnp.concatenate([bt[h], kt[h]], axis=0) for h in hs], _NT)
    a_all = [jnp.where(qs[0]["mask2"], x, 0.0) for x in a_all]
    t = _unit_lower_inverse([x[:CHUNK, :CHUNK] for x in a_all], sub, nblocks)
    w = _mms(t, at)
    ubar = [-x for x in _mms(t, _mms([x[:CHUNK, CHUNK:] for x in a_all], vh))]
    return dict(rh=rh, kh=kh, vh=vh, w_rt=[jnp.concatenate([w[h], rt[h]], axis=0) for h in hs], ubar=ubar,
                a_r=[x[CHUNK:, :] for x in a_all], bk_end=bk_end, p_end=col("p_end"), g=col("g"))


def _head_outputs(t, ys, rk_ref, lnw_ref, lnb_ref, z_refs):
    ns = range(len(ys))
    sls = [slice((n % HEADS) * HEAD, (n % HEADS + 1) * HEAD) for n in ns]
    bonus = [jnp.sum(t["rh"][n] * t["kh"][n] * rk_ref[:, sls[n]], axis=-1, keepdims=True) for n in ns]
    mean = [jnp.mean(y, axis=-1, keepdims=True) for y in ys]
    d = [ys[n] - mean[n] for n in ns]
    var = [jnp.mean(x * x, axis=-1, keepdims=True) for x in d]
    for n in ns:
        yn = d[n] * lax.rsqrt(var[n] + GN_EPS) * lnw_ref[:, sls[n]] + lnb_ref[:, sls[n]]
        z_refs[n][:, sls[n]] = ((yn + bonus[n] * t["vh"][n]) * t["g"][n]).astype(BF16)


QUAD = 4
QW = QUAD * HEAD
NQ = HEADS // QUAD
CHUNKS_PER_STEP = 2


def _bf(x):
    return x.astype(BF16)


def _dg(a, b, dims=_NN):
    return lax.dot_general(_bf(a), _bf(b), dims, preferred_element_type=F32)


def _seg_sum(x, e_seg):
    hi, lo = _split(x)
    both = jnp.dot(jnp.concatenate([hi, lo], axis=0), e_seg, preferred_element_type=F32)
    return both[:x.shape[0]] + both[x.shape[0]:]


def _chunk_prompt_body(r_ref, k_ref, v_ref, a_ref, lw_ref, g_ref, kk_ref, ka_ref, rk_ref, lnw_ref, lnb_ref,
                       z_ref, sout_ref, s_ref):
    ci = pl.program_id(1)

    @pl.when(ci == 0)
    def _():
        s_ref[...] = jnp.zeros_like(s_ref)

    rows = r_ref.shape[1]
    nsub = rows // CHUNK
    i32 = jnp.int32
    t64 = lax.broadcasted_iota(i32, (CHUNK, QW), 0)
    j64 = jnp.bitwise_and(lax.broadcasted_iota(i32, (CHUNK, QW), 1), HEAD - 1)
    strict = t64 > j64
    diag_blk = _div_pow2(t64, SUB) == _div_pow2(j64, SUB)
    eye_q = jnp.where(t64 == j64, 1.0, 0.0)
    t2 = lax.broadcasted_iota(i32, (CHUNK, 2 * QW), 0)
    j2 = jnp.bitwise_and(lax.broadcasted_iota(i32, (CHUNK, 2 * QW), 1), HEAD - 1)
    incl2 = t2 >= j2
    same_head = (_div_pow2(lax.broadcasted_iota(i32, (QW, QW), 0), HEAD)
                 == _div_pow2(lax.broadcasted_iota(i32, (QW, QW), 1), HEAD))
    bd_f32 = jnp.where(same_head, 1.0, 0.0)
    bd_bf = bd_f32.astype(BF16)

    lane = lax.broadcasted_iota(i32, (CHUNK, 2 * HEAD), 1)
    keep_lo = jnp.where(lane < HEAD, 1.0, 0.0).astype(BF16)
    keep_hi = jnp.where(lane >= HEAD, 1.0, 0.0).astype(BF16)
    zero_t = jnp.zeros((CHUNK, 2 * HEAD), BF16)

    def bd(x):
        xb = _bf(x)
        x0, x1 = xb[:, :2 * HEAD], xb[:, 2 * HEAD:]
        return jnp.concatenate([
            jnp.concatenate([x0 * keep_lo, zero_t], axis=1), jnp.concatenate([x0 * keep_hi, zero_t], axis=1),
            jnp.concatenate([zero_t, x1 * keep_lo], axis=1), jnp.concatenate([zero_t, x1 * keep_hi], axis=1),
        ], axis=0)

    rr = lax.broadcasted_iota(i32, (rows, rows), 0)
    cc = lax.broadcasted_iota(i32, (rows, rows), 1)
    same_chunk = _div_pow2(rr, CHUNK) == _div_pow2(cc, CHUNK)
    one01 = lambda m: jnp.where(m, 1.0, 0.0).astype(BF16)
    lw = lw_ref[0]
    cum = _dot_exact_lhs(one01(same_chunk & (rr >= cc)), lw)
    cum_end = _dot_exact_lhs(one01(same_chunk), lw)
    r, v, a, k_raw = r_ref[0], v_ref[0], a_ref[0], k_ref[0]
    k_mod = k_raw * (1.0 + (a - 1.0) * ka_ref[...])
    kk_un = k_raw * kk_ref[...]
    sq = kk_un * kk_un
    rkk = r * k_mod * rk_ref[...]
    qsl = [slice(q * QW, (q + 1) * QW) for q in range(NQ)]
    sums = [_seg_sum(jnp.concatenate([sq[:, s], rkk[:, s]], axis=0), bd_bf) for s in qsl]
    ss = jnp.concatenate([x[:rows] for x in sums], axis=1)
    bonus = jnp.concatenate([x[rows:] for x in sums], axis=1) * v
    kk = kk_un / jnp.maximum(jnp.sqrt(ss), 1e-12)
    b = kk * a
    e_neg = jnp.exp(-cum)
    e_end = jnp.exp(cum_end - cum)
    at = kk * jnp.exp(cum - lw)
    rt = r * jnp.exp(cum)
    bt = b * e_neg
    kt = k_mod * e_neg
    b_end = b * e_end
    k_end = k_mod * e_end
    p_end = jnp.exp(cum_end)

    probs = [(g, q) for g in range(nsub) for q in range(NQ)]
    blk = lambda x, g, q: x[g * CHUNK:(g + 1) * CHUNK, qsl[q]]
    np_ = range(len(probs))

    a_all = [_dg(jnp.concatenate([blk(at, g, q), blk(rt, g, q)], axis=0),
                 jnp.concatenate([bd(blk(bt, g, q)), bd(blk(kt, g, q))], axis=0), _NT) for g, q in probs]
    n_ab = [jnp.where(strict, x[:CHUNK, :QW], 0.0) for x in a_all]
    a_ak = [jnp.where(strict, x[:CHUNK, QW:], 0.0) for x in a_all]
    a_r = [jnp.where(incl2, x[CHUNK:, :], 0.0) for x in a_all]
    bd_v = [bd(blk(v, g, q)) for g, q in probs]
    akv = [_dg(a_ak[i], bd_v[i]) for i in np_]
    m = [jnp.where(diag_blk, -x, 0.0) for x in n_ab]
    td = [eye_q + x for x in m]
    m = [_dg(x, bd(x)) for x in m]
    for _ in range(2):
        both = [_dg(jnp.concatenate([m[i], td[i]], axis=0), bd(m[i])) for i in np_]
        m = [x[:CHUNK] for x in both]
        td = [td[i] + both[i][CHUNK:] for i in np_]
    td = [td[i] + _dg(td[i], bd(m[i])) for i in np_]
    xo = [_dg(td[i], bd(jnp.where(diag_blk, 0.0, n_ab[i]))) for i in np_]
    x2 = [_dg(x, bd(x)) for x in xo]
    zz = [eye_q - x for x in xo]
    zz = [zz[i] + _dg(zz[i], bd(x2[i])) for i in np_]
    t_inv = [_dg(zz[i], bd(td[i])) for i in np_]
    wu = [_dg(t_inv[i], jnp.concatenate([bd(blk(at, *probs[i])), bd(akv[i])], axis=1)) for i in np_]

    y_rows = []
    for g in range(nsub):
        ids = [g * NQ + q for q in range(NQ)]
        s0 = [s_ref[q] for q in range(NQ)]
        x = [_dg(jnp.concatenate([wu[i][:, :QW], blk(rt, g, q)], axis=0), s0[q], _NT)
             for q, i in enumerate(ids)]
        u = [-wu[i][:, QW:] - x[q][:CHUNK] for q, i in enumerate(ids)]
        y = [x[q][CHUNK:] + _dg(a_r[i], jnp.concatenate([bd(u[q]), bd_v[i]], axis=0))
             for q, i in enumerate(ids)]
        ds = [_dg(jnp.concatenate([u[q], blk(v, g, q)], axis=0),
                  jnp.concatenate([blk(b_end, g, q), blk(k_end, g, q)], axis=0), _TN) for q in range(NQ)]
        for q in range(NQ):
            s_ref[q] = s0[q] * p_end[g * CHUNK:g * CHUNK + 1, qsl[q]] + ds[q] * bd_f32
        y_rows.append(jnp.concatenate(y, axis=1))
    y = jnp.concatenate(y_rows, axis=0)

    inv_n = 1.0 / HEAD
    mean = jnp.concatenate([_seg_sum(y[:, s], bd_bf) for s in qsl], axis=1) * inv_n
    d = y - mean
    var = jnp.concatenate([_seg_sum((d * d)[:, s], bd_bf) for s in qsl], axis=1) * inv_n
    yn = d * lax.rsqrt(var + GN_EPS) * lnw_ref[...] + lnb_ref[...]
    z_ref[0] = ((yn + bonus) * g_ref[0]).astype(BF16)

    @pl.when(ci == pl.num_programs(1) - 1)
    def _():
        for h in range(HEADS):
            o = (h % QUAD) * HEAD
            sout_ref[0, h] = s_ref[h // QUAD, o:o + HEAD, o:o + HEAD]


def _chunk_prompt(proj, p):
    rows = CHUNKS_PER_STEP * CHUNK
    nc = PROMPT_ROWS // rows
    row_spec = pl.BlockSpec((1, rows, D_MODEL), lambda b, c: (b, c, 0))
    vec_spec = pl.BlockSpec((1, D_MODEL), lambda b, c: (0, 0))
    z, st = pl.pallas_call(
        _chunk_prompt_body,
        grid=(BATCH, nc),
        in_specs=[row_spec] * 6 + [vec_spec] * 5,
        out_specs=[
            row_spec,
            pl.BlockSpec((1, HEADS, HEAD, HEAD), lambda b, c: (b, 0, 0, 0)),
        ],
        out_shape=[
            jax.ShapeDtypeStruct((BATCH, PROMPT_ROWS, D_MODEL), BF16),
            jax.ShapeDtypeStruct((BATCH, HEADS, HEAD, HEAD), F32),
        ],
        scratch_shapes=[pltpu.VMEM((NQ, QW, QW), F32)],
        compiler_params=_cparams(("parallel", "arbitrary")),
        name="rwkv_chunk_prompt",
    )(*[x.reshape(BATCH, PROMPT_ROWS, D_MODEL) for x in proj],
      p["k_k"], p["k_a"], p["r_k"], p["lnx_w"], p["lnx_b"])
    return z.reshape(PROMPT_TOTAL, D_MODEL), st


def _chunk_sample_body(r_ref, k_ref, v_ref, a_ref, lw_ref, g_ref, kk_ref, ka_ref, rk_ref, lnw_ref, lnb_ref,
                       sin_ref, states_ref, z_ref, sout_ref):
    del states_ref
    q = _chunk_prologue((r_ref, k_ref, v_ref, a_ref, lw_ref, g_ref, kk_ref, ka_ref), DEC_SEQ)
    seq_of_row2 = _div_pow2(jnp.bitwise_and(lax.broadcasted_iota(jnp.int32, (2 * CHUNK, 1), 0), CHUNK - 1),
                            DEC_SEQ)
    hs = range(HEADS)
    seqs = range(SEQ_GROUP)
    rows = [slice(s * DEC_SEQ, (s + 1) * DEC_SEQ) for s in seqs]
    t = _head_terms([q], DEC_SEQ, 1)
    rows2 = [slice(CHUNK + s * DEC_SEQ, CHUNK + (s + 1) * DEC_SEQ) for s in seqs]
    ws = [jnp.concatenate([_mm(t["w_rt"][h][rows[s]], sin_ref[s, h], _NT) for s in seqs], axis=0) for h in hs]
    rs = [jnp.concatenate([_mm(t["w_rt"][h][rows2[s]], sin_ref[s, h], _NT) for s in seqs], axis=0) for h in hs]
    uv = [jnp.concatenate([t["ubar"][h] - ws[h], t["vh"][h]], axis=0) for h in hs]
    y = _mms(t["a_r"], uv)
    for h in hs:
        for s in seqs:
            mine = jnp.where(seq_of_row2 == s, uv[h], 0.0)
            p_end = t["p_end"][h][s * DEC_SEQ:s * DEC_SEQ + 1, :]
            sout_ref[s, h] = sin_ref[s, h] * p_end + _mm(mine, t["bk_end"][h], _TN)
    _head_outputs(t, [rs[h] + y[h] for h in hs], rk_ref, lnw_ref, lnb_ref, [z_ref] * HEADS)


def _chunk_sample(proj, p, state, new_states, layer):
    ng = DEC_BATCH // SEQ_GROUP
    row_spec = pl.BlockSpec((CHUNK, D_MODEL), lambda g: (g, 0))
    vec_spec = pl.BlockSpec((1, D_MODEL), lambda g: (0, 0))
    n_in = 6 + 5 + 2
    return pl.pallas_call(
        _chunk_sample_body,
        grid=(ng,),
        in_specs=[row_spec] * 6 + [vec_spec] * 5 + [
            pl.BlockSpec((SEQ_GROUP, HEADS, HEAD, HEAD), lambda g: (g, 0, 0, 0)),
            pl.BlockSpec(memory_space=pl.ANY),
        ],
        out_specs=[
            row_spec,
            pl.BlockSpec((SEQ_GROUP, None, HEADS, HEAD, HEAD), lambda g: (g, layer, 0, 0, 0)),
        ],
        out_shape=[
            jax.ShapeDtypeStruct((SAMPLE_TOTAL, D_MODEL), BF16),
            jax.ShapeDtypeStruct(new_states.shape, F32),
        ],
        input_output_aliases={n_in - 1: 1},
        compiler_params=_cparams(("parallel",)),
        name="rwkv_chunk_sample",
    )(*proj, p["k_k"], p["k_a"], p["r_k"], p["lnx_w"], p["lnx_b"], state, new_states)


def _pad_cols(w):
    return jnp.pad(w, ((0, 0), (0, LORA_PAD - w.shape[1]))).astype(BF16)


def _pad_rows(w):
    return jnp.pad(w, ((0, LORA_PAD - w.shape[0]), (0, 0))).astype(BF16)


def kernel(x_prompt, x_sample, state_wkv, state_shift, cache_win_k, cache_win_v, cache_meta_k,
           cache_meta_v, meta_tokens, norm_gains, rwkv_mu, rwkv_w_r, rwkv_w_k, rwkv_w_v, rwkv_w_o,
           rwkv_w0, rwkv_w1, rwkv_w2, rwkv_a0, rwkv_a1, rwkv_a2, rwkv_v0, rwkv_v1, rwkv_v2,
           rwkv_g1, rwkv_g2, rwkv_k_k, rwkv_k_a, rwkv_r_k, rwkv_lnx_w, rwkv_lnx_b,
           attn_w_qkv, attn_w_o, attn_sinks, mlp_w_up, mlp_w_down):
    assert x_prompt.shape == (BATCH, SEQ, D_MODEL) and x_sample.shape == (DEC_BATCH, DEC_SEQ, D_MODEL)
    n_swa = cache_win_k.shape[1]
    row = lambda v: v.reshape(1, D_MODEL).astype(F32)

    head = jnp.concatenate([jnp.zeros((PAD_ROWS, D_MODEL), F32), meta_tokens.astype(F32)], axis=0)
    xp = jnp.concatenate([jnp.broadcast_to(head[None], (BATCH, WINDOW, D_MODEL)), x_prompt], axis=1)
    xp = xp.reshape(PROMPT_TOTAL, D_MODEL)
    xs = x_sample.reshape(SAMPLE_TOTAL, D_MODEL).astype(F32)

    cmk = cache_meta_k.reshape(DEC_BATCH, n_swa, N_META, KV_WIDTH)
    cmv = cache_meta_v.reshape(DEC_BATCH, n_swa, N_META, KV_WIDTH)
    cwk = cache_win_k.reshape(DEC_BATCH, n_swa, WINDOW, KV_WIDTH)
    cwv = cache_win_v.reshape(DEC_BATCH, n_swa, WINDOW, KV_WIDTH)

    p_wkv, p_shift, s_shift = [], [], []
    s_wkv = jnp.zeros(state_wkv.shape, F32)
    p_wk, p_wv, p_mk, p_mv, s_wk, s_wv = [], [], [], [], [], []
    vfirst_p = vfirst_s = None
    for i in range(DEPTH):
        gains = norm_gains[i]
        j = i // 2
        if i % 2 == 0:
            p = dict(
                g0=row(gains[0]), mu=rwkv_mu[j],
                w_r=rwkv_w_r[j].astype(BF16), w_k=rwkv_w_k[j].astype(BF16), w_v=rwkv_w_v[j].astype(BF16),
                w0=row(rwkv_w0[j]), w1=_pad_cols(rwkv_w1[j]), w2=_pad_rows(rwkv_w2[j]),
                a0=row(rwkv_a0[j]), a1=_pad_cols(rwkv_a1[j]), a2=_pad_rows(rwkv_a2[j]),
                g1=_pad_cols(rwkv_g1[j]), g2=_pad_rows(rwkv_g2[j]),
                k_k=row(rwkv_k_k[j]), k_a=row(rwkv_k_a[j]), r_k=row(rwkv_r_k[j]),
                lnx_w=row(rwkv_lnx_w[j]), lnx_b=row(rwkv_lnx_b[j]),
            )
            if j > 0:
                p.update(v0=row(rwkv_v0[j - 1]), v1=_pad_cols(rwkv_v1[j - 1]), v2=_pad_rows(rwkv_v2[j - 1]))
            shift_rows = jnp.zeros((DEC_BATCH, DEC_SEQ, D_MODEL), F32).at[:, 0].set(state_shift[:, j])
            outs_p = _prep(xp, xp, vfirst_p if j > 0 else None, p, prompt=True)
            outs_s = _prep(xs, shift_rows.reshape(SAMPLE_TOTAL, D_MODEL), vfirst_s if j > 0 else None, p,
                           prompt=False)
            if j == 0:
                vfirst_p, vfirst_s = outs_p[2], outs_s[2]
            zp, st_p = _chunk_prompt(outs_p[:6], p)
            zs, s_wkv = _chunk_sample(outs_s[:6], p, state_wkv[:, j], s_wkv, j)
            xn_tail = outs_p[6].reshape(BATCH, _PREP_TILES_PER_PROMPT, 8, D_MODEL)
            p_wkv.append(st_p)
            p_shift.append(xn_tail[:, -1, -1])
            s_shift.append(outs_s[6].reshape(DEC_BATCH, DEC_SEQ, D_MODEL)[:, -1])
            w_o = rwkv_w_o[j].astype(BF16)
        else:
            w_qkv = attn_w_qkv[j].astype(BF16)
            q, kv = _qkv(xp, row(gains[0]), w_qkv)
            q_s, kv_s = _qkv(xs, row(gains[0]), w_qkv)
            sinks = attn_sinks[j].astype(F32)
            zp = _attn_prompt(sinks, q, kv)
            zs, nwk, nwv = _attn_sample(sinks, q_s, kv_s, cmk, cmv, cwk, cwv, j)
            def kv_rows(lo, hi, which):
                parts = [kv[b * PROMPT_ROWS + lo:b * PROMPT_ROWS + hi, which * KV_WIDTH:(which + 1) * KV_WIDTH]
                         for b in range(BATCH)]
                return jnp.stack(parts).reshape(BATCH, hi - lo, N_KV_HEADS, HEAD)

            p_mk.append(kv_rows(PAD_ROWS, WINDOW, 0))
            p_mv.append(kv_rows(PAD_ROWS, WINDOW, 1))
            p_wk.append(kv_rows(PROMPT_ROWS - WINDOW, PROMPT_ROWS, 0))
            p_wv.append(kv_rows(PROMPT_ROWS - WINDOW, PROMPT_ROWS, 1))
            s_wk.append(nwk.reshape(DEC_BATCH, WINDOW, N_KV_HEADS, HEAD))
            s_wv.append(nwv.reshape(DEC_BATCH, WINDOW, N_KV_HEADS, HEAD))
            w_o = attn_w_o[j].astype(BF16)
        tail = (w_o, row(gains[1]), row(gains[2]), row(gains[3]), mlp_w_up[i].astype(BF16),
                mlp_w_down[i].astype(BF16))
        xp = _mixer_mlp(xp, zp, *tail)
        xs = _mixer_mlp(xs, zs, *tail)

    y_prompt = xp.reshape(BATCH, PROMPT_ROWS, D_MODEL)[:, WINDOW:]
    y_sample = xs.reshape(DEC_BATCH, DEC_SEQ, D_MODEL)
    st = lambda xs: jnp.stack(xs, axis=1)
    return (y_prompt, y_sample, st(p_wkv), st(p_shift), st(p_wk), st(p_wv), st(p_mk), st(p_mv),
            s_wkv, st(s_shift), st(s_wk), st(s_wv))
```

```python
import functools

import jax
import jax.numpy as jnp
from jax import lax
from jax.experimental import pallas as pl
from jax.experimental.pallas import tpu as pltpu

F32 = jnp.float32
BF16 = jnp.bfloat16

D_MODEL = 1024
BATCH = 2
SEQ = 8192
DEPTH = 4
DEC_BATCH = 128
DEC_SEQ = 8
PAST_LEN = 8192
N_META = 16
HEADS = 16
HEAD = 64
N_KV_HEADS = 4
GROUP = HEADS // N_KV_HEADS
KV_WIDTH = N_KV_HEADS * HEAD
WINDOW = 128
D_FF = 4 * D_MODEL
RMS_EPS = 1e-6
GN_EPS = 6.4e-4
NEG_INF = -1e30
LORA_PAD = 128

PAD_ROWS = WINDOW - N_META
PROMPT_ROWS = PAD_ROWS + N_META + SEQ
PROMPT_TOTAL = BATCH * PROMPT_ROWS
SAMPLE_TOTAL = DEC_BATCH * DEC_SEQ
TOTAL_ROWS = PROMPT_TOTAL + SAMPLE_TOTAL

CHUNK = 64
SUB = 16
SEQ_GROUP = CHUNK // DEC_SEQ
TM_PROMPT = 640
TM_SAMPLE = 512
TM_PREP_SAMPLE = 256
TM_PREP = 320
TF = 2048
VMEM_LIMIT = 56 * 1024 * 1024
SMALL_MM_PASSES = 1


def _cparams(sem):
    return pltpu.CompilerParams(dimension_semantics=sem, vmem_limit_bytes=VMEM_LIMIT)


def _rms(x, g):
    return x * lax.rsqrt(jnp.mean(x * x, axis=-1, keepdims=True) + RMS_EPS) * g


def _sigmoid(x):
    return 1.0 / (1.0 + jnp.exp(-x))


def _bdot(a, b):
    return jnp.dot(a.astype(BF16), b.astype(BF16), preferred_element_type=F32)


def _split(x):
    hi = x.astype(BF16)
    lo = (x - hi.astype(F32)).astype(BF16)
    return hi, lo


_NN = (((1,), (0,)), ((), ()))
_NT = (((1,), (1,)), ((), ()))
_TN = (((0,), (0,)), ((), ()))


def _mm(a, b, dims=_NN, passes=SMALL_MM_PASSES):
    dg = functools.partial(lax.dot_general, dimension_numbers=dims, preferred_element_type=F32)
    if passes == 1:
        return dg(a.astype(BF16), b.astype(BF16))
    ah, al = _split(a)
    bh, bl = _split(b)
    return dg(ah, bh) + (dg(ah, bl) + dg(al, bh))


def _div_pow2(x, n):
    shift = n.bit_length() - 1
    assert 1 << shift == n
    return jnp.right_shift(x, shift)


def _dot_exact_lhs(m01, x):
    hi, lo = _split(x)
    return (jnp.dot(m01, hi, preferred_element_type=F32)
            + jnp.dot(m01, lo, preferred_element_type=F32))


def _row_tile(rows):
    tm = TM_PROMPT if rows % TM_PROMPT == 0 else TM_SAMPLE
    assert rows % tm == 0
    return tm


def _mixer_mlp_body(x_ref, z_ref, wo_ref, g1_ref, g2_ref, g3_ref, wup_ref, wdn_ref, o_ref, x1_ref, xn_ref, acc_ref):
    f = pl.program_id(1)

    @pl.when(f == 0)
    def _():
        m = jnp.dot(z_ref[...], wo_ref[...], preferred_element_type=F32)
        x1 = x_ref[...] + _rms(m, g1_ref[...])
        x1_ref[...] = x1
        xn_ref[...] = _rms(x1, g2_ref[...]).astype(BF16)
        acc_ref[...] = jnp.zeros_like(acc_ref)

    h = jnp.dot(xn_ref[...], wup_ref[...], preferred_element_type=F32)
    a = jnp.maximum(h, 0.0)
    acc_ref[...] += jnp.dot((a * a).astype(BF16), wdn_ref[...], preferred_element_type=F32)

    @pl.when(f == pl.num_programs(1) - 1)
    def _():
        o_ref[...] = x1_ref[...] + _rms(acc_ref[...], g3_ref[...])


def _mixer_mlp(x, z, w_o, g1, g2, g3, wup, wdn):
    rows = x.shape[0]
    tm = _row_tile(rows)
    vec = pl.BlockSpec((1, D_MODEL), lambda i, f: (0, 0))
    return pl.pallas_call(
        _mixer_mlp_body,
        grid=(rows // tm, D_FF // TF),
        in_specs=[
            pl.BlockSpec((tm, D_MODEL), lambda i, f: (i, 0)),
            pl.BlockSpec((tm, D_MODEL), lambda i, f: (i, 0)),
            pl.BlockSpec((D_MODEL, D_MODEL), lambda i, f: (0, 0)),
            vec, vec, vec,
            pl.BlockSpec((D_MODEL, TF), lambda i, f: (0, f)),
            pl.BlockSpec((TF, D_MODEL), lambda i, f: (f, 0)),
        ],
        out_specs=pl.BlockSpec((tm, D_MODEL), lambda i, f: (i, 0)),
        out_shape=jax.ShapeDtypeStruct((rows, D_MODEL), F32),
        scratch_shapes=[pltpu.VMEM((tm, D_MODEL), F32), pltpu.VMEM((tm, D_MODEL), BF16),
                        pltpu.VMEM((tm, D_MODEL), F32)],
        compiler_params=_cparams(("parallel", "arbitrary")),
        name="mixer_mlp",
    )(x, z, w_o, g1, g2, g3, wup, wdn)


def _qkv_body(x_ref, g_ref, w_ref, q_ref, kv_ref):
    xn = _rms(x_ref[...], g_ref[...]).astype(BF16)
    res = jnp.dot(xn, w_ref[...], preferred_element_type=F32)
    q_ref[...] = (res[:, :D_MODEL] * (HEAD ** -0.5)).astype(BF16)
    kv_ref[...] = res[:, D_MODEL:]


def _qkv(x, g, w):
    rows = x.shape[0]
    tm = _row_tile(rows)
    return pl.pallas_call(
        _qkv_body,
        grid=(rows // tm,),
        in_specs=[
            pl.BlockSpec((tm, D_MODEL), lambda i: (i, 0)),
            pl.BlockSpec((1, D_MODEL), lambda i: (0, 0)),
            pl.BlockSpec((D_MODEL, D_MODEL + 2 * KV_WIDTH), lambda i: (0, 0)),
        ],
        out_specs=[
            pl.BlockSpec((tm, D_MODEL), lambda i: (i, 0)),
            pl.BlockSpec((tm, 2 * KV_WIDTH), lambda i: (i, 0)),
        ],
        out_shape=[
            jax.ShapeDtypeStruct((rows, D_MODEL), BF16),
            jax.ShapeDtypeStruct((rows, 2 * KV_WIDTH), F32),
        ],
        compiler_params=_cparams(("parallel",)),
        name="qkv",
    )(x, g, w)


def _alibi_slope(h):
    return 2.0 ** (-8.0 * (h + 1) / HEADS)


def _group_queries(q, kvh):
    return jnp.concatenate([q[:, h * HEAD:(h + 1) * HEAD] for h in range(kvh * GROUP, (kvh + 1) * GROUP)], axis=0)


def _group_slopes(kvh, tq):
    g = _div_pow2(lax.broadcasted_iota(jnp.int32, (GROUP * tq, 1), 0), tq)
    slope = jnp.zeros((GROUP * tq, 1), F32)
    for i in range(GROUP):
        slope = jnp.where(g == i, _alibi_slope(kvh * GROUP + i), slope)
    return slope


def _group_sinks(sink_ref, kvh):
    return [sink_ref[kvh * GROUP + i] for i in range(GROUP)]


def _alibi_bias(ok, mind, slope):
    return jnp.where(ok, slope * mind, -NEG_INF)


def _attend_groups(qs, ks, vs, biases, sinks):
    scores = [lax.dot_general(q, k, _NT, preferred_element_type=F32) for q, k in zip(qs, ks)]
    probs = []
    for s, bias, sink in zip(scores, biases, sinks):
        logits = s - bias
        tq = logits.shape[0] // GROUP
        parts = []
        for i in range(GROUP):
            lg = logits[i * tq:(i + 1) * tq]
            m = jnp.maximum(jnp.max(lg, axis=-1, keepdims=True), sink[i])
            p = jnp.exp(lg - m)
            den = jnp.sum(p, axis=-1, keepdims=True) + jnp.exp(sink[i] - m)
            parts.append(p / den)
        probs.append(jnp.concatenate(parts, axis=0).astype(BF16))
    return [jnp.dot(p, v, preferred_element_type=F32) for p, v in zip(probs, vs)]


def _store_group(o_ref, rows, kvh, o, tq):
    for i in range(GROUP):
        h = kvh * GROUP + i
        o_ref[rows, h * HEAD:(h + 1) * HEAD] = o[i * tq:(i + 1) * tq].astype(o_ref.dtype)


_BLOCKS_PER_PROMPT = PROMPT_ROWS // WINDOW


def _attn_prompt_body(sink_ref, q_ref, kv0_ref, kvp_ref, kvc_ref, o_ref, bias_ref):
    i = pl.program_id(1)
    kvhs = range(N_KV_HEADS)

    @pl.when(i <= 2)
    def _():
        shp = (GROUP * WINDOW, 3 * WINDOW)
        r = jnp.bitwise_and(lax.broadcasted_iota(jnp.int32, shp, 0), WINDOW - 1)
        c = lax.broadcasted_iota(jnp.int32, shp, 1)
        kblk = _div_pow2(c, WINDOW)
        cc = c - kblk * WINDOW
        qpos = jnp.where(i == 0, r - PAD_ROWS, N_META + (i - 1) * WINDOW + r)
        kpos = jnp.where(kblk == 0, cc - PAD_ROWS, N_META + (i - 3 + kblk) * WINDOW + cc)
        kmeta = kblk == 0
        kvalid = jnp.where(kmeta, cc - PAD_ROWS, jnp.where(kblk == 1, i - 2, i - 1)) >= 0
        dist = qpos - kpos
        ok = kvalid & (dist >= 0) & (kmeta | (dist < WINDOW))
        mind = jnp.minimum(dist, WINDOW).astype(F32)
        for j in kvhs:
            bias_ref[j] = _alibi_bias(ok, mind, _group_slopes(j, WINDOW))

    kv = jnp.concatenate([kv0_ref[...], kvp_ref[...], kvc_ref[...]], axis=0).astype(BF16)
    q = q_ref[...]
    outs = _attend_groups(
        [_group_queries(q, j) for j in kvhs],
        [kv[:, j * HEAD:(j + 1) * HEAD] for j in kvhs],
        [kv[:, KV_WIDTH + j * HEAD:KV_WIDTH + (j + 1) * HEAD] for j in kvhs],
        [bias_ref[j] for j in kvhs],
        [_group_sinks(sink_ref, j) for j in kvhs])
    for j in kvhs:
        _store_group(o_ref, slice(None), j, outs[j], WINDOW)


def _attn_prompt(sinks, q, kv):
    nb = _BLOCKS_PER_PROMPT
    return pl.pallas_call(
        _attn_prompt_body,
        grid=(BATCH, nb),
        in_specs=[
            pl.BlockSpec(memory_space=pltpu.SMEM),
            pl.BlockSpec((WINDOW, D_MODEL), lambda b, i: (b * nb + i, 0)),
            pl.BlockSpec((WINDOW, 2 * KV_WIDTH), lambda b, i: (b * nb, 0)),
            pl.BlockSpec((WINDOW, 2 * KV_WIDTH), lambda b, i: (b * nb + jnp.maximum(i - 1, 0), 0)),
            pl.BlockSpec((WINDOW, 2 * KV_WIDTH), lambda b, i: (b * nb + i, 0)),
        ],
        out_specs=pl.BlockSpec((WINDOW, D_MODEL), lambda b, i: (b * nb + i, 0)),
        out_shape=jax.ShapeDtypeStruct((PROMPT_TOTAL, D_MODEL), BF16),
        scratch_shapes=[pltpu.VMEM((N_KV_HEADS, GROUP * WINDOW, 3 * WINDOW), F32)],
        compiler_params=_cparams(("arbitrary", "arbitrary")),
        name="attn_prompt",
    )(sinks, q, kv, kv, kv)


_ATT_SEQS = 8
_SAMPLE_KEYS = N_META + WINDOW + DEC_SEQ


def _attn_sample_body(sink_ref, q_ref, kv_ref, mk_ref, mv_ref, wk_ref, wv_ref, o_ref, nwk_ref, nwv_ref):
    shp = (GROUP * DEC_SEQ, _SAMPLE_KEYS)
    t = jnp.bitwise_and(lax.broadcasted_iota(jnp.int32, shp, 0), DEC_SEQ - 1)
    c = lax.broadcasted_iota(jnp.int32, shp, 1)
    qpos = PAST_LEN + t
    kpos = jnp.where(c < N_META, c,
                     jnp.where(c < N_META + WINDOW, PAST_LEN - WINDOW + (c - N_META),
                               PAST_LEN + (c - N_META - WINDOW)))
    kmeta = c < N_META
    kvalid = kmeta | (kpos >= N_META)
    dist = qpos - kpos
    ok = kvalid & (dist >= 0) & (kmeta | (dist < WINDOW))
    mind = jnp.minimum(dist, WINDOW).astype(F32)
    kvhs = range(N_KV_HEADS)
    biases = [_alibi_bias(ok, mind, _group_slopes(j, DEC_SEQ)) for j in kvhs]
    sinks = [_group_sinks(sink_ref, j) for j in kvhs]
    qs, ks, vs, bs, cs = [], [], [], [], []
    for s in range(_ATT_SEQS):
        rows = slice(s * DEC_SEQ, (s + 1) * DEC_SEQ)
        knew = kv_ref[rows, :KV_WIDTH]
        vnew = kv_ref[rows, KV_WIDTH:]
        wk = wk_ref[s]
        wv = wv_ref[s]
        nwk_ref[s] = jnp.concatenate([wk[DEC_SEQ:], knew], axis=0)
        nwv_ref[s] = jnp.concatenate([wv[DEC_SEQ:], vnew], axis=0)
        kall = jnp.concatenate([mk_ref[s], wk, knew], axis=0).astype(BF16)
        vall = jnp.concatenate([mv_ref[s], wv, vnew], axis=0).astype(BF16)
        q = q_ref[rows, :]
        for j in kvhs:
            qs.append(_group_queries(q, j))
            ks.append(kall[:, j * HEAD:(j + 1) * HEAD])
            vs.append(vall[:, j * HEAD:(j + 1) * HEAD])
            bs.append(biases[j])
            cs.append(sinks[j])
    outs = _attend_groups(qs, ks, vs, bs, cs)
    for s in range(_ATT_SEQS):
        for j in kvhs:
            _store_group(o_ref, slice(s * DEC_SEQ, (s + 1) * DEC_SEQ), j, outs[s * N_KV_HEADS + j], DEC_SEQ)


def _attn_sample(sinks, q, kv, cmk, cmv, cwk, cwv, layer):
    rows = _ATT_SEQS * DEC_SEQ
    ng = DEC_BATCH // _ATT_SEQS
    cache_spec = lambda n: pl.BlockSpec((_ATT_SEQS, None, n, KV_WIDTH), lambda g: (g, layer, 0, 0))
    return pl.pallas_call(
        _attn_sample_body,
        grid=(ng,),
        in_specs=[
            pl.BlockSpec(memory_space=pltpu.SMEM),
            pl.BlockSpec((rows, D_MODEL), lambda g: (g, 0)),
            pl.BlockSpec((rows, 2 * KV_WIDTH), lambda g: (g, 0)),
            cache_spec(N_META), cache_spec(N_META), cache_spec(WINDOW), cache_spec(WINDOW),
        ],
        out_specs=[
            pl.BlockSpec((rows, D_MODEL), lambda g: (g, 0)),
            pl.BlockSpec((_ATT_SEQS, WINDOW, KV_WIDTH), lambda g: (g, 0, 0)),
            pl.BlockSpec((_ATT_SEQS, WINDOW, KV_WIDTH), lambda g: (g, 0, 0)),
        ],
        out_shape=[
            jax.ShapeDtypeStruct((SAMPLE_TOTAL, D_MODEL), BF16),
            jax.ShapeDtypeStruct((DEC_BATCH, WINDOW, KV_WIDTH), F32),
            jax.ShapeDtypeStruct((DEC_BATCH, WINDOW, KV_WIDTH), F32),
        ],
        compiler_params=_cparams(("parallel",)),
        name="attn_sample",
    )(sinks, q, kv, cmk, cmv, cwk, cwv)


_PREP_TILES_PER_PROMPT = PROMPT_ROWS // TM_PREP
_DECAY_SCALE = 0.6065306597126334


def _prep_body(*refs, prompt, vres):
    it = iter(refs)
    x_ref = next(it)
    side_ref = next(it)
    vf_ref = next(it) if vres else None
    g0_ref, mu_ref = next(it), next(it)
    wr_ref, wk_ref, wv_ref = next(it), next(it), next(it)
    w0_ref, w1_ref, w2_ref = next(it), next(it), next(it)
    a0_ref, a1_ref, a2_ref = next(it), next(it), next(it)
    if vres:
        v0_ref, v1_ref, v2_ref = next(it), next(it), next(it)
    g1_ref, g2_ref = next(it), next(it)
    r_ref, k_ref, v_ref, a_ref, lw_ref, g_ref, xn_ref = (next(it) for _ in range(7))

    tm = x_ref.shape[0]
    g0 = g0_ref[...]
    xn = _rms(x_ref[...], g0)
    row = lax.broadcasted_iota(jnp.int32, (tm, 1), 0)
    if prompt:
        ib = lax.rem(pl.program_id(0), _PREP_TILES_PER_PROMPT)
        xn = jnp.where(ib * tm + row < PAD_ROWS, 0.0, xn)
        before = _rms(side_ref[...], g0)[7:8, :]
        before = jnp.where(ib == 0, 0.0, before)
        x_prev = jnp.where(row == 0, before, pltpu.roll(xn, 1, 0))
        xn_ref[...] = xn[tm - 8:, :]
    else:
        x_prev = jnp.where(jnp.bitwise_and(row, DEC_SEQ - 1) == 0, side_ref[...], pltpu.roll(xn, 1, 0))
        xn_ref[...] = xn
    xx = x_prev - xn
    xr, xw, xk, xv, xa, xg = (xn + xx * mu_ref[c:c + 1, :] for c in range(6))

    r_ref[...] = _bdot(xr, wr_ref[...])
    k_ref[...] = _bdot(xk, wk_ref[...])
    v = _bdot(xv, wv_ref[...])
    wl = w0_ref[...] + _bdot(jnp.tanh(_bdot(xw, w1_ref[...])), w2_ref[...])
    lw_ref[...] = -_DECAY_SCALE * _sigmoid(wl)
    if vres:
        mix = _sigmoid(v0_ref[...] + _bdot(_bdot(xv, v1_ref[...]), v2_ref[...]))
        v = v + (vf_ref[...] - v) * mix
    v_ref[...] = v
    a_ref[...] = _sigmoid(a0_ref[...] + _bdot(_bdot(xa, a1_ref[...]), a2_ref[...]))
    g_ref[...] = _bdot(_sigmoid(_bdot(xg, g1_ref[...])), g2_ref[...])


def _prep(x, side, vfirst, p, prompt):
    vres = vfirst is not None
    if prompt:
        tm, nt, nrows = TM_PREP, PROMPT_TOTAL // TM_PREP, PROMPT_TOTAL
        side_spec = pl.BlockSpec((8, D_MODEL), lambda i: (jnp.maximum(i * (TM_PREP // 8) - 1, 0), 0))
        xn_spec = pl.BlockSpec((8, D_MODEL), lambda i: (i, 0))
        xn_shape = jax.ShapeDtypeStruct((nt * 8, D_MODEL), F32)
    else:
        tm, nt, nrows = TM_PREP_SAMPLE, SAMPLE_TOTAL // TM_PREP_SAMPLE, SAMPLE_TOTAL
        side_spec = pl.BlockSpec((tm, D_MODEL), lambda i: (i, 0))
        xn_spec = pl.BlockSpec((tm, D_MODEL), lambda i: (i, 0))
        xn_shape = jax.ShapeDtypeStruct((nrows, D_MODEL), F32)
    row_spec = pl.BlockSpec((tm, D_MODEL), lambda i: (i, 0))
    const = lambda a: pl.BlockSpec(a.shape, lambda i: (0,) * a.ndim)

    args = [x, side]
    specs = [row_spec, side_spec]
    if vres:
        args.append(vfirst)
        specs.append(row_spec)
    names = ["g0", "mu", "w_r", "w_k", "w_v", "w0", "w1", "w2", "a0", "a1", "a2"]
    if vres:
        names += ["v0", "v1", "v2"]
    names += ["g1", "g2"]
    for n in names:
        args.append(p[n])
        specs.append(const(p[n]))
    big = jax.ShapeDtypeStruct((nrows, D_MODEL), F32)
    return pl.pallas_call(
        functools.partial(_prep_body, prompt=prompt, vres=vres),
        grid=(nt,),
        in_specs=specs,
        out_specs=[row_spec] * 6 + [xn_spec],
        out_shape=[big] * 6 + [xn_shape],
        compiler_params=_cparams(("parallel",)),
        name="rwkv_prep_prompt" if prompt else "rwkv_prep_sample",
    )(*args)


def _eye(n):
    r = lax.broadcasted_iota(jnp.int32, (n, n), 0)
    c = lax.broadcasted_iota(jnp.int32, (n, n), 1)
    return (r == c).astype(F32)


def _mms(a_list, b_list, dims=_NN):
    return [_mm(a, b, dims) for a, b in zip(a_list, b_list)]


def _unit_lower_inverse(n_list, sub, nblocks):
    n = n_list[0].shape[0]
    eye = _eye(n)
    r = lax.broadcasted_iota(jnp.int32, (n, n), 0)
    c = lax.broadcasted_iota(jnp.int32, (n, n), 1)
    diag_blk = _div_pow2(r, sub) == _div_pow2(c, sub)
    ms = [jnp.where(diag_blk, -x, 0.0) for x in n_list]
    tds = [eye + m for m in ms]
    span = 1
    while 2 * span < sub:
        ms = _mms(ms, ms)
        tds = [td + p for td, p in zip(tds, _mms(tds, ms))]
        span *= 2
    if nblocks == 1:
        return tds
    xs = _mms(tds, [jnp.where(diag_blk, 0.0, x) for x in n_list])
    zs = [eye - x for x in xs]
    if nblocks > 2:
        assert nblocks <= 4
        zs = [z + p for z, p in zip(zs, _mms(zs, _mms(xs, xs)))]
    return _mms(zs, tds)


def _stacked_causal_mask(c, seq_len):
    row = lax.broadcasted_iota(jnp.int32, (2 * c, 2 * c), 0)
    t = jnp.bitwise_and(row, c - 1)
    j = jnp.bitwise_and(lax.broadcasted_iota(jnp.int32, (2 * c, 2 * c), 1), c - 1)
    same = _div_pow2(t, seq_len) == _div_pow2(j, seq_len)
    return same & (t - j >= jnp.where(row < c, 1, 0))


def _chunk_prologue(refs, seq_len, lead=None):
    r_ref, k_ref, v_ref, a_ref, lw_ref, g_ref = (x if lead is None else x.at[lead] for x in refs[:6])
    kk_ref, ka_ref = refs[6:]
    c = CHUNK
    row = lax.broadcasted_iota(jnp.int32, (c, c), 0)
    col = lax.broadcasted_iota(jnp.int32, (c, c), 1)
    same = _div_pow2(row, seq_len) == _div_pow2(col, seq_len)
    strict = same & (row > col)
    incl = same & (row >= col)
    lw = lw_ref[...]
    one01 = lambda m: jnp.where(m, 1.0, 0.0).astype(BF16)
    cum = _dot_exact_lhs(one01(incl), lw)
    cum_end = _dot_exact_lhs(one01(same), lw)
    k_raw = k_ref[...]
    a = a_ref[...]
    q = dict(
        mask2=_stacked_causal_mask(c, seq_len),
        r=r_ref[...], v=v_ref[...], a=a, g=g_ref[...],
        k=k_raw * (1.0 + (a - 1.0) * ka_ref[...]),
        kk_un=k_raw * kk_ref[...],
        e_cum=jnp.exp(cum), e_neg=jnp.exp(-cum), e_excl=jnp.exp(cum - lw),
        e_end=jnp.exp(cum_end - cum), p_end=jnp.exp(cum_end),
    )
    return q


def _head_terms(qs, sub, nblocks):
    hs = range(len(qs) * HEADS)
    col = lambda name: [q[name][:, h * HEAD:(h + 1) * HEAD] for q in qs for h in range(HEADS)]
    rh, kh, vh, ah, kk = col("r"), col("k"), col("v"), col("a"), col("kk_un")
    e_excl, e_cum, e_neg, e_end = col("e_excl"), col("e_cum"), col("e_neg"), col("e_end")
    kk = [x / jnp.maximum(jnp.sqrt(jnp.sum(x * x, axis=-1, keepdims=True)), 1e-12) for x in kk]
    bh = [kk[h] * ah[h] for h in hs]
    at = [kk[h] * e_excl[h] for h in hs]
    rt = [rh[h] * e_cum[h] for h in hs]
    bt = [bh[h] * e_neg[h] for h in hs]
    kt = [kh[h] * e_neg[h] for h in hs]
    bk_end = [jnp.concatenate([bh[h] * e_end[h], kh[h] * e_end[h]], axis=0) for h in hs]
    a_all = _mms([jnp.concatenate([at[h], rt[h]], axis=0) for h in hs],
                 [jnp.concatenate([bt[h], kt[h]], axis=0) for h in hs], _NT)
    a_all = [jnp.where(qs[0]["mask2"], x, 0.0) for x in a_all]
    t = _unit_lower_inverse([x[:CHUNK, :CHUNK] for x in a_all], sub, nblocks)
    w = _mms(t, at)
    ubar = [-x for x in _mms(t, _mms([x[:CHUNK, CHUNK:] for x in a_all], vh))]
    return dict(rh=rh, kh=kh, vh=vh, w_rt=[jnp.concatenate([w[h], rt[h]], axis=0) for h in hs], ubar=ubar,
                a_r=[x[CHUNK:, :] for x in a_all], bk_end=bk_end, p_end=col("p_end"), g=col("g"))


def _head_outputs(t, ys, rk_ref, lnw_ref, lnb_ref, z_refs):
    ns = range(len(ys))
    sls = [slice((n % HEADS) * HEAD, (n % HEADS + 1) * HEAD) for n in ns]
    bonus = [jnp.sum(t["rh"][n] * t["kh"][n] * rk_ref[:, sls[n]], axis=-1, keepdims=True) for n in ns]
    mean = [jnp.mean(y, axis=-1, keepdims=True) for y in ys]
    d = [ys[n] - mean[n] for n in ns]
    var = [jnp.mean(x * x, axis=-1, keepdims=True) for x in d]
    for n in ns:
        yn = d[n] * lax.rsqrt(var[n] + GN_EPS) * lnw_ref[:, sls[n]] + lnb_ref[:, sls[n]]
        z_refs[n][:, sls[n]] = ((yn + bonus[n] * t["vh"][n]) * t["g"][n]).astype(BF16)


QUAD = 4
QW = QUAD * HEAD
NQ = HEADS // QUAD
CHUNKS_PER_STEP = 5


def _bf(x):
    return x.astype(BF16)


def _dg(a, b, dims=_NN):
    return lax.dot_general(_bf(a), _bf(b), dims, preferred_element_type=F32)


def _seg_sum(x, e_seg):
    hi, lo = _split(x)
    both = jnp.dot(jnp.concatenate([hi, lo], axis=0), e_seg, preferred_element_type=F32)
    return both[:x.shape[0]] + both[x.shape[0]:]


def _chunk_prompt_body(r_ref, k_ref, v_ref, a_ref, lw_ref, g_ref, kk_ref, ka_ref, rk_ref, lnw_ref, lnb_ref,
                       z_ref, sout_ref, s_ref):
    ci = pl.program_id(1)

    @pl.when(ci == 0)
    def _():
        s_ref[...] = jnp.zeros_like(s_ref)

    rows = r_ref.shape[1]
    nsub = rows // CHUNK
    i32 = jnp.int32
    t64 = lax.broadcasted_iota(i32, (CHUNK, QW), 0)
    j64 = jnp.bitwise_and(lax.broadcasted_iota(i32, (CHUNK, QW), 1), HEAD - 1)
    strict = t64 > j64
    diag_blk = _div_pow2(t64, SUB) == _div_pow2(j64, SUB)
    eye_q = jnp.where(t64 == j64, 1.0, 0.0)
    t2 = lax.broadcasted_iota(i32, (CHUNK, 2 * QW), 0)
    j2 = jnp.bitwise_and(lax.broadcasted_iota(i32, (CHUNK, 2 * QW), 1), HEAD - 1)
    incl2 = t2 >= j2
    same_head = (_div_pow2(lax.broadcasted_iota(i32, (QW, QW), 0), HEAD)
                 == _div_pow2(lax.broadcasted_iota(i32, (QW, QW), 1), HEAD))
    bd_f32 = jnp.where(same_head, 1.0, 0.0)
    bd_bf = bd_f32.astype(BF16)

    lane = lax.broadcasted_iota(i32, (CHUNK, 2 * HEAD), 1)
    keep_lo = jnp.where(lane < HEAD, 1.0, 0.0).astype(BF16)
    keep_hi = jnp.where(lane >= HEAD, 1.0, 0.0).astype(BF16)
    zero_t = jnp.zeros((CHUNK, 2 * HEAD), BF16)

    def bd(x):
        xb = _bf(x)
        x0, x1 = xb[:, :2 * HEAD], xb[:, 2 * HEAD:]
        return jnp.concatenate([
            jnp.concatenate([x0 * keep_lo, zero_t], axis=1), jnp.concatenate([x0 * keep_hi, zero_t], axis=1),
            jnp.concatenate([zero_t, x1 * keep_lo], axis=1), jnp.concatenate([zero_t, x1 * keep_hi], axis=1),
        ], axis=0)

    rr = lax.broadcasted_iota(i32, (rows, rows), 0)
    cc = lax.broadcasted_iota(i32, (rows, rows), 1)
    same_chunk = _div_pow2(rr, CHUNK) == _div_pow2(cc, CHUNK)
    one01 = lambda m: jnp.where(m, 1.0, 0.0).astype(BF16)
    lw = lw_ref[0]
    cum = _dot_exact_lhs(one01(same_chunk & (rr >= cc)), lw)
    cum_end = _dot_exact_lhs(one01(same_chunk), lw)
    r, v, a, k_raw = r_ref[0], v_ref[0], a_ref[0], k_ref[0]
    k_mod = k_raw * (1.0 + (a - 1.0) * ka_ref[...])
    kk_un = k_raw * kk_ref[...]
    sq = kk_un * kk_un
    rkk = r * k_mod * rk_ref[...]
    qsl = [slice(q * QW, (q + 1) * QW) for q in range(NQ)]
    sums = [_seg_sum(jnp.concatenate([sq[:, s], rkk[:, s]], axis=0), bd_bf) for s in qsl]
    ss = jnp.concatenate([x[:rows] for x in sums], axis=1)
    bonus = jnp.concatenate([x[rows:] for x in sums], axis=1) * v
    kk = kk_un / jnp.maximum(jnp.sqrt(ss), 1e-12)
    b = kk * a
    e_neg = jnp.exp(-cum)
    e_end = jnp.exp(cum_end - cum)
    at = kk * jnp.exp(cum - lw)
    rt = r * jnp.exp(cum)
    bt = b * e_neg
    kt = k_mod * e_neg
    b_end = b * e_end
    k_end = k_mod * e_end
    p_end = jnp.exp(cum_end)

    probs = [(g, q) for g in range(nsub) for q in range(NQ)]
    blk = lambda x, g, q: x[g * CHUNK:(g + 1) * CHUNK, qsl[q]]
    np_ = range(len(probs))

    a_all = [_dg(jnp.concatenate([blk(at, g, q), blk(rt, g, q)], axis=0),
                 jnp.concatenate([bd(blk(bt, g, q)), bd(blk(kt, g, q))], axis=0), _NT) for g, q in probs]
    n_ab = [jnp.where(strict, x[:CHUNK, :QW], 0.0) for x in a_all]
    a_ak = [jnp.where(strict, x[:CHUNK, QW:], 0.0) for x in a_all]
    a_r = [jnp.where(incl2, x[CHUNK:, :], 0.0) for x in a_all]
    bd_v = [bd(blk(v, g, q)) for g, q in probs]
    akv = [_dg(a_ak[i], bd_v[i]) for i in np_]
    m = [jnp.where(diag_blk, -x, 0.0) for x in n_ab]
    td = [eye_q + x for x in m]
    m = [_dg(x, bd(x)) for x in m]
    for _ in range(2):
        both = [_dg(jnp.concatenate([m[i], td[i]], axis=0), bd(m[i])) for i in np_]
        m = [x[:CHUNK] for x in both]
        td = [td[i] + both[i][CHUNK:] for i in np_]
    td = [td[i] + _dg(td[i], bd(m[i])) for i in np_]
    xo = [_dg(td[i], bd(jnp.where(diag_blk, 0.0, n_ab[i]))) for i in np_]
    x2 = [_dg(x, bd(x)) for x in xo]
    zz = [eye_q - x for x in xo]
    zz = [zz[i] + _dg(zz[i], bd(x2[i])) for i in np_]
    t_inv = [_dg(zz[i], bd(td[i])) for i in np_]
    wu = [_dg(t_inv[i], jnp.concatenate([bd(blk(at, *probs[i])), bd(akv[i])], axis=1)) for i in np_]

    y_rows = []
    for g in range(nsub):
        ids = [g * NQ + q for q in range(NQ)]
        s0 = [s_ref[q] for q in range(NQ)]
        x = [_dg(jnp.concatenate([wu[i][:, :QW], blk(rt, g, q)], axis=0), s0[q], _NT)
             for q, i in enumerate(ids)]
        u = [-wu[i][:, QW:] - x[q][:CHUNK] for q, i in enumerate(ids)]
        y = [x[q][CHUNK:] + _dg(a_r[i], jnp.concatenate([bd(u[q]), bd_v[i]], axis=0))
             for q, i in enumerate(ids)]
        ds = [_dg(jnp.concatenate([u[q], blk(v, g, q)], axis=0),
                  jnp.concatenate([blk(b_end, g, q), blk(k_end, g, q)], axis=0), _TN) for q in range(NQ)]
        for q in range(NQ):
            s_ref[q] = s0[q] * p_end[g * CHUNK:g * CHUNK + 1, qsl[q]] + ds[q] * bd_f32
        y_rows.append(jnp.concatenate(y, axis=1))
    y = jnp.concatenate(y_rows, axis=0)

    inv_n = 1.0 / HEAD
    mean = jnp.concatenate([_seg_sum(y[:, s], bd_bf) for s in qsl], axis=1) * inv_n
    d = y - mean
    var = jnp.concatenate([_seg_sum((d * d)[:, s], bd_bf) for s in qsl], axis=1) * inv_n
    yn = d * lax.rsqrt(var + GN_EPS) * lnw_ref[...] + lnb_ref[...]
    z_ref[0] = ((yn + bonus) * g_ref[0]).astype(BF16)

    @pl.when(ci == pl.num_programs(1) - 1)
    def _():
        for h in range(HEADS):
            o = (h % QUAD) * HEAD
            sout_ref[0, h] = s_ref[h // QUAD, o:o + HEAD, o:o + HEAD]


def _chunk_prompt(proj, p):
    rows = CHUNKS_PER_STEP * CHUNK
    nc = PROMPT_ROWS // rows
    row_spec = pl.BlockSpec((1, rows, D_MODEL), lambda b, c: (b, c, 0))
    vec_spec = pl.BlockSpec((1, D_MODEL), lambda b, c: (0, 0))
    z, st = pl.pallas_call(
        _chunk_prompt_body,
        grid=(BATCH, nc),
        in_specs=[row_spec] * 6 + [vec_spec] * 5,
        out_specs=[
            row_spec,
            pl.BlockSpec((1, HEADS, HEAD, HEAD), lambda b, c: (b, 0, 0, 0)),
        ],
        out_shape=[
            jax.ShapeDtypeStruct((BATCH, PROMPT_ROWS, D_MODEL), BF16),
            jax.ShapeDtypeStruct((BATCH, HEADS, HEAD, HEAD), F32),
        ],
        scratch_shapes=[pltpu.VMEM((NQ, QW, QW), F32)],
        compiler_params=_cparams(("parallel", "arbitrary")),
        name="rwkv_chunk_prompt",
    )(*[x.reshape(BATCH, PROMPT_ROWS, D_MODEL) for x in proj],
      p["k_k"], p["k_a"], p["r_k"], p["lnx_w"], p["lnx_b"])
    return z.reshape(PROMPT_TOTAL, D_MODEL), st


def _chunk_sample_body(r_ref, k_ref, v_ref, a_ref, lw_ref, g_ref, kk_ref, ka_ref, rk_ref, lnw_ref, lnb_ref,
                       sin_ref, states_ref, z_ref, sout_ref):
    del states_ref
    q = _chunk_prologue((r_ref, k_ref, v_ref, a_ref, lw_ref, g_ref, kk_ref, ka_ref), DEC_SEQ)
    seq_of_row2 = _div_pow2(jnp.bitwise_and(lax.broadcasted_iota(jnp.int32, (2 * CHUNK, 1), 0), CHUNK - 1),
                            DEC_SEQ)
    hs = range(HEADS)
    seqs = range(SEQ_GROUP)
    rows = [slice(s * DEC_SEQ, (s + 1) * DEC_SEQ) for s in seqs]
    t = _head_terms([q], DEC_SEQ, 1)
    rows2 = [slice(CHUNK + s * DEC_SEQ, CHUNK + (s + 1) * DEC_SEQ) for s in seqs]
    ws = [jnp.concatenate([_mm(t["w_rt"][h][rows[s]], sin_ref[s, h], _NT) for s in seqs], axis=0) for h in hs]
    rs = [jnp.concatenate([_mm(t["w_rt"][h][rows2[s]], sin_ref[s, h], _NT) for s in seqs], axis=0) for h in hs]
    uv = [jnp.concatenate([t["ubar"][h] - ws[h], t["vh"][h]], axis=0) for h in hs]
    y = _mms(t["a_r"], uv)
    for h in hs:
        for s in seqs:
            mine = jnp.where(seq_of_row2 == s, uv[h], 0.0)
            p_end = t["p_end"][h][s * DEC_SEQ:s * DEC_SEQ + 1, :]
            sout_ref[s, h] = sin_ref[s, h] * p_end + _mm(mine, t["bk_end"][h], _TN)
    _head_outputs(t, [rs[h] + y[h] for h in hs], rk_ref, lnw_ref, lnb_ref, [z_ref] * HEADS)


def _chunk_sample(proj, p, state, new_states, layer):
    ng = DEC_BATCH // SEQ_GROUP
    row_spec = pl.BlockSpec((CHUNK, D_MODEL), lambda g: (g, 0))
    vec_spec = pl.BlockSpec((1, D_MODEL), lambda g: (0, 0))
    n_in = 6 + 5 + 2
    return pl.pallas_call(
        _chunk_sample_body,
        grid=(ng,),
        in_specs=[row_spec] * 6 + [vec_spec] * 5 + [
            pl.BlockSpec((SEQ_GROUP, HEADS, HEAD, HEAD), lambda g: (g, 0, 0, 0)),
            pl.BlockSpec(memory_space=pl.ANY),
        ],
        out_specs=[
            row_spec,
            pl.BlockSpec((SEQ_GROUP, None, HEADS, HEAD, HEAD), lambda g: (g, layer, 0, 0, 0)),
        ],
        out_shape=[
            jax.ShapeDtypeStruct((SAMPLE_TOTAL, D_MODEL), BF16),
            jax.ShapeDtypeStruct(new_states.shape, F32),
        ],
        input_output_aliases={n_in - 1: 1},
        compiler_params=_cparams(("parallel",)),
        name="rwkv_chunk_sample",
    )(*proj, p["k_k"], p["k_a"], p["r_k"], p["lnx_w"], p["lnx_b"], state, new_states)


def _pad_cols(w):
    return jnp.pad(w, ((0, 0), (0, LORA_PAD - w.shape[1]))).astype(BF16)


def _pad_rows(w):
    return jnp.pad(w, ((0, LORA_PAD - w.shape[0]), (0, 0))).astype(BF16)


def kernel(x_prompt, x_sample, state_wkv, state_shift, cache_win_k, cache_win_v, cache_meta_k,
           cache_meta_v, meta_tokens, norm_gains, rwkv_mu, rwkv_w_r, rwkv_w_k, rwkv_w_v, rwkv_w_o,
           rwkv_w0, rwkv_w1, rwkv_w2, rwkv_a0, rwkv_a1, rwkv_a2, rwkv_v0, rwkv_v1, rwkv_v2,
           rwkv_g1, rwkv_g2, rwkv_k_k, rwkv_k_a, rwkv_r_k, rwkv_lnx_w, rwkv_lnx_b,
           attn_w_qkv, attn_w_o, attn_sinks, mlp_w_up, mlp_w_down):
    assert x_prompt.shape == (BATCH, SEQ, D_MODEL) and x_sample.shape == (DEC_BATCH, DEC_SEQ, D_MODEL)
    n_swa = cache_win_k.shape[1]
    row = lambda v: v.reshape(1, D_MODEL).astype(F32)

    head = jnp.concatenate([jnp.zeros((PAD_ROWS, D_MODEL), F32), meta_tokens.astype(F32)], axis=0)
    xp = jnp.concatenate([jnp.broadcast_to(head[None], (BATCH, WINDOW, D_MODEL)), x_prompt], axis=1)
    xp = xp.reshape(PROMPT_TOTAL, D_MODEL)
    xs = x_sample.reshape(SAMPLE_TOTAL, D_MODEL).astype(F32)

    cmk = cache_meta_k.reshape(DEC_BATCH, n_swa, N_META, KV_WIDTH)
    cmv = cache_meta_v.reshape(DEC_BATCH, n_swa, N_META, KV_WIDTH)
    cwk = cache_win_k.reshape(DEC_BATCH, n_swa, WINDOW, KV_WIDTH)
    cwv = cache_win_v.reshape(DEC_BATCH, n_swa, WINDOW, KV_WIDTH)

    p_wkv, p_shift, s_shift = [], [], []
    s_wkv = jnp.zeros(state_wkv.shape, F32)
    p_wk, p_wv, p_mk, p_mv, s_wk, s_wv = [], [], [], [], [], []
    vfirst_p = vfirst_s = None
    for i in range(DEPTH):
        gains = norm_gains[i]
        j = i // 2
        if i % 2 == 0:
            p = dict(
                g0=row(gains[0]), mu=rwkv_mu[j],
                w_r=rwkv_w_r[j].astype(BF16), w_k=rwkv_w_k[j].astype(BF16), w_v=rwkv_w_v[j].astype(BF16),
                w0=row(rwkv_w0[j]), w1=_pad_cols(rwkv_w1[j]), w2=_pad_rows(rwkv_w2[j]),
                a0=row(rwkv_a0[j]), a1=_pad_cols(rwkv_a1[j]), a2=_pad_rows(rwkv_a2[j]),
                g1=_pad_cols(rwkv_g1[j]), g2=_pad_rows(rwkv_g2[j]),
                k_k=row(rwkv_k_k[j]), k_a=row(rwkv_k_a[j]), r_k=row(rwkv_r_k[j]),
                lnx_w=row(rwkv_lnx_w[j]), lnx_b=row(rwkv_lnx_b[j]),
            )
            if j > 0:
                p.update(v0=row(rwkv_v0[j - 1]), v1=_pad_cols(rwkv_v1[j - 1]), v2=_pad_rows(rwkv_v2[j - 1]))
            shift_rows = jnp.zeros((DEC_BATCH, DEC_SEQ, D_MODEL), F32).at[:, 0].set(state_shift[:, j])
            outs_p = _prep(xp, xp, vfirst_p if j > 0 else None, p, prompt=True)
            outs_s = _prep(xs, shift_rows.reshape(SAMPLE_TOTAL, D_MODEL), vfirst_s if j > 0 else None, p,
                           prompt=False)
            if j == 0:
                vfirst_p, vfirst_s = outs_p[2], outs_s[2]
            zp, st_p = _chunk_prompt(outs_p[:6], p)
            zs, s_wkv = _chunk_sample(outs_s[:6], p, state_wkv[:, j], s_wkv, j)
            xn_tail = outs_p[6].reshape(BATCH, _PREP_TILES_PER_PROMPT, 8, D_MODEL)
            p_wkv.append(st_p)
            p_shift.append(xn_tail[:, -1, -1])
            s_shift.append(outs_s[6].reshape(DEC_BATCH, DEC_SEQ, D_MODEL)[:, -1])
            w_o = rwkv_w_o[j].astype(BF16)
        else:
            w_qkv = attn_w_qkv[j].astype(BF16)
            q, kv = _qkv(xp, row(gains[0]), w_qkv)
            q_s, kv_s = _qkv(xs, row(gains[0]), w_qkv)
            sinks = attn_sinks[j].astype(F32)
            zp = _attn_prompt(sinks, q, kv)
            zs, nwk, nwv = _attn_sample(sinks, q_s, kv_s, cmk, cmv, cwk, cwv, j)
            def kv_rows(lo, hi, which):
                parts = [kv[b * PROMPT_ROWS + lo:b * PROMPT_ROWS + hi, which * KV_WIDTH:(which + 1) * KV_WIDTH]
                         for b in range(BATCH)]
                return jnp.stack(parts).reshape(BATCH, hi - lo, N_KV_HEADS, HEAD)

            p_mk.append(kv_rows(PAD_ROWS, WINDOW, 0))
            p_mv.append(kv_rows(PAD_ROWS, WINDOW, 1))
            p_wk.append(kv_rows(PROMPT_ROWS - WINDOW, PROMPT_ROWS, 0))
            p_wv.append(kv_rows(PROMPT_ROWS - WINDOW, PROMPT_ROWS, 1))
            s_wk.append(nwk.reshape(DEC_BATCH, WINDOW, N_KV_HEADS, HEAD))
            s_wv.append(nwv.reshape(DEC_BATCH, WINDOW, N_KV_HEADS, HEAD))
            w_o = attn_w_o[j].astype(BF16)
        tail = (w_o, row(gains[1]), row(gains[2]), row(gains[3]), mlp_w_up[i].astype(BF16),
                mlp_w_down[i].astype(BF16))
        xp = _mixer_mlp(xp, zp, *tail)
        xs = _mixer_mlp(xs, zs, *tail)

    y_prompt = xp.reshape(BATCH, PROMPT_ROWS, D_MODEL)[:, WINDOW:]
    y_sample = xs.reshape(DEC_BATCH, DEC_SEQ, D_MODEL)
    st = lambda xs: jnp.stack(xs, axis=1)
    return (y_prompt, y_sample, st(p_wkv), st(p_shift), st(p_wk), st(p_wv), st(p_mk), st(p_mv),
            s_wkv, st(s_shift), st(s_wk), st(s_wv))
```

```python
import functools

import jax
import jax.numpy as jnp
from jax import lax
from jax.experimental import pallas as pl
from jax.experimental.pallas import tpu as pltpu

F32 = jnp.float32
BF16 = jnp.bfloat16

D_MODEL = 1024
BATCH = 2
SEQ = 8192
DEPTH = 4
DEC_BATCH = 128
DEC_SEQ = 8
PAST_LEN = 8192
N_META = 16
HEADS = 16
HEAD = 64
N_KV_HEADS = 4
GROUP = HEADS // N_KV_HEADS
KV_WIDTH = N_KV_HEADS * HEAD
WINDOW = 128
D_FF = 4 * D_MODEL
RMS_EPS = 1e-6
GN_EPS = 6.4e-4
NEG_INF = -1e30
LORA_PAD = 128

PAD_ROWS = WINDOW - N_META
PROMPT_ROWS = PAD_ROWS + N_META + SEQ
PROMPT_TOTAL = BATCH * PROMPT_ROWS
SAMPLE_TOTAL = DEC_BATCH * DEC_SEQ
TOTAL_ROWS = PROMPT_TOTAL + SAMPLE_TOTAL

CHUNK = 64
SUB = 16
SEQ_GROUP = CHUNK // DEC_SEQ
TM_PROMPT = 640
TM_SAMPLE = 1024
TM_PREP_SAMPLE = 256
TM_PREP = 320
TF = 2048
VMEM_LIMIT = 56 * 1024 * 1024
SMALL_MM_PASSES = 1


def _cparams(sem):
    return pltpu.CompilerParams(dimension_semantics=sem, vmem_limit_bytes=VMEM_LIMIT)


def _rms(x, g):
    return x * lax.rsqrt(jnp.mean(x * x, axis=-1, keepdims=True) + RMS_EPS) * g


def _sigmoid(x):
    return 1.0 / (1.0 + jnp.exp(-x))


def _bdot(a, b):
    return jnp.dot(a.astype(BF16), b.astype(BF16), preferred_element_type=F32)


def _split(x):
    hi = x.astype(BF16)
    lo = (x - hi.astype(F32)).astype(BF16)
    return hi, lo


_NN = (((1,), (0,)), ((), ()))
_NT = (((1,), (1,)), ((), ()))
_TN = (((0,), (0,)), ((), ()))


def _mm(a, b, dims=_NN, passes=SMALL_MM_PASSES):
    dg = functools.partial(lax.dot_general, dimension_numbers=dims, preferred_element_type=F32)
    if passes == 1:
        return dg(a.astype(BF16), b.astype(BF16))
    ah, al = _split(a)
    bh, bl = _split(b)
    return dg(ah, bh) + (dg(ah, bl) + dg(al, bh))


def _div_pow2(x, n):
    shift = n.bit_length() - 1
    assert 1 << shift == n
    return jnp.right_shift(x, shift)


def _dot_exact_lhs(m01, x):
    hi, lo = _split(x)
    return (jnp.dot(m01, hi, preferred_element_type=F32)
            + jnp.dot(m01, lo, preferred_element_type=F32))


def _row_tile(rows):
    tm = TM_PROMPT if rows % TM_PROMPT == 0 else TM_SAMPLE
    assert rows % tm == 0
    return tm


def _mixer_mlp_body(x_ref, z_ref, wo_ref, g1_ref, g2_ref, g3_ref, wup_ref, wdn_ref, o_ref, x1_ref, xn_ref, acc_ref):
    f = pl.program_id(1)

    @pl.when(f == 0)
    def _():
        m = jnp.dot(z_ref[...], wo_ref[...], preferred_element_type=F32)
        x1 = x_ref[...] + _rms(m, g1_ref[...])
        x1_ref[...] = x1
        xn_ref[...] = _rms(x1, g2_ref[...]).astype(BF16)
        acc_ref[...] = jnp.zeros_like(acc_ref)

    h = jnp.dot(xn_ref[...], wup_ref[...], preferred_element_type=F32)
    a = jnp.maximum(h, 0.0)
    acc_ref[...] += jnp.dot((a * a).astype(BF16), wdn_ref[...], preferred_element_type=F32)

    @pl.when(f == pl.num_programs(1) - 1)
    def _():
        o_ref[...] = x1_ref[...] + _rms(acc_ref[...], g3_ref[...])


def _mixer_mlp(x, z, w_o, g1, g2, g3, wup, wdn):
    rows = x.shape[0]
    tm = _row_tile(rows)
    vec = pl.BlockSpec((1, D_MODEL), lambda i, f: (0, 0))
    return pl.pallas_call(
        _mixer_mlp_body,
        grid=(rows // tm, D_FF // TF),
        in_specs=[
            pl.BlockSpec((tm, D_MODEL), lambda i, f: (i, 0)),
            pl.BlockSpec((tm, D_MODEL), lambda i, f: (i, 0)),
            pl.BlockSpec((D_MODEL, D_MODEL), lambda i, f: (0, 0)),
            vec, vec, vec,
            pl.BlockSpec((D_MODEL, TF), lambda i, f: (0, f)),
            pl.BlockSpec((TF, D_MODEL), lambda i, f: (f, 0)),
        ],
        out_specs=pl.BlockSpec((tm, D_MODEL), lambda i, f: (i, 0)),
        out_shape=jax.ShapeDtypeStruct((rows, D_MODEL), F32),
        scratch_shapes=[pltpu.VMEM((tm, D_MODEL), F32), pltpu.VMEM((tm, D_MODEL), BF16),
                        pltpu.VMEM((tm, D_MODEL), F32)],
        compiler_params=_cparams(("parallel", "arbitrary")),
        name="mixer_mlp",
    )(x, z, w_o, g1, g2, g3, wup, wdn)


def _qkv_body(x_ref, g_ref, w_ref, q_ref, kv_ref):
    xn = _rms(x_ref[...], g_ref[...]).astype(BF16)
    res = jnp.dot(xn, w_ref[...], preferred_element_type=F32)
    q_ref[...] = (res[:, :D_MODEL] * (HEAD ** -0.5)).astype(BF16)
    kv_ref[...] = res[:, D_MODEL:]


def _qkv(x, g, w):
    rows = x.shape[0]
    tm = _row_tile(rows)
    return pl.pallas_call(
        _qkv_body,
        grid=(rows // tm,),
        in_specs=[
            pl.BlockSpec((tm, D_MODEL), lambda i: (i, 0)),
            pl.BlockSpec((1, D_MODEL), lambda i: (0, 0)),
            pl.BlockSpec((D_MODEL, D_MODEL + 2 * KV_WIDTH), lambda i: (0, 0)),
        ],
        out_specs=[
            pl.BlockSpec((tm, D_MODEL), lambda i: (i, 0)),
            pl.BlockSpec((tm, 2 * KV_WIDTH), lambda i: (i, 0)),
        ],
        out_shape=[
            jax.ShapeDtypeStruct((rows, D_MODEL), BF16),
            jax.ShapeDtypeStruct((rows, 2 * KV_WIDTH), F32),
        ],
        compiler_params=_cparams(("parallel",)),
        name="qkv",
    )(x, g, w)


def _alibi_slope(h):
    return 2.0 ** (-8.0 * (h + 1) / HEADS)


def _group_queries(q, kvh):
    return jnp.concatenate([q[:, h * HEAD:(h + 1) * HEAD] for h in range(kvh * GROUP, (kvh + 1) * GROUP)], axis=0)


def _group_slopes(kvh, tq):
    g = _div_pow2(lax.broadcasted_iota(jnp.int32, (GROUP * tq, 1), 0), tq)
    slope = jnp.zeros((GROUP * tq, 1), F32)
    for i in range(GROUP):
        slope = jnp.where(g == i, _alibi_slope(kvh * GROUP + i), slope)
    return slope


def _group_sinks(sink_ref, kvh):
    return [sink_ref[kvh * GROUP + i] for i in range(GROUP)]


def _alibi_bias(ok, mind, slope):
    return jnp.where(ok, slope * mind, -NEG_INF)


def _attend_groups(qs, ks, vs, biases, sinks):
    scores = [lax.dot_general(q, k, _NT, preferred_element_type=F32) for q, k in zip(qs, ks)]
    probs = []
    for s, bias, sink in zip(scores, biases, sinks):
        logits = s - bias
        tq = logits.shape[0] // GROUP
        parts = []
        for i in range(GROUP):
            lg = logits[i * tq:(i + 1) * tq]
            m = jnp.maximum(jnp.max(lg, axis=-1, keepdims=True), sink[i])
            p = jnp.exp(lg - m)
            den = jnp.sum(p, axis=-1, keepdims=True) + jnp.exp(sink[i] - m)
            parts.append(p / den)
        probs.append(jnp.concatenate(parts, axis=0).astype(BF16))
    return [jnp.dot(p, v, preferred_element_type=F32) for p, v in zip(probs, vs)]


def _store_group(o_ref, rows, kvh, o, tq):
    for i in range(GROUP):
        h = kvh * GROUP + i
        o_ref[rows, h * HEAD:(h + 1) * HEAD] = o[i * tq:(i + 1) * tq].astype(o_ref.dtype)


_BLOCKS_PER_PROMPT = PROMPT_ROWS // WINDOW


def _attn_prompt_body(sink_ref, q_ref, kv0_ref, kvp_ref, kvc_ref, o_ref, bias_ref):
    i = pl.program_id(1)
    kvhs = range(N_KV_HEADS)

    @pl.when(i <= 2)
    def _():
        shp = (GROUP * WINDOW, 3 * WINDOW)
        r = jnp.bitwise_and(lax.broadcasted_iota(jnp.int32, shp, 0), WINDOW - 1)
        c = lax.broadcasted_iota(jnp.int32, shp, 1)
        kblk = _div_pow2(c, WINDOW)
        cc = c - kblk * WINDOW
        qpos = jnp.where(i == 0, r - PAD_ROWS, N_META + (i - 1) * WINDOW + r)
        kpos = jnp.where(kblk == 0, cc - PAD_ROWS, N_META + (i - 3 + kblk) * WINDOW + cc)
        kmeta = kblk == 0
        kvalid = jnp.where(kmeta, cc - PAD_ROWS, jnp.where(kblk == 1, i - 2, i - 1)) >= 0
        dist = qpos - kpos
        ok = kvalid & (dist >= 0) & (kmeta | (dist < WINDOW))
        mind = jnp.minimum(dist, WINDOW).astype(F32)
        for j in kvhs:
            bias_ref[j] = _alibi_bias(ok, mind, _group_slopes(j, WINDOW))

    kv = jnp.concatenate([kv0_ref[...], kvp_ref[...], kvc_ref[...]], axis=0).astype(BF16)
    q = q_ref[...]
    outs = _attend_groups(
        [_group_queries(q, j) for j in kvhs],
        [kv[:, j * HEAD:(j + 1) * HEAD] for j in kvhs],
        [kv[:, KV_WIDTH + j * HEAD:KV_WIDTH + (j + 1) * HEAD] for j in kvhs],
        [bias_ref[j] for j in kvhs],
        [_group_sinks(sink_ref, j) for j in kvhs])
    for j in kvhs:
        _store_group(o_ref, slice(None), j, outs[j], WINDOW)


def _attn_prompt(sinks, q, kv):
    nb = _BLOCKS_PER_PROMPT
    return pl.pallas_call(
        _attn_prompt_body,
        grid=(BATCH, nb),
        in_specs=[
            pl.BlockSpec(memory_space=pltpu.SMEM),
            pl.BlockSpec((WINDOW, D_MODEL), lambda b, i: (b * nb + i, 0)),
            pl.BlockSpec((WINDOW, 2 * KV_WIDTH), lambda b, i: (b * nb, 0)),
            pl.BlockSpec((WINDOW, 2 * KV_WIDTH), lambda b, i: (b * nb + jnp.maximum(i - 1, 0), 0)),
            pl.BlockSpec((WINDOW, 2 * KV_WIDTH), lambda b, i: (b * nb + i, 0)),
        ],
        out_specs=pl.BlockSpec((WINDOW, D_MODEL), lambda b, i: (b * nb + i, 0)),
        out_shape=jax.ShapeDtypeStruct((PROMPT_TOTAL, D_MODEL), BF16),
        scratch_shapes=[pltpu.VMEM((N_KV_HEADS, GROUP * WINDOW, 3 * WINDOW), F32)],
        compiler_params=_cparams(("arbitrary", "arbitrary")),
        name="attn_prompt",
    )(sinks, q, kv, kv, kv)


_ATT_SEQS = 8
_SAMPLE_KEYS = N_META + WINDOW + DEC_SEQ


def _attn_sample_body(sink_ref, q_ref, kv_ref, mk_ref, mv_ref, wk_ref, wv_ref, o_ref, nwk_ref, nwv_ref):
    shp = (GROUP * DEC_SEQ, _SAMPLE_KEYS)
    t = jnp.bitwise_and(lax.broadcasted_iota(jnp.int32, shp, 0), DEC_SEQ - 1)
    c = lax.broadcasted_iota(jnp.int32, shp, 1)
    qpos = PAST_LEN + t
    kpos = jnp.where(c < N_META, c,
                     jnp.where(c < N_META + WINDOW, PAST_LEN - WINDOW + (c - N_META),
                               PAST_LEN + (c - N_META - WINDOW)))
    kmeta = c < N_META
    kvalid = kmeta | (kpos >= N_META)
    dist = qpos - kpos
    ok = kvalid & (dist >= 0) & (kmeta | (dist < WINDOW))
    mind = jnp.minimum(dist, WINDOW).astype(F32)
    kvhs = range(N_KV_HEADS)
    biases = [_alibi_bias(ok, mind, _group_slopes(j, DEC_SEQ)) for j in kvhs]
    sinks = [_group_sinks(sink_ref, j) for j in kvhs]
    qs, ks, vs, bs, cs = [], [], [], [], []
    for s in range(_ATT_SEQS):
        rows = slice(s * DEC_SEQ, (s + 1) * DEC_SEQ)
        knew = kv_ref[rows, :KV_WIDTH]
        vnew = kv_ref[rows, KV_WIDTH:]
        wk = wk_ref[s]
        wv = wv_ref[s]
        nwk_ref[s] = jnp.concatenate([wk[DEC_SEQ:], knew], axis=0)
        nwv_ref[s] = jnp.concatenate([wv[DEC_SEQ:], vnew], axis=0)
        kall = jnp.concatenate([mk_ref[s], wk, knew], axis=0).astype(BF16)
        vall = jnp.concatenate([mv_ref[s], wv, vnew], axis=0).astype(BF16)
        q = q_ref[rows, :]
        for j in kvhs:
            qs.append(_group_queries(q, j))
            ks.append(kall[:, j * HEAD:(j + 1) * HEAD])
            vs.append(vall[:, j * HEAD:(j + 1) * HEAD])
            bs.append(biases[j])
            cs.append(sinks[j])
    outs = _attend_groups(qs, ks, vs, bs, cs)
    for s in range(_ATT_SEQS):
        for j in kvhs:
            _store_group(o_ref, slice(s * DEC_SEQ, (s + 1) * DEC_SEQ), j, outs[s * N_KV_HEADS + j], DEC_SEQ)


def _attn_sample(sinks, q, kv, cmk, cmv, cwk, cwv, layer):
    rows = _ATT_SEQS * DEC_SEQ
    ng = DEC_BATCH // _ATT_SEQS
    cache_spec = lambda n: pl.BlockSpec((_ATT_SEQS, None, n, KV_WIDTH), lambda g: (g, layer, 0, 0))
    return pl.pallas_call(
        _attn_sample_body,
        grid=(ng,),
        in_specs=[
            pl.BlockSpec(memory_space=pltpu.SMEM),
            pl.BlockSpec((rows, D_MODEL), lambda g: (g, 0)),
            pl.BlockSpec((rows, 2 * KV_WIDTH), lambda g: (g, 0)),
            cache_spec(N_META), cache_spec(N_META), cache_spec(WINDOW), cache_spec(WINDOW),
        ],
        out_specs=[
            pl.BlockSpec((rows, D_MODEL), lambda g: (g, 0)),
            pl.BlockSpec((_ATT_SEQS, WINDOW, KV_WIDTH), lambda g: (g, 0, 0)),
            pl.BlockSpec((_ATT_SEQS, WINDOW, KV_WIDTH), lambda g: (g, 0, 0)),
        ],
        out_shape=[
            jax.ShapeDtypeStruct((SAMPLE_TOTAL, D_MODEL), BF16),
            jax.ShapeDtypeStruct((DEC_BATCH, WINDOW, KV_WIDTH), F32),
            jax.ShapeDtypeStruct((DEC_BATCH, WINDOW, KV_WIDTH), F32),
        ],
        compiler_params=_cparams(("parallel",)),
        name="attn_sample",
    )(sinks, q, kv, cmk, cmv, cwk, cwv)


_PREP_TILES_PER_PROMPT = PROMPT_ROWS // TM_PREP
_DECAY_SCALE = 0.6065306597126334


def _prep_body(*refs, prompt, vres):
    it = iter(refs)
    x_ref = next(it)
    side_ref = next(it)
    vf_ref = next(it) if vres else None
    g0_ref, mu_ref = next(it), next(it)
    wr_ref, wk_ref, wv_ref = next(it), next(it), next(it)
    w0_ref, w1_ref, w2_ref = next(it), next(it), next(it)
    a0_ref, a1_ref, a2_ref = next(it), next(it), next(it)
    if vres:
        v0_ref, v1_ref, v2_ref = next(it), next(it), next(it)
    g1_ref, g2_ref = next(it), next(it)
    r_ref, k_ref, v_ref, a_ref, lw_ref, g_ref, xn_ref = (next(it) for _ in range(7))

    tm = x_ref.shape[0]
    g0 = g0_ref[...]
    xn = _rms(x_ref[...], g0)
    row = lax.broadcasted_iota(jnp.int32, (tm, 1), 0)
    if prompt:
        ib = lax.rem(pl.program_id(0), _PREP_TILES_PER_PROMPT)
        xn = jnp.where(ib * tm + row < PAD_ROWS, 0.0, xn)
        before = _rms(side_ref[...], g0)[7:8, :]
        before = jnp.where(ib == 0, 0.0, before)
        x_prev = jnp.where(row == 0, before, pltpu.roll(xn, 1, 0))
        xn_ref[...] = xn[tm - 8:, :]
    else:
        x_prev = jnp.where(jnp.bitwise_and(row, DEC_SEQ - 1) == 0, side_ref[...], pltpu.roll(xn, 1, 0))
        xn_ref[...] = xn
    xx = x_prev - xn
    xr, xw, xk, xv, xa, xg = (xn + xx * mu_ref[c:c + 1, :] for c in range(6))

    r_ref[...] = _bdot(xr, wr_ref[...])
    k_ref[...] = _bdot(xk, wk_ref[...])
    v = _bdot(xv, wv_ref[...])
    wl = w0_ref[...] + _bdot(jnp.tanh(_bdot(xw, w1_ref[...])), w2_ref[...])
    lw_ref[...] = -_DECAY_SCALE * _sigmoid(wl)
    if vres:
        mix = _sigmoid(v0_ref[...] + _bdot(_bdot(xv, v1_ref[...]), v2_ref[...]))
        v = v + (vf_ref[...] - v) * mix
    v_ref[...] = v
    a_ref[...] = _sigmoid(a0_ref[...] + _bdot(_bdot(xa, a1_ref[...]), a2_ref[...]))
    g_ref[...] = _bdot(_sigmoid(_bdot(xg, g1_ref[...])), g2_ref[...])


def _prep(x, side, vfirst, p, prompt):
    vres = vfirst is not None
    if prompt:
        tm, nt, nrows = TM_PREP, PROMPT_TOTAL // TM_PREP, PROMPT_TOTAL
        side_spec = pl.BlockSpec((8, D_MODEL), lambda i: (jnp.maximum(i * (TM_PREP // 8) - 1, 0), 0))
        xn_spec = pl.BlockSpec((8, D_MODEL), lambda i: (i, 0))
        xn_shape = jax.ShapeDtypeStruct((nt * 8, D_MODEL), F32)
    else:
        tm, nt, nrows = TM_PREP_SAMPLE, SAMPLE_TOTAL // TM_PREP_SAMPLE, SAMPLE_TOTAL
        side_spec = pl.BlockSpec((tm, D_MODEL), lambda i: (i, 0))
        xn_spec = pl.BlockSpec((tm, D_MODEL), lambda i: (i, 0))
        xn_shape = jax.ShapeDtypeStruct((nrows, D_MODEL), F32)
    row_spec = pl.BlockSpec((tm, D_MODEL), lambda i: (i, 0))
    const = lambda a: pl.BlockSpec(a.shape, lambda i: (0,) * a.ndim)

    args = [x, side]
    specs = [row_spec, side_spec]
    if vres:
        args.append(vfirst)
        specs.append(row_spec)
    names = ["g0", "mu", "w_r", "w_k", "w_v", "w0", "w1", "w2", "a0", "a1", "a2"]
    if vres:
        names += ["v0", "v1", "v2"]
    names += ["g1", "g2"]
    for n in names:
        args.append(p[n])
        specs.append(const(p[n]))
    big = jax.ShapeDtypeStruct((nrows, D_MODEL), F32)
    return pl.pallas_call(
        functools.partial(_prep_body, prompt=prompt, vres=vres),
        grid=(nt,),
        in_specs=specs,
        out_specs=[row_spec] * 6 + [xn_spec],
        out_shape=[big] * 6 + [xn_shape],
        compiler_params=_cparams(("parallel",)),
        name="rwkv_prep_prompt" if prompt else "rwkv_prep_sample",
    )(*args)


def _eye(n):
    r = lax.broadcasted_iota(jnp.int32, (n, n), 0)
    c = lax.broadcasted_iota(jnp.int32, (n, n), 1)
    return (r == c).astype(F32)


def _mms(a_list, b_list, dims=_NN):
    return [_mm(a, b, dims) for a, b in zip(a_list, b_list)]


def _unit_lower_inverse(n_list, sub, nblocks):
    n = n_list[0].shape[0]
    eye = _eye(n)
    r = lax.broadcasted_iota(jnp.int32, (n, n), 0)
    c = lax.broadcasted_iota(jnp.int32, (n, n), 1)
    diag_blk = _div_pow2(r, sub) == _div_pow2(c, sub)
    ms = [jnp.where(diag_blk, -x, 0.0) for x in n_list]
    tds = [eye + m for m in ms]
    span = 1
    while 2 * span < sub:
        ms = _mms(ms, ms)
        tds = [td + p for td, p in zip(tds, _mms(tds, ms))]
        span *= 2
    if nblocks == 1:
        return tds
    xs = _mms(tds, [jnp.where(diag_blk, 0.0, x) for x in n_list])
    zs = [eye - x for x in xs]
    if nblocks > 2:
        assert nblocks <= 4
        zs = [z + p for z, p in zip(zs, _mms(zs, _mms(xs, xs)))]
    return _mms(zs, tds)


def _stacked_causal_mask(c, seq_len):
    row = lax.broadcasted_iota(jnp.int32, (2 * c, 2 * c), 0)
    t = jnp.bitwise_and(row, c - 1)
    j = jnp.bitwise_and(lax.broadcasted_iota(jnp.int32, (2 * c, 2 * c), 1), c - 1)
    same = _div_pow2(t, seq_len) == _div_pow2(j, seq_len)
    return same & (t - j >= jnp.where(row < c, 1, 0))


def _chunk_prologue(refs, seq_len, lead=None):
    r_ref, k_ref, v_ref, a_ref, lw_ref, g_ref = (x if lead is None else x.at[lead] for x in refs[:6])
    kk_ref, ka_ref = refs[6:]
    c = CHUNK
    row = lax.broadcasted_iota(jnp.int32, (c, c), 0)
    col = lax.broadcasted_iota(jnp.int32, (c, c), 1)
    same = _div_pow2(row, seq_len) == _div_pow2(col, seq_len)
    strict = same & (row > col)
    incl = same & (row >= col)
    lw = lw_ref[...]
    one01 = lambda m: jnp.where(m, 1.0, 0.0).astype(BF16)
    cum = _dot_exact_lhs(one01(incl), lw)
    cum_end = _dot_exact_lhs(one01(same), lw)
    k_raw = k_ref[...]
    a = a_ref[...]
    q = dict(
        mask2=_stacked_causal_mask(c, seq_len),
        r=r_ref[...], v=v_ref[...], a=a, g=g_ref[...],
        k=k_raw * (1.0 + (a - 1.0) * ka_ref[...]),
        kk_un=k_raw * kk_ref[...],
        e_cum=jnp.exp(cum), e_neg=jnp.exp(-cum), e_excl=jnp.exp(cum - lw),
        e_end=jnp.exp(cum_end - cum), p_end=jnp.exp(cum_end),
    )
    return q


def _head_terms(qs, sub, nblocks):
    hs = range(len(qs) * HEADS)
    col = lambda name: [q[name][:, h * HEAD:(h + 1) * HEAD] for q in qs for h in range(HEADS)]
    rh, kh, vh, ah, kk = col("r"), col("k"), col("v"), col("a"), col("kk_un")
    e_excl, e_cum, e_neg, e_end = col("e_excl"), col("e_cum"), col("e_neg"), col("e_end")
    kk = [x / jnp.maximum(jnp.sqrt(jnp.sum(x * x, axis=-1, keepdims=True)), 1e-12) for x in kk]
    bh = [kk[h] * ah[h] for h in hs]
    at = [kk[h] * e_excl[h] for h in hs]
    rt = [rh[h] * e_cum[h] for h in hs]
    bt = [bh[h] * e_neg[h] for h in hs]
    kt = [kh[h] * e_neg[h] for h in hs]
    bk_end = [jnp.concatenate([bh[h] * e_end[h], kh[h] * e_end[h]], axis=0) for h in hs]
    a_all = _mms([jnp.concatenate([at[h], rt[h]], axis=0) for h in hs],
                 [jnp.concatenate([bt[h], kt[h]], axis=0) for h in hs], _NT)
    a_all = [jnp.where(qs[0]["mask2"], x, 0.0) for x in a_all]
    t = _unit_lower_inverse([x[:CHUNK, :CHUNK] for x in a_all], sub, nblocks)
    w = _mms(t, at)
    ubar = [-x for x in _mms(t, _mms([x[:CHUNK, CHUNK:] for x in a_all], vh))]
    return dict(rh=rh, kh=kh, vh=vh, w_rt=[jnp.concatenate([w[h], rt[h]], axis=0) for h in hs], ubar=ubar,
                a_r=[x[CHUNK:, :] for x in a_all], bk_end=bk_end, p_end=col("p_end"), g=col("g"))


def _head_outputs(t, ys, rk_ref, lnw_ref, lnb_ref, z_refs):
    ns = range(len(ys))
    sls = [slice((n % HEADS) * HEAD, (n % HEADS + 1) * HEAD) for n in ns]
    bonus = [jnp.sum(t["rh"][n] * t["kh"][n] * rk_ref[:, sls[n]], axis=-1, keepdims=True) for n in ns]
    mean = [jnp.mean(y, axis=-1, keepdims=True) for y in ys]
    d = [ys[n] - mean[n] for n in ns]
    var = [jnp.mean(x * x, axis=-1, keepdims=True) for x in d]
    for n in ns:
        yn = d[n] * lax.rsqrt(var[n] + GN_EPS) * lnw_ref[:, sls[n]] + lnb_ref[:, sls[n]]
        z_refs[n][:, sls[n]] = ((yn + bonus[n] * t["vh"][n]) * t["g"][n]).astype(BF16)


QUAD = 4
QW = QUAD * HEAD
NQ = HEADS // QUAD
CHUNKS_PER_STEP = 5


def _bf(x):
    return x.astype(BF16)


def _dg(a, b, dims=_NN):
    return lax.dot_general(_bf(a), _bf(b), dims, preferred_element_type=F32)


def _seg_sum(x, e_seg):
    hi, lo = _split(x)
    both = jnp.dot(jnp.concatenate([hi, lo], axis=0), e_seg, preferred_element_type=F32)
    return both[:x.shape[0]] + both[x.shape[0]:]


def _chunk_prompt_body(r_ref, k_ref, v_ref, a_ref, lw_ref, g_ref, kk_ref, ka_ref, rk_ref, lnw_ref, lnb_ref,
                       z_ref, sout_ref, s_ref):
    ci = pl.program_id(1)

    @pl.when(ci == 0)
    def _():
        s_ref[...] = jnp.zeros_like(s_ref)

    rows = r_ref.shape[1]
    nsub = rows // CHUNK
    i32 = jnp.int32
    t64 = lax.broadcasted_iota(i32, (CHUNK, QW), 0)
    j64 = jnp.bitwise_and(lax.broadcasted_iota(i32, (CHUNK, QW), 1), HEAD - 1)
    strict = t64 > j64
    diag_blk = _div_pow2(t64, SUB) == _div_pow2(j64, SUB)
    eye_q = jnp.where(t64 == j64, 1.0, 0.0)
    t2 = lax.broadcasted_iota(i32, (CHUNK, 2 * QW), 0)
    j2 = jnp.bitwise_and(lax.broadcasted_iota(i32, (CHUNK, 2 * QW), 1), HEAD - 1)
    incl2 = t2 >= j2
    same_head = (_div_pow2(lax.broadcasted_iota(i32, (QW, QW), 0), HEAD)
                 == _div_pow2(lax.broadcasted_iota(i32, (QW, QW), 1), HEAD))
    bd_f32 = jnp.where(same_head, 1.0, 0.0)
    bd_bf = bd_f32.astype(BF16)

    lane = lax.broadcasted_iota(i32, (CHUNK, 2 * HEAD), 1)
    keep_lo = jnp.where(lane < HEAD, 1.0, 0.0).astype(BF16)
    keep_hi = jnp.where(lane >= HEAD, 1.0, 0.0).astype(BF16)
    zero_t = jnp.zeros((CHUNK, 2 * HEAD), BF16)

    def bd(x):
        xb = _bf(x)
        x0, x1 = xb[:, :2 * HEAD], xb[:, 2 * HEAD:]
        return jnp.concatenate([
            jnp.concatenate([x0 * keep_lo, zero_t], axis=1), jnp.concatenate([x0 * keep_hi, zero_t], axis=1),
            jnp.concatenate([zero_t, x1 * keep_lo], axis=1), jnp.concatenate([zero_t, x1 * keep_hi], axis=1),
        ], axis=0)

    rr = lax.broadcasted_iota(i32, (rows, rows), 0)
    cc = lax.broadcasted_iota(i32, (rows, rows), 1)
    same_chunk = _div_pow2(rr, CHUNK) == _div_pow2(cc, CHUNK)
    one01 = lambda m: jnp.where(m, 1.0, 0.0).astype(BF16)
    lw = lw_ref[0]
    cum = _dot_exact_lhs(one01(same_chunk & (rr >= cc)), lw)
    cum_end = _dot_exact_lhs(one01(same_chunk), lw)
    r, v, a, k_raw = r_ref[0], v_ref[0], a_ref[0], k_ref[0]
    k_mod = k_raw * (1.0 + (a - 1.0) * ka_ref[...])
    kk_un = k_raw * kk_ref[...]
    sq = kk_un * kk_un
    rkk = r * k_mod * rk_ref[...]
    qsl = [slice(q * QW, (q + 1) * QW) for q in range(NQ)]
    sums = [_seg_sum(jnp.concatenate([sq[:, s], rkk[:, s]], axis=0), bd_bf) for s in qsl]
    ss = jnp.concatenate([x[:rows] for x in sums], axis=1)
    bonus = jnp.concatenate([x[rows:] for x in sums], axis=1) * v
    kk = kk_un / jnp.maximum(jnp.sqrt(ss), 1e-12)
    b = kk * a
    e_neg = jnp.exp(-cum)
    e_end = jnp.exp(cum_end - cum)
    at = kk * jnp.exp(cum - lw)
    rt = r * jnp.exp(cum)
    bt = b * e_neg
    kt = k_mod * e_neg
    b_end = b * e_end
    k_end = k_mod * e_end
    p_end = jnp.exp(cum_end)

    probs = [(g, q) for g in range(nsub) for q in range(NQ)]
    blk = lambda x, g, q: x[g * CHUNK:(g + 1) * CHUNK, qsl[q]]
    np_ = range(len(probs))

    a_all = [_dg(jnp.concatenate([blk(at, g, q), blk(rt, g, q)], axis=0),
                 jnp.concatenate([bd(blk(bt, g, q)), bd(blk(kt, g, q))], axis=0), _NT) for g, q in probs]
    n_ab = [jnp.where(strict, x[:CHUNK, :QW], 0.0) for x in a_all]
    a_ak = [_bf(jnp.where(strict, x[:CHUNK, QW:], 0.0)) for x in a_all]
    a_r = [_bf(jnp.where(incl2, x[CHUNK:, :], 0.0)) for x in a_all]
    bd_v = [bd(blk(v, g, q)) for g, q in probs]
    akv = [_dg(a_ak[i], bd_v[i]) for i in np_]
    m = [jnp.where(diag_blk, -x, 0.0) for x in n_ab]
    td = [eye_q + x for x in m]
    m = [_dg(x, bd(x)) for x in m]
    for _ in range(2):
        both = [_dg(jnp.concatenate([m[i], td[i]], axis=0), bd(m[i])) for i in np_]
        m = [x[:CHUNK] for x in both]
        td = [td[i] + both[i][CHUNK:] for i in np_]
    td = [td[i] + _dg(td[i], bd(m[i])) for i in np_]
    xo = [_dg(td[i], bd(jnp.where(diag_blk, 0.0, n_ab[i]))) for i in np_]
    x2 = [_dg(x, bd(x)) for x in xo]
    zz = [eye_q - x for x in xo]
    zz = [zz[i] + _dg(zz[i], bd(x2[i])) for i in np_]
    t_inv = [_bf(_dg(zz[i], bd(td[i]))) for i in np_]
    wu = [_dg(t_inv[i], jnp.concatenate([bd(blk(at, *probs[i])), bd(akv[i])], axis=1)) for i in np_]
    w_b = [_bf(x[:, :QW]) for x in wu]
    ubar_neg = [x[:, QW:] for x in wu]

    y_rows = []
    for g in range(nsub):
        ids = [g * NQ + q for q in range(NQ)]
        s0 = [s_ref[q] for q in range(NQ)]
        x = [_dg(jnp.concatenate([w_b[i], _bf(blk(rt, g, q))], axis=0), s0[q], _NT)
             for q, i in enumerate(ids)]
        u = [-ubar_neg[i] - x[q][:CHUNK] for q, i in enumerate(ids)]
        y = [x[q][CHUNK:] + _dg(a_r[i], jnp.concatenate([bd(u[q]), bd_v[i]], axis=0))
             for q, i in enumerate(ids)]
        ds = [_dg(jnp.concatenate([u[q], blk(v, g, q)], axis=0),
                  jnp.concatenate([blk(b_end, g, q), blk(k_end, g, q)], axis=0), _TN) for q in range(NQ)]
        for q in range(NQ):
            s_ref[q] = s0[q] * p_end[g * CHUNK:g * CHUNK + 1, qsl[q]] + ds[q] * bd_f32
        y_rows.append(jnp.concatenate(y, axis=1))
    y = jnp.concatenate(y_rows, axis=0)

    inv_n = 1.0 / HEAD
    mean = jnp.concatenate([_seg_sum(y[:, s], bd_bf) for s in qsl], axis=1) * inv_n
    d = y - mean
    var = jnp.concatenate([_seg_sum((d * d)[:, s], bd_bf) for s in qsl], axis=1) * inv_n
    yn = d * lax.rsqrt(var + GN_EPS) * lnw_ref[...] + lnb_ref[...]
    z_ref[0] = ((yn + bonus) * g_ref[0]).astype(BF16)

    @pl.when(ci == pl.num_programs(1) - 1)
    def _():
        for h in range(HEADS):
            o = (h % QUAD) * HEAD
            sout_ref[0, h] = s_ref[h // QUAD, o:o + HEAD, o:o + HEAD]


def _chunk_prompt(proj, p):
    rows = CHUNKS_PER_STEP * CHUNK
    nc = PROMPT_ROWS // rows
    row_spec = pl.BlockSpec((1, rows, D_MODEL), lambda b, c: (b, c, 0))
    vec_spec = pl.BlockSpec((1, D_MODEL), lambda b, c: (0, 0))
    z, st = pl.pallas_call(
        _chunk_prompt_body,
        grid=(BATCH, nc),
        in_specs=[row_spec] * 6 + [vec_spec] * 5,
        out_specs=[
            row_spec,
            pl.BlockSpec((1, HEADS, HEAD, HEAD), lambda b, c: (b, 0, 0, 0)),
        ],
        out_shape=[
            jax.ShapeDtypeStruct((BATCH, PROMPT_ROWS, D_MODEL), BF16),
            jax.ShapeDtypeStruct((BATCH, HEADS, HEAD, HEAD), F32),
        ],
        scratch_shapes=[pltpu.VMEM((NQ, QW, QW), F32)],
        compiler_params=_cparams(("parallel", "arbitrary")),
        name="rwkv_chunk_prompt",
    )(*[x.reshape(BATCH, PROMPT_ROWS, D_MODEL) for x in proj],
      p["k_k"], p["k_a"], p["r_k"], p["lnx_w"], p["lnx_b"])
    return z.reshape(PROMPT_TOTAL, D_MODEL), st


def _chunk_sample_body(*refs, layer, n_layers):
    first = layer == 0
    (r_ref, k_ref, v_ref, a_ref, lw_ref, g_ref, kk_ref, ka_ref, rk_ref, lnw_ref, lnb_ref, sin_ref) = refs[:12]
    z_ref, sout_ref = refs[-2:]
    q = _chunk_prologue((r_ref, k_ref, v_ref, a_ref, lw_ref, g_ref, kk_ref, ka_ref), DEC_SEQ)
    seq_of_row2 = _div_pow2(jnp.bitwise_and(lax.broadcasted_iota(jnp.int32, (2 * CHUNK, 1), 0), CHUNK - 1),
                            DEC_SEQ)
    hs = range(HEADS)
    seqs = range(SEQ_GROUP)
    rows = [slice(s * DEC_SEQ, (s + 1) * DEC_SEQ) for s in seqs]
    t = _head_terms([q], DEC_SEQ, 1)
    rows2 = [slice(CHUNK + s * DEC_SEQ, CHUNK + (s + 1) * DEC_SEQ) for s in seqs]
    ws = [jnp.concatenate([_mm(t["w_rt"][h][rows[s]], sin_ref[s, h], _NT) for s in seqs], axis=0) for h in hs]
    rs = [jnp.concatenate([_mm(t["w_rt"][h][rows2[s]], sin_ref[s, h], _NT) for s in seqs], axis=0) for h in hs]
    uv = [jnp.concatenate([t["ubar"][h] - ws[h], t["vh"][h]], axis=0) for h in hs]
    y = _mms(t["a_r"], uv)
    if first:
        for other in range(1, n_layers):
            sout_ref[:, other] = jnp.zeros((SEQ_GROUP, HEADS, HEAD, HEAD), F32)
    for h in hs:
        for s in seqs:
            mine = jnp.where(seq_of_row2 == s, uv[h], 0.0)
            p_end = t["p_end"][h][s * DEC_SEQ:s * DEC_SEQ + 1, :]
            new = sin_ref[s, h] * p_end + _mm(mine, t["bk_end"][h], _TN)
            if first:
                sout_ref[s, 0, h] = new
            else:
                sout_ref[s, h] = new
    _head_outputs(t, [rs[h] + y[h] for h in hs], rk_ref, lnw_ref, lnb_ref, [z_ref] * HEADS)


def _chunk_sample(proj, p, state, new_states, layer, n_layers):
    ng = DEC_BATCH // SEQ_GROUP
    row_spec = pl.BlockSpec((CHUNK, D_MODEL), lambda g: (g, 0))
    vec_spec = pl.BlockSpec((1, D_MODEL), lambda g: (0, 0))
    in_specs = [row_spec] * 6 + [vec_spec] * 5 + [pl.BlockSpec((SEQ_GROUP, HEADS, HEAD, HEAD), lambda g: (g, 0, 0, 0))]
    args = [*proj, p["k_k"], p["k_a"], p["r_k"], p["lnx_w"], p["lnx_b"], state]
    if layer == 0:
        state_spec = pl.BlockSpec((SEQ_GROUP, n_layers, HEADS, HEAD, HEAD), lambda g: (g, 0, 0, 0, 0))
        aliases = {}
    else:
        state_spec = pl.BlockSpec((SEQ_GROUP, None, HEADS, HEAD, HEAD), lambda g: (g, layer, 0, 0, 0))
        in_specs.append(pl.BlockSpec(memory_space=pl.ANY))
        args.append(new_states)
        aliases = {len(args) - 1: 1}
    return pl.pallas_call(
        functools.partial(_chunk_sample_body, layer=layer, n_layers=n_layers),
        grid=(ng,),
        in_specs=in_specs,
        out_specs=[row_spec, state_spec],
        out_shape=[
            jax.ShapeDtypeStruct((SAMPLE_TOTAL, D_MODEL), BF16),
            jax.ShapeDtypeStruct((DEC_BATCH, n_layers, HEADS, HEAD, HEAD), F32),
        ],
        input_output_aliases=aliases,
        compiler_params=_cparams(("parallel",)),
        name="rwkv_chunk_sample",
    )(*args)


def _pad_cols(w):
    return jnp.pad(w, ((0, 0), (0, LORA_PAD - w.shape[1]))).astype(BF16)


def _pad_rows(w):
    return jnp.pad(w, ((0, LORA_PAD - w.shape[0]), (0, 0))).astype(BF16)


def kernel(x_prompt, x_sample, state_wkv, state_shift, cache_win_k, cache_win_v, cache_meta_k,
           cache_meta_v, meta_tokens, norm_gains, rwkv_mu, rwkv_w_r, rwkv_w_k, rwkv_w_v, rwkv_w_o,
           rwkv_w0, rwkv_w1, rwkv_w2, rwkv_a0, rwkv_a1, rwkv_a2, rwkv_v0, rwkv_v1, rwkv_v2,
           rwkv_g1, rwkv_g2, rwkv_k_k, rwkv_k_a, rwkv_r_k, rwkv_lnx_w, rwkv_lnx_b,
           attn_w_qkv, attn_w_o, attn_sinks, mlp_w_up, mlp_w_down):
    assert x_prompt.shape == (BATCH, SEQ, D_MODEL) and x_sample.shape == (DEC_BATCH, DEC_SEQ, D_MODEL)
    n_swa = cache_win_k.shape[1]
    row = lambda v: v.reshape(1, D_MODEL).astype(F32)

    head = jnp.concatenate([jnp.zeros((PAD_ROWS, D_MODEL), F32), meta_tokens.astype(F32)], axis=0)
    xp = jnp.concatenate([jnp.broadcast_to(head[None], (BATCH, WINDOW, D_MODEL)), x_prompt], axis=1)
    xp = xp.reshape(PROMPT_TOTAL, D_MODEL)
    xs = x_sample.reshape(SAMPLE_TOTAL, D_MODEL).astype(F32)

    cmk = cache_meta_k.reshape(DEC_BATCH, n_swa, N_META, KV_WIDTH)
    cmv = cache_meta_v.reshape(DEC_BATCH, n_swa, N_META, KV_WIDTH)
    cwk = cache_win_k.reshape(DEC_BATCH, n_swa, WINDOW, KV_WIDTH)
    cwv = cache_win_v.reshape(DEC_BATCH, n_swa, WINDOW, KV_WIDTH)

    p_wkv, p_shift, s_shift = [], [], []
    s_wkv = None
    p_wk, p_wv, p_mk, p_mv, s_wk, s_wv = [], [], [], [], [], []
    vfirst_p = vfirst_s = None
    for i in range(DEPTH):
        gains = norm_gains[i]
        j = i // 2
        if i % 2 == 0:
            p = dict(
                g0=row(gains[0]), mu=rwkv_mu[j],
                w_r=rwkv_w_r[j].astype(BF16), w_k=rwkv_w_k[j].astype(BF16), w_v=rwkv_w_v[j].astype(BF16),
                w0=row(rwkv_w0[j]), w1=_pad_cols(rwkv_w1[j]), w2=_pad_rows(rwkv_w2[j]),
                a0=row(rwkv_a0[j]), a1=_pad_cols(rwkv_a1[j]), a2=_pad_rows(rwkv_a2[j]),
                g1=_pad_cols(rwkv_g1[j]), g2=_pad_rows(rwkv_g2[j]),
                k_k=row(rwkv_k_k[j]), k_a=row(rwkv_k_a[j]), r_k=row(rwkv_r_k[j]),
                lnx_w=row(rwkv_lnx_w[j]), lnx_b=row(rwkv_lnx_b[j]),
            )
            if j > 0:
                p.update(v0=row(rwkv_v0[j - 1]), v1=_pad_cols(rwkv_v1[j - 1]), v2=_pad_rows(rwkv_v2[j - 1]))
            shift_rows = jnp.zeros((DEC_BATCH, DEC_SEQ, D_MODEL), F32).at[:, 0].set(state_shift[:, j])
            outs_p = _prep(xp, xp, vfirst_p if j > 0 else None, p, prompt=True)
            outs_s = _prep(xs, shift_rows.reshape(SAMPLE_TOTAL, D_MODEL), vfirst_s if j > 0 else None, p,
                           prompt=False)
            if j == 0:
                vfirst_p, vfirst_s = outs_p[2], outs_s[2]
            zp, st_p = _chunk_prompt(outs_p[:6], p)
            zs, s_wkv = _chunk_sample(outs_s[:6], p, state_wkv[:, j], s_wkv, j, state_wkv.shape[1])
            xn_tail = outs_p[6].reshape(BATCH, _PREP_TILES_PER_PROMPT, 8, D_MODEL)
            p_wkv.append(st_p)
            p_shift.append(xn_tail[:, -1, -1])
            s_shift.append(outs_s[6].reshape(DEC_BATCH, DEC_SEQ, D_MODEL)[:, -1])
            w_o = rwkv_w_o[j].astype(BF16)
        else:
            w_qkv = attn_w_qkv[j].astype(BF16)
            q, kv = _qkv(xp, row(gains[0]), w_qkv)
            q_s, kv_s = _qkv(xs, row(gains[0]), w_qkv)
            sinks = attn_sinks[j].astype(F32)
            zp = _attn_prompt(sinks, q, kv)
            zs, nwk, nwv = _attn_sample(sinks, q_s, kv_s, cmk, cmv, cwk, cwv, j)
            def kv_rows(lo, hi, which):
                parts = [kv[b * PROMPT_ROWS + lo:b * PROMPT_ROWS + hi, which * KV_WIDTH:(which + 1) * KV_WIDTH]
                         for b in range(BATCH)]
                return jnp.stack(parts).reshape(BATCH, hi - lo, N_KV_HEADS, HEAD)

            p_mk.append(kv_rows(PAD_ROWS, WINDOW, 0))
            p_mv.append(kv_rows(PAD_ROWS, WINDOW, 1))
            p_wk.append(kv_rows(PROMPT_ROWS - WINDOW, PROMPT_ROWS, 0))
            p_wv.append(kv_rows(PROMPT_ROWS - WINDOW, PROMPT_ROWS, 1))
            s_wk.append(nwk.reshape(DEC_BATCH, WINDOW, N_KV_HEADS, HEAD))
            s_wv.append(nwv.reshape(DEC_BATCH, WINDOW, N_KV_HEADS, HEAD))
            w_o = attn_w_o[j].astype(BF16)
        tail = (w_o, row(gains[1]), row(gains[2]), row(gains[3]), mlp_w_up[i].astype(BF16),
                mlp_w_down[i].astype(BF16))
        xp = _mixer_mlp(xp, zp, *tail)
        xs = _mixer_mlp(xs, zs, *tail)

    y_prompt = xp.reshape(BATCH, PROMPT_ROWS, D_MODEL)[:, WINDOW:]
    y_sample = xs.reshape(DEC_BATCH, DEC_SEQ, D_MODEL)
    st = lambda xs: jnp.stack(xs, axis=1)
    return (y_prompt, y_sample, st(p_wkv), st(p_shift), st(p_wk), st(p_wv), st(p_mk), st(p_mv),
            s_wkv, st(s_shift), st(s_wk), st(s_wv))
```

```python
import functools

import jax
import jax.numpy as jnp
from jax import lax
from jax.experimental import pallas as pl
from jax.experimental.pallas import tpu as pltpu

F32 = jnp.float32
BF16 = jnp.bfloat16

D_MODEL = 1024
BATCH = 2
SEQ = 8192
DEPTH = 4
DEC_BATCH = 128
DEC_SEQ = 8
PAST_LEN = 8192
N_META = 16
HEADS = 16
HEAD = 64
N_KV_HEADS = 4
GROUP = HEADS // N_KV_HEADS
KV_WIDTH = N_KV_HEADS * HEAD
WINDOW = 128
D_FF = 4 * D_MODEL
RMS_EPS = 1e-6
GN_EPS = 6.4e-4
NEG_INF = -1e30
LORA_PAD = 128

PAD_ROWS = WINDOW - N_META
PROMPT_ROWS = PAD_ROWS + N_META + SEQ
PROMPT_TOTAL = BATCH * PROMPT_ROWS
SAMPLE_TOTAL = DEC_BATCH * DEC_SEQ
TOTAL_ROWS = PROMPT_TOTAL + SAMPLE_TOTAL

CHUNK = 64
SUB = 16
SEQ_GROUP = CHUNK // DEC_SEQ
TM_PROMPT = 640
TM_SAMPLE = 512
TM_PREP_SAMPLE = 256
TM_PREP = 320
TF = 2048
VMEM_LIMIT = 56 * 1024 * 1024
SMALL_MM_PASSES = 1


def _cparams(sem):
    return pltpu.CompilerParams(dimension_semantics=sem, vmem_limit_bytes=VMEM_LIMIT)


def _weight_spec(w):
    if isinstance(w, tuple):
        arr, layer = w
        shape, index = (None,) + arr.shape[1:], (layer,) + (0,) * (arr.ndim - 1)
    else:
        arr, shape, index = w, w.shape, (0,) * w.ndim
    return pl.BlockSpec(shape, lambda *_: index), arr


def _rms(x, g):
    return x * lax.rsqrt(jnp.mean(x * x, axis=-1, keepdims=True) + RMS_EPS) * g


def _sigmoid(x):
    return 1.0 / (1.0 + jnp.exp(-x))


def _bdot(a, b):
    return jnp.dot(a.astype(BF16), b.astype(BF16), preferred_element_type=F32)


def _split(x):
    hi = x.astype(BF16)
    lo = (x - hi.astype(F32)).astype(BF16)
    return hi, lo


_NN = (((1,), (0,)), ((), ()))
_NT = (((1,), (1,)), ((), ()))
_TN = (((0,), (0,)), ((), ()))


def _mm(a, b, dims=_NN, passes=SMALL_MM_PASSES):
    dg = functools.partial(lax.dot_general, dimension_numbers=dims, preferred_element_type=F32)
    if passes == 1:
        return dg(a.astype(BF16), b.astype(BF16))
    ah, al = _split(a)
    bh, bl = _split(b)
    return dg(ah, bh) + (dg(ah, bl) + dg(al, bh))


def _div_pow2(x, n):
    shift = n.bit_length() - 1
    assert 1 << shift == n
    return jnp.right_shift(x, shift)


def _dot_exact_lhs(m01, x):
    hi, lo = _split(x)
    return (jnp.dot(m01, hi, preferred_element_type=F32)
            + jnp.dot(m01, lo, preferred_element_type=F32))


def _row_tile(rows):
    tm = TM_PROMPT if rows % TM_PROMPT == 0 else TM_SAMPLE
    assert rows % tm == 0
    return tm


def _mixer_mlp_body(x_ref, z_ref, wo_ref, g1_ref, g2_ref, g3_ref, wup_ref, wdn_ref, o_ref, x1_ref, xn_ref, acc_ref):
    f = pl.program_id(1)

    @pl.when(f == 0)
    def _():
        m = jnp.dot(z_ref[...], wo_ref[...], preferred_element_type=F32)
        x1 = x_ref[...] + _rms(m, g1_ref[...])
        x1_ref[...] = x1
        xn_ref[...] = _rms(x1, g2_ref[...]).astype(BF16)
        acc_ref[...] = jnp.zeros_like(acc_ref)

    h = jnp.dot(xn_ref[...], wup_ref[...], preferred_element_type=F32)
    a = jnp.maximum(h, 0.0)
    acc_ref[...] += jnp.dot((a * a).astype(BF16), wdn_ref[...], preferred_element_type=F32)

    @pl.when(f == pl.num_programs(1) - 1)
    def _():
        o_ref[...] = x1_ref[...] + _rms(acc_ref[...], g3_ref[...])


def _mixer_mlp(x, z, w_o, g1, g2, g3, wup, wdn, layer):
    rows = x.shape[0]
    tm = _row_tile(rows)
    vec = pl.BlockSpec((1, D_MODEL), lambda i, f: (0, 0))
    wo_spec, w_o = _weight_spec(w_o)
    return pl.pallas_call(
        _mixer_mlp_body,
        grid=(rows // tm, D_FF // TF),
        in_specs=[
            pl.BlockSpec((tm, D_MODEL), lambda i, f: (i, 0)),
            pl.BlockSpec((tm, D_MODEL), lambda i, f: (i, 0)),
            wo_spec,
            vec, vec, vec,
            pl.BlockSpec((None, D_MODEL, TF), lambda i, f: (layer, 0, f)),
            pl.BlockSpec((None, TF, D_MODEL), lambda i, f: (layer, f, 0)),
        ],
        out_specs=pl.BlockSpec((tm, D_MODEL), lambda i, f: (i, 0)),
        out_shape=jax.ShapeDtypeStruct((rows, D_MODEL), F32),
        scratch_shapes=[pltpu.VMEM((tm, D_MODEL), F32), pltpu.VMEM((tm, D_MODEL), BF16),
                        pltpu.VMEM((tm, D_MODEL), F32)],
        compiler_params=_cparams(("parallel", "arbitrary")),
        name="mixer_mlp",
    )(x, z, w_o, g1, g2, g3, wup, wdn)


def _qkv_body(x_ref, g_ref, w_ref, q_ref, kv_ref):
    xn = _rms(x_ref[...], g_ref[...]).astype(BF16)
    res = jnp.dot(xn, w_ref[...], preferred_element_type=F32)
    q_ref[...] = (res[:, :D_MODEL] * (HEAD ** -0.5)).astype(BF16)
    kv_ref[...] = res[:, D_MODEL:]


def _qkv(x, g, w):
    rows = x.shape[0]
    tm = _row_tile(rows)
    w_spec, w = _weight_spec(w)
    return pl.pallas_call(
        _qkv_body,
        grid=(rows // tm,),
        in_specs=[
            pl.BlockSpec((tm, D_MODEL), lambda i: (i, 0)),
            pl.BlockSpec((1, D_MODEL), lambda i: (0, 0)),
            w_spec,
        ],
        out_specs=[
            pl.BlockSpec((tm, D_MODEL), lambda i: (i, 0)),
            pl.BlockSpec((tm, 2 * KV_WIDTH), lambda i: (i, 0)),
        ],
        out_shape=[
            jax.ShapeDtypeStruct((rows, D_MODEL), BF16),
            jax.ShapeDtypeStruct((rows, 2 * KV_WIDTH), F32),
        ],
        compiler_params=_cparams(("parallel",)),
        name="qkv",
    )(x, g, w)


def _alibi_slope(h):
    return 2.0 ** (-8.0 * (h + 1) / HEADS)


def _group_queries(q, kvh):
    return jnp.concatenate([q[:, h * HEAD:(h + 1) * HEAD] for h in range(kvh * GROUP, (kvh + 1) * GROUP)], axis=0)


def _group_slopes(kvh, tq):
    g = _div_pow2(lax.broadcasted_iota(jnp.int32, (GROUP * tq, 1), 0), tq)
    slope = jnp.zeros((GROUP * tq, 1), F32)
    for i in range(GROUP):
        slope = jnp.where(g == i, _alibi_slope(kvh * GROUP + i), slope)
    return slope


def _group_sinks(sink_ref, kvh):
    return [sink_ref[kvh * GROUP + i] for i in range(GROUP)]


def _alibi_bias(ok, mind, slope):
    return jnp.where(ok, slope * mind, -NEG_INF)


def _attend_groups(qs, ks, vs, biases, sinks):
    scores = [lax.dot_general(q, k, _NT, preferred_element_type=F32) for q, k in zip(qs, ks)]
    probs = []
    for s, bias, sink in zip(scores, biases, sinks):
        logits = s - bias
        tq = logits.shape[0] // GROUP
        parts = []
        for i in range(GROUP):
            lg = logits[i * tq:(i + 1) * tq]
            m = jnp.maximum(jnp.max(lg, axis=-1, keepdims=True), sink[i])
            p = jnp.exp(lg - m)
            den = jnp.sum(p, axis=-1, keepdims=True) + jnp.exp(sink[i] - m)
            parts.append(p / den)
        probs.append(jnp.concatenate(parts, axis=0).astype(BF16))
    return [jnp.dot(p, v, preferred_element_type=F32) for p, v in zip(probs, vs)]


def _store_group(o_ref, rows, kvh, o, tq):
    for i in range(GROUP):
        h = kvh * GROUP + i
        o_ref[rows, h * HEAD:(h + 1) * HEAD] = o[i * tq:(i + 1) * tq].astype(o_ref.dtype)


_BLOCKS_PER_PROMPT = PROMPT_ROWS // WINDOW


def _attn_prompt_body(sink_ref, q_ref, kv0_ref, kvp_ref, kvc_ref, o_ref, bias_ref):
    i = pl.program_id(1)
    kvhs = range(N_KV_HEADS)

    @pl.when(i <= 2)
    def _():
        shp = (GROUP * WINDOW, 3 * WINDOW)
        r = jnp.bitwise_and(lax.broadcasted_iota(jnp.int32, shp, 0), WINDOW - 1)
        c = lax.broadcasted_iota(jnp.int32, shp, 1)
        kblk = _div_pow2(c, WINDOW)
        cc = c - kblk * WINDOW
        qpos = jnp.where(i == 0, r - PAD_ROWS, N_META + (i - 1) * WINDOW + r)
        kpos = jnp.where(kblk == 0, cc - PAD_ROWS, N_META + (i - 3 + kblk) * WINDOW + cc)
        kmeta = kblk == 0
        kvalid = jnp.where(kmeta, cc - PAD_ROWS, jnp.where(kblk == 1, i - 2, i - 1)) >= 0
        dist = qpos - kpos
        ok = kvalid & (dist >= 0) & (kmeta | (dist < WINDOW))
        mind = jnp.minimum(dist, WINDOW).astype(F32)
        for j in kvhs:
            bias_ref[j] = _alibi_bias(ok, mind, _group_slopes(j, WINDOW))

    kv = jnp.concatenate([kv0_ref[...], kvp_ref[...], kvc_ref[...]], axis=0).astype(BF16)
    q = q_ref[...]
    outs = _attend_groups(
        [_group_queries(q, j) for j in kvhs],
        [kv[:, j * HEAD:(j + 1) * HEAD] for j in kvhs],
        [kv[:, KV_WIDTH + j * HEAD:KV_WIDTH + (j + 1) * HEAD] for j in kvhs],
        [bias_ref[j] for j in kvhs],
        [_group_sinks(sink_ref, j) for j in kvhs])
    for j in kvhs:
        _store_group(o_ref, slice(None), j, outs[j], WINDOW)


def _attn_prompt(sinks, q, kv):
    nb = _BLOCKS_PER_PROMPT
    return pl.pallas_call(
        _attn_prompt_body,
        grid=(BATCH, nb),
        in_specs=[
            pl.BlockSpec(memory_space=pltpu.SMEM),
            pl.BlockSpec((WINDOW, D_MODEL), lambda b, i: (b * nb + i, 0)),
            pl.BlockSpec((WINDOW, 2 * KV_WIDTH), lambda b, i: (b * nb, 0)),
            pl.BlockSpec((WINDOW, 2 * KV_WIDTH), lambda b, i: (b * nb + jnp.maximum(i - 1, 0), 0)),
            pl.BlockSpec((WINDOW, 2 * KV_WIDTH), lambda b, i: (b * nb + i, 0)),
        ],
        out_specs=pl.BlockSpec((WINDOW, D_MODEL), lambda b, i: (b * nb + i, 0)),
        out_shape=jax.ShapeDtypeStruct((PROMPT_TOTAL, D_MODEL), BF16),
        scratch_shapes=[pltpu.VMEM((N_KV_HEADS, GROUP * WINDOW, 3 * WINDOW), F32)],
        compiler_params=_cparams(("arbitrary", "arbitrary")),
        name="attn_prompt",
    )(sinks, q, kv, kv, kv)


_ATT_SEQS = 8
_SAMPLE_KEYS = N_META + WINDOW + DEC_SEQ


def _attn_sample_body(sink_ref, q_ref, kv_ref, mk_ref, mv_ref, wk_ref, wv_ref, o_ref, nwk_ref, nwv_ref):
    shp = (GROUP * DEC_SEQ, _SAMPLE_KEYS)
    t = jnp.bitwise_and(lax.broadcasted_iota(jnp.int32, shp, 0), DEC_SEQ - 1)
    c = lax.broadcasted_iota(jnp.int32, shp, 1)
    qpos = PAST_LEN + t
    kpos = jnp.where(c < N_META, c,
                     jnp.where(c < N_META + WINDOW, PAST_LEN - WINDOW + (c - N_META),
                               PAST_LEN + (c - N_META - WINDOW)))
    kmeta = c < N_META
    kvalid = kmeta | (kpos >= N_META)
    dist = qpos - kpos
    ok = kvalid & (dist >= 0) & (kmeta | (dist < WINDOW))
    mind = jnp.minimum(dist, WINDOW).astype(F32)
    kvhs = range(N_KV_HEADS)
    biases = [_alibi_bias(ok, mind, _group_slopes(j, DEC_SEQ)) for j in kvhs]
    sinks = [_group_sinks(sink_ref, j) for j in kvhs]
    qs, ks, vs, bs, cs = [], [], [], [], []
    for s in range(_ATT_SEQS):
        rows = slice(s * DEC_SEQ, (s + 1) * DEC_SEQ)
        knew = kv_ref[rows, :KV_WIDTH]
        vnew = kv_ref[rows, KV_WIDTH:]
        wk = wk_ref[s]
        wv = wv_ref[s]
        nwk_ref[s] = jnp.concatenate([wk[DEC_SEQ:], knew], axis=0)
        nwv_ref[s] = jnp.concatenate([wv[DEC_SEQ:], vnew], axis=0)
        kall = jnp.concatenate([mk_ref[s], wk, knew], axis=0).astype(BF16)
        vall = jnp.concatenate([mv_ref[s], wv, vnew], axis=0).astype(BF16)
        q = q_ref[rows, :]
        for j in kvhs:
            qs.append(_group_queries(q, j))
            ks.append(kall[:, j * HEAD:(j + 1) * HEAD])
            vs.append(vall[:, j * HEAD:(j + 1) * HEAD])
            bs.append(biases[j])
            cs.append(sinks[j])
    outs = _attend_groups(qs, ks, vs, bs, cs)
    for s in range(_ATT_SEQS):
        for j in kvhs:
            _store_group(o_ref, slice(s * DEC_SEQ, (s + 1) * DEC_SEQ), j, outs[s * N_KV_HEADS + j], DEC_SEQ)


def _attn_sample(sinks, q, kv, cmk, cmv, cwk, cwv, layer):
    rows = _ATT_SEQS * DEC_SEQ
    ng = DEC_BATCH // _ATT_SEQS
    cache_spec = lambda n: pl.BlockSpec((_ATT_SEQS, None, n, KV_WIDTH), lambda g: (g, layer, 0, 0))
    return pl.pallas_call(
        _attn_sample_body,
        grid=(ng,),
        in_specs=[
            pl.BlockSpec(memory_space=pltpu.SMEM),
            pl.BlockSpec((rows, D_MODEL), lambda g: (g, 0)),
            pl.BlockSpec((rows, 2 * KV_WIDTH), lambda g: (g, 0)),
            cache_spec(N_META), cache_spec(N_META), cache_spec(WINDOW), cache_spec(WINDOW),
        ],
        out_specs=[
            pl.BlockSpec((rows, D_MODEL), lambda g: (g, 0)),
            pl.BlockSpec((_ATT_SEQS, WINDOW, KV_WIDTH), lambda g: (g, 0, 0)),
            pl.BlockSpec((_ATT_SEQS, WINDOW, KV_WIDTH), lambda g: (g, 0, 0)),
        ],
        out_shape=[
            jax.ShapeDtypeStruct((SAMPLE_TOTAL, D_MODEL), BF16),
            jax.ShapeDtypeStruct((DEC_BATCH, WINDOW, KV_WIDTH), F32),
            jax.ShapeDtypeStruct((DEC_BATCH, WINDOW, KV_WIDTH), F32),
        ],
        compiler_params=_cparams(("parallel",)),
        name="attn_sample",
    )(sinks, q, kv, cmk, cmv, cwk, cwv)


_PREP_TILES_PER_PROMPT = PROMPT_ROWS // TM_PREP
_DECAY_SCALE = 0.6065306597126334


def _prep_body(*refs, prompt, vres):
    it = iter(refs)
    x_ref = next(it)
    side_ref = next(it)
    vf_ref = next(it) if vres else None
    g0_ref, mu_ref = next(it), next(it)
    wr_ref, wk_ref, wv_ref = next(it), next(it), next(it)
    w0_ref, w1_ref, w2_ref = next(it), next(it), next(it)
    a0_ref, a1_ref, a2_ref = next(it), next(it), next(it)
    if vres:
        v0_ref, v1_ref, v2_ref = next(it), next(it), next(it)
    g1_ref, g2_ref = next(it), next(it)
    r_ref, k_ref, v_ref, a_ref, lw_ref, g_ref, xn_ref = (next(it) for _ in range(7))

    tm = x_ref.shape[0]
    g0 = g0_ref[...]
    xn = _rms(x_ref[...], g0)
    row = lax.broadcasted_iota(jnp.int32, (tm, 1), 0)
    if prompt:
        ib = lax.rem(pl.program_id(0), _PREP_TILES_PER_PROMPT)
        xn = jnp.where(ib * tm + row < PAD_ROWS, 0.0, xn)
        before = _rms(side_ref[...], g0)[7:8, :]
        before = jnp.where(ib == 0, 0.0, before)
        x_prev = jnp.where(row == 0, before, pltpu.roll(xn, 1, 0))
        xn_ref[...] = xn[tm - 8:, :]
    else:
        x_prev = jnp.where(jnp.bitwise_and(row, DEC_SEQ - 1) == 0, side_ref[...], pltpu.roll(xn, 1, 0))
        xn_ref[...] = xn
    xx = x_prev - xn
    xr, xw, xk, xv, xa, xg = (xn + xx * mu_ref[c:c + 1, :] for c in range(6))

    r_ref[...] = _bdot(xr, wr_ref[...])
    k_ref[...] = _bdot(xk, wk_ref[...])
    v = _bdot(xv, wv_ref[...])
    wl = w0_ref[...] + _bdot(jnp.tanh(_bdot(xw, w1_ref[...])), w2_ref[...])
    lw_ref[...] = -_DECAY_SCALE * _sigmoid(wl)
    if vres:
        mix = _sigmoid(v0_ref[...] + _bdot(_bdot(xv, v1_ref[...]), v2_ref[...]))
        v = v + (vf_ref[...] - v) * mix
    v_ref[...] = v
    a_ref[...] = _sigmoid(a0_ref[...] + _bdot(_bdot(xa, a1_ref[...]), a2_ref[...]))
    g_ref[...] = _bdot(_sigmoid(_bdot(xg, g1_ref[...])), g2_ref[...])


def _prep(x, side, vfirst, p, prompt):
    vres = vfirst is not None
    if prompt:
        tm, nt, nrows = TM_PREP, PROMPT_TOTAL // TM_PREP, PROMPT_TOTAL
        side_spec = pl.BlockSpec((8, D_MODEL), lambda i: (jnp.maximum(i * (TM_PREP // 8) - 1, 0), 0))
        xn_spec = pl.BlockSpec((8, D_MODEL), lambda i: (i, 0))
        xn_shape = jax.ShapeDtypeStruct((nt * 8, D_MODEL), F32)
    else:
        tm, nt, nrows = TM_PREP_SAMPLE, SAMPLE_TOTAL // TM_PREP_SAMPLE, SAMPLE_TOTAL
        side_spec = pl.BlockSpec((tm, D_MODEL), lambda i: (i, 0))
        xn_spec = pl.BlockSpec((tm, D_MODEL), lambda i: (i, 0))
        xn_shape = jax.ShapeDtypeStruct((nrows, D_MODEL), F32)
    row_spec = pl.BlockSpec((tm, D_MODEL), lambda i: (i, 0))

    args = [x, side]
    specs = [row_spec, side_spec]
    if vres:
        args.append(vfirst)
        specs.append(row_spec)
    names = ["g0", "mu", "w_r", "w_k", "w_v", "w0", "w1", "w2", "a0", "a1", "a2"]
    if vres:
        names += ["v0", "v1", "v2"]
    names += ["g1", "g2"]
    for n in names:
        spec, arr = _weight_spec(p[n])
        args.append(arr)
        specs.append(spec)
    big = jax.ShapeDtypeStruct((nrows, D_MODEL), F32)
    return pl.pallas_call(
        functools.partial(_prep_body, prompt=prompt, vres=vres),
        grid=(nt,),
        in_specs=specs,
        out_specs=[row_spec] * 6 + [xn_spec],
        out_shape=[big] * 6 + [xn_shape],
        compiler_params=_cparams(("parallel",)),
        name="rwkv_prep_prompt" if prompt else "rwkv_prep_sample",
    )(*args)


def _eye(n):
    r = lax.broadcasted_iota(jnp.int32, (n, n), 0)
    c = lax.broadcasted_iota(jnp.int32, (n, n), 1)
    return (r == c).astype(F32)


def _mms(a_list, b_list, dims=_NN):
    return [_mm(a, b, dims) for a, b in zip(a_list, b_list)]


def _unit_lower_inverse(n_list, sub, nblocks):
    n = n_list[0].shape[0]
    eye = _eye(n)
    r = lax.broadcasted_iota(jnp.int32, (n, n), 0)
    c = lax.broadcasted_iota(jnp.int32, (n, n), 1)
    diag_blk = _div_pow2(r, sub) == _div_pow2(c, sub)
    ms = [jnp.where(diag_blk, -x, 0.0) for x in n_list]
    tds = [eye + m for m in ms]
    span = 1
    while 2 * span < sub:
        ms = _mms(ms, ms)
        tds = [td + p for td, p in zip(tds, _mms(tds, ms))]
        span *= 2
    if nblocks == 1:
        return tds
    xs = _mms(tds, [jnp.where(diag_blk, 0.0, x) for x in n_list])
    zs = [eye - x for x in xs]
    if nblocks > 2:
        assert nblocks <= 4
        zs = [z + p for z, p in zip(zs, _mms(zs, _mms(xs, xs)))]
    return _mms(zs, tds)


def _stacked_causal_mask(c, seq_len):
    row = lax.broadcasted_iota(jnp.int32, (2 * c, 2 * c), 0)
    t = jnp.bitwise_and(row, c - 1)
    j = jnp.bitwise_and(lax.broadcasted_iota(jnp.int32, (2 * c, 2 * c), 1), c - 1)
    same = _div_pow2(t, seq_len) == _div_pow2(j, seq_len)
    return same & (t - j >= jnp.where(row < c, 1, 0))


def _chunk_prologue(refs, seq_len, lead=None):
    r_ref, k_ref, v_ref, a_ref, lw_ref, g_ref = (x if lead is None else x.at[lead] for x in refs[:6])
    kk_ref, ka_ref = refs[6:]
    c = CHUNK
    row = lax.broadcasted_iota(jnp.int32, (c, c), 0)
    col = lax.broadcasted_iota(jnp.int32, (c, c), 1)
    same = _div_pow2(row, seq_len) == _div_pow2(col, seq_len)
    strict = same & (row > col)
    incl = same & (row >= col)
    lw = lw_ref[...]
    one01 = lambda m: jnp.where(m, 1.0, 0.0).astype(BF16)
    cum = _dot_exact_lhs(one01(incl), lw)
    cum_end = _dot_exact_lhs(one01(same), lw)
    k_raw = k_ref[...]
    a = a_ref[...]
    q = dict(
        mask2=_stacked_causal_mask(c, seq_len),
        r=r_ref[...], v=v_ref[...], a=a, g=g_ref[...],
        k=k_raw * (1.0 + (a - 1.0) * ka_ref[...]),
        kk_un=k_raw * kk_ref[...],
        e_cum=jnp.exp(cum), e_neg=jnp.exp(-cum), e_excl=jnp.exp(cum - lw),
        e_end=jnp.exp(cum_end - cum), p_end=jnp.exp(cum_end),
    )
    return q


def _head_terms(qs, sub, nblocks):
    hs = range(len(qs) * HEADS)
    col = lambda name: [q[name][:, h * HEAD:(h + 1) * HEAD] for q in qs for h in range(HEADS)]
    rh, kh, vh, ah, kk = col("r"), col("k"), col("v"), col("a"), col("kk_un")
    e_excl, e_cum, e_neg, e_end = col("e_excl"), col("e_cum"), col("e_neg"), col("e_end")
    kk = [x / jnp.maximum(jnp.sqrt(jnp.sum(x * x, axis=-1, keepdims=True)), 1e-12) for x in kk]
    bh = [kk[h] * ah[h] for h in hs]
    at = [kk[h] * e_excl[h] for h in hs]
    rt = [rh[h] * e_cum[h] for h in hs]
    bt = [bh[h] * e_neg[h] for h in hs]
    kt = [kh[h] * e_neg[h] for h in hs]
    bk_end = [jnp.concatenate([bh[h] * e_end[h], kh[h] * e_end[h]], axis=0) for h in hs]
    a_all = _mms([jnp.concatenate([at[h], rt[h]], axis=0) for h in hs],
                 [jnp.concatenate([bt[h], kt[h]], axis=0) for h in hs], _NT)
    a_all = [jnp.where(qs[0]["mask2"], x, 0.0) for x in a_all]
    t = _unit_lower_inverse([x[:CHUNK, :CHUNK] for x in a_all], sub, nblocks)
    w = _mms(t, at)
    ubar = [-x for x in _mms(t, _mms([x[:CHUNK, CHUNK:] for x in a_all], vh))]
    return dict(rh=rh, kh=kh, vh=vh, w_rt=[jnp.concatenate([w[h], rt[h]], axis=0) for h in hs], ubar=ubar,
                a_r=[x[CHUNK:, :] for x in a_all], bk_end=bk_end, p_end=col("p_end"), g=col("g"))


def _head_outputs(t, ys, rk_ref, lnw_ref, lnb_ref, z_refs):
    ns = range(len(ys))
    sls = [slice((n % HEADS) * HEAD, (n % HEADS + 1) * HEAD) for n in ns]
    bonus = [jnp.sum(t["rh"][n] * t["kh"][n] * rk_ref[:, sls[n]], axis=-1, keepdims=True) for n in ns]
    mean = [jnp.mean(y, axis=-1, keepdims=True) for y in ys]
    d = [ys[n] - mean[n] for n in ns]
    var = [jnp.mean(x * x, axis=-1, keepdims=True) for x in d]
    for n in ns:
        yn = d[n] * lax.rsqrt(var[n] + GN_EPS) * lnw_ref[:, sls[n]] + lnb_ref[:, sls[n]]
        z_refs[n][:, sls[n]] = ((yn + bonus[n] * t["vh"][n]) * t["g"][n]).astype(BF16)


QUAD = 4
QW = QUAD * HEAD
NQ = HEADS // QUAD
CHUNKS_PER_STEP = 5


def _bf(x):
    return x.astype(BF16)


def _dg(a, b, dims=_NN):
    return lax.dot_general(_bf(a), _bf(b), dims, preferred_element_type=F32)


def _seg_sum(x, e_seg):
    hi, lo = _split(x)
    both = jnp.dot(jnp.concatenate([hi, lo], axis=0), e_seg, preferred_element_type=F32)
    return both[:x.shape[0]] + both[x.shape[0]:]


def _chunk_prompt_body(r_ref, k_ref, v_ref, a_ref, lw_ref, g_ref, kk_ref, ka_ref, rk_ref, lnw_ref, lnb_ref,
                       z_ref, sout_ref, s_ref):
    ci = pl.program_id(1)

    @pl.when(ci == 0)
    def _():
        s_ref[...] = jnp.zeros_like(s_ref)

    rows = r_ref.shape[1]
    nsub = rows // CHUNK
    i32 = jnp.int32
    t64 = lax.broadcasted_iota(i32, (CHUNK, QW), 0)
    j64 = jnp.bitwise_and(lax.broadcasted_iota(i32, (CHUNK, QW), 1), HEAD - 1)
    strict = t64 > j64
    diag_blk = _div_pow2(t64, SUB) == _div_pow2(j64, SUB)
    eye_q = jnp.where(t64 == j64, 1.0, 0.0)
    t2 = lax.broadcasted_iota(i32, (CHUNK, 2 * QW), 0)
    j2 = jnp.bitwise_and(lax.broadcasted_iota(i32, (CHUNK, 2 * QW), 1), HEAD - 1)
    incl2 = t2 >= j2
    same_head = (_div_pow2(lax.broadcasted_iota(i32, (QW, QW), 0), HEAD)
                 == _div_pow2(lax.broadcasted_iota(i32, (QW, QW), 1), HEAD))
    bd_f32 = jnp.where(same_head, 1.0, 0.0)
    bd_bf = bd_f32.astype(BF16)

    lane = lax.broadcasted_iota(i32, (CHUNK, 2 * HEAD), 1)
    keep_lo = jnp.where(lane < HEAD, 1.0, 0.0).astype(BF16)
    keep_hi = jnp.where(lane >= HEAD, 1.0, 0.0).astype(BF16)
    zero_t = jnp.zeros((CHUNK, 2 * HEAD), BF16)

    def bd(x):
        xb = _bf(x)
        x0, x1 = xb[:, :2 * HEAD], xb[:, 2 * HEAD:]
        return jnp.concatenate([
            jnp.concatenate([x0 * keep_lo, zero_t], axis=1), jnp.concatenate([x0 * keep_hi, zero_t], axis=1),
            jnp.concatenate([zero_t, x1 * keep_lo], axis=1), jnp.concatenate([zero_t, x1 * keep_hi], axis=1),
        ], axis=0)

    rr = lax.broadcasted_iota(i32, (rows, rows), 0)
    cc = lax.broadcasted_iota(i32, (rows, rows), 1)
    same_chunk = _div_pow2(rr, CHUNK) == _div_pow2(cc, CHUNK)
    one01 = lambda m: jnp.where(m, 1.0, 0.0).astype(BF16)
    lw = lw_ref[0]
    cum = _dot_exact_lhs(one01(same_chunk & (rr >= cc)), lw)
    cum_end = _dot_exact_lhs(one01(same_chunk), lw)
    r, v, a, k_raw = r_ref[0], v_ref[0], a_ref[0], k_ref[0]
    k_mod = k_raw * (1.0 + (a - 1.0) * ka_ref[...])
    kk_un = k_raw * kk_ref[...]
    sq = kk_un * kk_un
    rkk = r * k_mod * rk_ref[...]
    qsl = [slice(q * QW, (q + 1) * QW) for q in range(NQ)]
    sums = [_seg_sum(jnp.concatenate([sq[:, s], rkk[:, s]], axis=0), bd_bf) for s in qsl]
    ss = jnp.concatenate([x[:rows] for x in sums], axis=1)
    bonus = jnp.concatenate([x[rows:] for x in sums], axis=1) * v
    kk = kk_un / jnp.maximum(jnp.sqrt(ss), 1e-12)
    b = kk * a
    e_neg = jnp.exp(-cum)
    e_end = jnp.exp(cum_end - cum)
    at = kk * jnp.exp(cum - lw)
    rt = r * jnp.exp(cum)
    bt = b * e_neg
    kt = k_mod * e_neg
    b_end = b * e_end
    k_end = k_mod * e_end
    p_end = jnp.exp(cum_end)

    probs = [(g, q) for g in range(nsub) for q in range(NQ)]
    blk = lambda x, g, q: x[g * CHUNK:(g + 1) * CHUNK, qsl[q]]
    np_ = range(len(probs))

    a_all = [_dg(jnp.concatenate([blk(at, g, q), blk(rt, g, q)], axis=0),
                 jnp.concatenate([bd(blk(bt, g, q)), bd(blk(kt, g, q))], axis=0), _NT) for g, q in probs]
    n_ab = [jnp.where(strict, x[:CHUNK, :QW], 0.0) for x in a_all]
    a_ak = [_bf(jnp.where(strict, x[:CHUNK, QW:], 0.0)) for x in a_all]
    a_r = [_bf(jnp.where(incl2, x[CHUNK:, :], 0.0)) for x in a_all]
    bd_v = [bd(blk(v, g, q)) for g, q in probs]
    akv = [_dg(a_ak[i], bd_v[i]) for i in np_]
    m = [jnp.where(diag_blk, -x, 0.0) for x in n_ab]
    td = [eye_q + x for x in m]
    m = [_dg(x, bd(x)) for x in m]
    for _ in range(2):
        both = [_dg(jnp.concatenate([m[i], td[i]], axis=0), bd(m[i])) for i in np_]
        m = [x[:CHUNK] for x in both]
        td = [td[i] + both[i][CHUNK:] for i in np_]
    td = [td[i] + _dg(td[i], bd(m[i])) for i in np_]
    xo = [_dg(td[i], bd(jnp.where(diag_blk, 0.0, n_ab[i]))) for i in np_]
    x2 = [_dg(x, bd(x)) for x in xo]
    zz = [eye_q - x for x in xo]
    zz = [zz[i] + _dg(zz[i], bd(x2[i])) for i in np_]
    t_inv = [_bf(_dg(zz[i], bd(td[i]))) for i in np_]
    wu = [_dg(t_inv[i], jnp.concatenate([bd(blk(at, *probs[i])), bd(akv[i])], axis=1)) for i in np_]
    w_b = [_bf(x[:, :QW]) for x in wu]
    ubar_neg = [x[:, QW:] for x in wu]

    y_rows = []
    for g in range(nsub):
        ids = [g * NQ + q for q in range(NQ)]
        s0 = [s_ref[q] for q in range(NQ)]
        x = [_dg(jnp.concatenate([w_b[i], _bf(blk(rt, g, q))], axis=0), s0[q], _NT)
             for q, i in enumerate(ids)]
        u = [-ubar_neg[i] - x[q][:CHUNK] for q, i in enumerate(ids)]
        y = [x[q][CHUNK:] + _dg(a_r[i], jnp.concatenate([bd(u[q]), bd_v[i]], axis=0))
             for q, i in enumerate(ids)]
        ds = [_dg(jnp.concatenate([u[q], blk(v, g, q)], axis=0),
                  jnp.concatenate([blk(b_end, g, q), blk(k_end, g, q)], axis=0), _TN) for q in range(NQ)]
        for q in range(NQ):
            s_ref[q] = s0[q] * p_end[g * CHUNK:g * CHUNK + 1, qsl[q]] + ds[q] * bd_f32
        y_rows.append(jnp.concatenate(y, axis=1))
    y = jnp.concatenate(y_rows, axis=0)

    inv_n = 1.0 / HEAD
    mean = jnp.concatenate([_seg_sum(y[:, s], bd_bf) for s in qsl], axis=1) * inv_n
    d = y - mean
    var = jnp.concatenate([_seg_sum((d * d)[:, s], bd_bf) for s in qsl], axis=1) * inv_n
    yn = d * lax.rsqrt(var + GN_EPS) * lnw_ref[...] + lnb_ref[...]
    z_ref[0] = ((yn + bonus) * g_ref[0]).astype(BF16)

    @pl.when(ci == pl.num_programs(1) - 1)
    def _():
        for h in range(HEADS):
            o = (h % QUAD) * HEAD
            sout_ref[0, h] = s_ref[h // QUAD, o:o + HEAD, o:o + HEAD]


def _chunk_prompt(proj, p):
    rows = CHUNKS_PER_STEP * CHUNK
    nc = PROMPT_ROWS // rows
    row_spec = pl.BlockSpec((1, rows, D_MODEL), lambda b, c: (b, c, 0))
    vec_spec = pl.BlockSpec((1, D_MODEL), lambda b, c: (0, 0))
    z, st = pl.pallas_call(
        _chunk_prompt_body,
        grid=(BATCH, nc),
        in_specs=[row_spec] * 6 + [vec_spec] * 5,
        out_specs=[
            row_spec,
            pl.BlockSpec((1, HEADS, HEAD, HEAD), lambda b, c: (b, 0, 0, 0)),
        ],
        out_shape=[
            jax.ShapeDtypeStruct((BATCH, PROMPT_ROWS, D_MODEL), BF16),
            jax.ShapeDtypeStruct((BATCH, HEADS, HEAD, HEAD), F32),
        ],
        scratch_shapes=[pltpu.VMEM((NQ, QW, QW), F32)],
        compiler_params=_cparams(("parallel", "arbitrary")),
        name="rwkv_chunk_prompt",
    )(*[x.reshape(BATCH, PROMPT_ROWS, D_MODEL) for x in proj],
      p["k_k"], p["k_a"], p["r_k"], p["lnx_w"], p["lnx_b"])
    return z.reshape(PROMPT_TOTAL, D_MODEL), st


def _chunk_sample_body(*refs, layer, n_layers):
    first = layer == 0
    (r_ref, k_ref, v_ref, a_ref, lw_ref, g_ref, kk_ref, ka_ref, rk_ref, lnw_ref, lnb_ref, sin_ref) = refs[:12]
    z_ref, sout_ref = refs[-2:]
    q = _chunk_prologue((r_ref, k_ref, v_ref, a_ref, lw_ref, g_ref, kk_ref, ka_ref), DEC_SEQ)
    seq_of_row2 = _div_pow2(jnp.bitwise_and(lax.broadcasted_iota(jnp.int32, (2 * CHUNK, 1), 0), CHUNK - 1),
                            DEC_SEQ)
    hs = range(HEADS)
    seqs = range(SEQ_GROUP)
    rows = [slice(s * DEC_SEQ, (s + 1) * DEC_SEQ) for s in seqs]
    t = _head_terms([q], DEC_SEQ, 1)
    rows2 = [slice(CHUNK + s * DEC_SEQ, CHUNK + (s + 1) * DEC_SEQ) for s in seqs]
    ws = [jnp.concatenate([_mm(t["w_rt"][h][rows[s]], sin_ref[s, h], _NT) for s in seqs], axis=0) for h in hs]
    rs = [jnp.concatenate([_mm(t["w_rt"][h][rows2[s]], sin_ref[s, h], _NT) for s in seqs], axis=0) for h in hs]
    uv = [jnp.concatenate([t["ubar"][h] - ws[h], t["vh"][h]], axis=0) for h in hs]
    y = _mms(t["a_r"], uv)
    if first:
        for other in range(1, n_layers):
            sout_ref[:, other] = jnp.zeros((SEQ_GROUP, HEADS, HEAD, HEAD), F32)
    for h in hs:
        for s in seqs:
            mine = jnp.where(seq_of_row2 == s, uv[h], 0.0)
            p_end = t["p_end"][h][s * DEC_SEQ:s * DEC_SEQ + 1, :]
            new = sin_ref[s, h] * p_end + _mm(mine, t["bk_end"][h], _TN)
            if first:
                sout_ref[s, 0, h] = new
            else:
                sout_ref[s, h] = new
    _head_outputs(t, [rs[h] + y[h] for h in hs], rk_ref, lnw_ref, lnb_ref, [z_ref] * HEADS)


def _chunk_sample(proj, p, state, new_states, layer, n_layers):
    ng = DEC_BATCH // SEQ_GROUP
    row_spec = pl.BlockSpec((CHUNK, D_MODEL), lambda g: (g, 0))
    vec_spec = pl.BlockSpec((1, D_MODEL), lambda g: (0, 0))
    in_specs = [row_spec] * 6 + [vec_spec] * 5 + [pl.BlockSpec((SEQ_GROUP, HEADS, HEAD, HEAD), lambda g: (g, 0, 0, 0))]
    args = [*proj, p["k_k"], p["k_a"], p["r_k"], p["lnx_w"], p["lnx_b"], state]
    if layer == 0:
        state_spec = pl.BlockSpec((SEQ_GROUP, n_layers, HEADS, HEAD, HEAD), lambda g: (g, 0, 0, 0, 0))
        aliases = {}
    else:
        state_spec = pl.BlockSpec((SEQ_GROUP, None, HEADS, HEAD, HEAD), lambda g: (g, layer, 0, 0, 0))
        in_specs.append(pl.BlockSpec(memory_space=pl.ANY))
        args.append(new_states)
        aliases = {len(args) - 1: 1}
    return pl.pallas_call(
        functools.partial(_chunk_sample_body, layer=layer, n_layers=n_layers),
        grid=(ng,),
        in_specs=in_specs,
        out_specs=[row_spec, state_spec],
        out_shape=[
            jax.ShapeDtypeStruct((SAMPLE_TOTAL, D_MODEL), BF16),
            jax.ShapeDtypeStruct((DEC_BATCH, n_layers, HEADS, HEAD, HEAD), F32),
        ],
        input_output_aliases=aliases,
        compiler_params=_cparams(("parallel",)),
        name="rwkv_chunk_sample",
    )(*args)


def _pad_cols(w):
    return jnp.pad(w, ((0, 0), (0, LORA_PAD - w.shape[1]))).astype(BF16)


def _pad_rows(w):
    return jnp.pad(w, ((0, LORA_PAD - w.shape[0]), (0, 0))).astype(BF16)


def kernel(x_prompt, x_sample, state_wkv, state_shift, cache_win_k, cache_win_v, cache_meta_k,
           cache_meta_v, meta_tokens, norm_gains, rwkv_mu, rwkv_w_r, rwkv_w_k, rwkv_w_v, rwkv_w_o,
           rwkv_w0, rwkv_w1, rwkv_w2, rwkv_a0, rwkv_a1, rwkv_a2, rwkv_v0, rwkv_v1, rwkv_v2,
           rwkv_g1, rwkv_g2, rwkv_k_k, rwkv_k_a, rwkv_r_k, rwkv_lnx_w, rwkv_lnx_b,
           attn_w_qkv, attn_w_o, attn_sinks, mlp_w_up, mlp_w_down):
    assert x_prompt.shape == (BATCH, SEQ, D_MODEL) and x_sample.shape == (DEC_BATCH, DEC_SEQ, D_MODEL)
    n_swa = cache_win_k.shape[1]
    row = lambda v: v.reshape(1, D_MODEL).astype(F32)

    head = jnp.concatenate([jnp.zeros((PAD_ROWS, D_MODEL), F32), meta_tokens.astype(F32)], axis=0)
    xp = jnp.concatenate([jnp.broadcast_to(head[None], (BATCH, WINDOW, D_MODEL)), x_prompt], axis=1)
    xp = xp.reshape(PROMPT_TOTAL, D_MODEL)
    xs = x_sample.reshape(SAMPLE_TOTAL, D_MODEL).astype(F32)

    cmk = cache_meta_k.reshape(DEC_BATCH, n_swa, N_META, KV_WIDTH)
    cmv = cache_meta_v.reshape(DEC_BATCH, n_swa, N_META, KV_WIDTH)
    cwk = cache_win_k.reshape(DEC_BATCH, n_swa, WINDOW, KV_WIDTH)
    cwv = cache_win_v.reshape(DEC_BATCH, n_swa, WINDOW, KV_WIDTH)

    w_up_all, w_down_all = mlp_w_up.astype(BF16), mlp_w_down.astype(BF16)
    w_r_all, w_k_all, w_v_all = rwkv_w_r.astype(BF16), rwkv_w_k.astype(BF16), rwkv_w_v.astype(BF16)
    rwkv_wo_all, attn_wo_all, w_qkv_all = rwkv_w_o.astype(BF16), attn_w_o.astype(BF16), attn_w_qkv.astype(BF16)
    p_wkv, p_shift, s_shift = [], [], []
    s_wkv = None
    p_wk, p_wv, p_mk, p_mv, s_wk, s_wv = [], [], [], [], [], []
    vfirst_p = vfirst_s = None
    for i in range(DEPTH):
        gains = norm_gains[i]
        j = i // 2
        if i % 2 == 0:
            p = dict(
                g0=row(gains[0]), mu=rwkv_mu[j],
                w_r=(w_r_all, j), w_k=(w_k_all, j), w_v=(w_v_all, j),
                w0=row(rwkv_w0[j]), w1=_pad_cols(rwkv_w1[j]), w2=_pad_rows(rwkv_w2[j]),
                a0=row(rwkv_a0[j]), a1=_pad_cols(rwkv_a1[j]), a2=_pad_rows(rwkv_a2[j]),
                g1=_pad_cols(rwkv_g1[j]), g2=_pad_rows(rwkv_g2[j]),
                k_k=row(rwkv_k_k[j]), k_a=row(rwkv_k_a[j]), r_k=row(rwkv_r_k[j]),
                lnx_w=row(rwkv_lnx_w[j]), lnx_b=row(rwkv_lnx_b[j]),
            )
            if j > 0:
                p.update(v0=row(rwkv_v0[j - 1]), v1=_pad_cols(rwkv_v1[j - 1]), v2=_pad_rows(rwkv_v2[j - 1]))
            shift_rows = jnp.zeros((DEC_BATCH, DEC_SEQ, D_MODEL), F32).at[:, 0].set(state_shift[:, j])
            outs_p = _prep(xp, xp, vfirst_p if j > 0 else None, p, prompt=True)
            outs_s = _prep(xs, shift_rows.reshape(SAMPLE_TOTAL, D_MODEL), vfirst_s if j > 0 else None, p,
                           prompt=False)
            if j == 0:
                vfirst_p, vfirst_s = outs_p[2], outs_s[2]
            zp, st_p = _chunk_prompt(outs_p[:6], p)
            zs, s_wkv = _chunk_sample(outs_s[:6], p, state_wkv[:, j], s_wkv, j, state_wkv.shape[1])
            xn_tail = outs_p[6].reshape(BATCH, _PREP_TILES_PER_PROMPT, 8, D_MODEL)
            p_wkv.append(st_p)
            p_shift.append(xn_tail[:, -1, -1])
            s_shift.append(outs_s[6].reshape(DEC_BATCH, DEC_SEQ, D_MODEL)[:, -1])
            w_o = (rwkv_wo_all, j)
        else:
            w_qkv = (w_qkv_all, j)
            q, kv = _qkv(xp, row(gains[0]), w_qkv)
            q_s, kv_s = _qkv(xs, row(gains[0]), w_qkv)
            sinks = attn_sinks[j].astype(F32)
            zp = _attn_prompt(sinks, q, kv)
            zs, nwk, nwv = _attn_sample(sinks, q_s, kv_s, cmk, cmv, cwk, cwv, j)
            def kv_rows(lo, hi, which):
                parts = [kv[b * PROMPT_ROWS + lo:b * PROMPT_ROWS + hi, which * KV_WIDTH:(which + 1) * KV_WIDTH]
                         for b in range(BATCH)]
                return jnp.stack(parts).reshape(BATCH, hi - lo, N_KV_HEADS, HEAD)

            p_mk.append(kv_rows(PAD_ROWS, WINDOW, 0))
            p_mv.append(kv_rows(PAD_ROWS, WINDOW, 1))
            p_wk.append(kv_rows(PROMPT_ROWS - WINDOW, PROMPT_ROWS, 0))
            p_wv.append(kv_rows(PROMPT_ROWS - WINDOW, PROMPT_ROWS, 1))
            s_wk.append(nwk.reshape(DEC_BATCH, WINDOW, N_KV_HEADS, HEAD))
            s_wv.append(nwv.reshape(DEC_BATCH, WINDOW, N_KV_HEADS, HEAD))
            w_o = (attn_wo_all, j)
        tail = (w_o, row(gains[1]), row(gains[2]), row(gains[3]), w_up_all, w_down_all, i)
        xp = _mixer_mlp(xp, zp, *tail)
        xs = _mixer_mlp(xs, zs, *tail)

    y_prompt = xp.reshape(BATCH, PROMPT_ROWS, D_MODEL)[:, WINDOW:]
    y_sample = xs.reshape(DEC_BATCH, DEC_SEQ, D_MODEL)
    st = lambda xs: jnp.stack(xs, axis=1)
    return (y_prompt, y_sample, st(p_wkv), st(p_shift), st(p_wk), st(p_wv), st(p_mk), st(p_mv),
            s_wkv, st(s_shift), st(s_wk), st(s_wv))
```

```python
import functools

import jax
import jax.numpy as jnp
from jax import lax
from jax.experimental import pallas as pl
from jax.experimental.pallas import tpu as pltpu

F32 = jnp.float32
BF16 = jnp.bfloat16

D_MODEL = 1024
BATCH = 2
SEQ = 8192
DEPTH = 4
DEC_BATCH = 128
DEC_SEQ = 8
PAST_LEN = 8192
N_META = 16
HEADS = 16
HEAD = 64
N_KV_HEADS = 4
GROUP = HEADS // N_KV_HEADS
KV_WIDTH = N_KV_HEADS * HEAD
WINDOW = 128
D_FF = 4 * D_MODEL
RMS_EPS = 1e-6
GN_EPS = 6.4e-4
NEG_INF = -1e30
LORA_PAD = 128

PAD_ROWS = WINDOW - N_META
PROMPT_ROWS = PAD_ROWS + N_META + SEQ
PROMPT_TOTAL = BATCH * PROMPT_ROWS
SAMPLE_TOTAL = DEC_BATCH * DEC_SEQ
TOTAL_ROWS = PROMPT_TOTAL + SAMPLE_TOTAL

CHUNK = 64
SUB = 16
SEQ_GROUP = CHUNK // DEC_SEQ
TM_PROMPT = 640
TM_SAMPLE = 512
TM_PREP_SAMPLE = 256
TM_PREP = 320
TF = 2048
VMEM_LIMIT = 56 * 1024 * 1024
SMALL_MM_PASSES = 1


def _cparams(sem):
    return pltpu.CompilerParams(dimension_semantics=sem, vmem_limit_bytes=VMEM_LIMIT)


def _weight_spec(w):
    if isinstance(w, tuple):
        arr, layer = w
        shape, index = (None,) + arr.shape[1:], (layer,) + (0,) * (arr.ndim - 1)
    else:
        arr, shape, index = w, w.shape, (0,) * w.ndim
    return pl.BlockSpec(shape, lambda *_: index), arr


def _rms(x, g):
    return x * lax.rsqrt(jnp.mean(x * x, axis=-1, keepdims=True) + RMS_EPS) * g


def _sigmoid(x):
    return 1.0 / (1.0 + jnp.exp(-x))


def _bdot(a, b):
    return jnp.dot(a.astype(BF16), b.astype(BF16), preferred_element_type=F32)


def _split(x):
    hi = x.astype(BF16)
    lo = (x - hi.astype(F32)).astype(BF16)
    return hi, lo


_NN = (((1,), (0,)), ((), ()))
_NT = (((1,), (1,)), ((), ()))
_TN = (((0,), (0,)), ((), ()))


def _mm(a, b, dims=_NN, passes=SMALL_MM_PASSES):
    dg = functools.partial(lax.dot_general, dimension_numbers=dims, preferred_element_type=F32)
    if passes == 1:
        return dg(a.astype(BF16), b.astype(BF16))
    ah, al = _split(a)
    bh, bl = _split(b)
    return dg(ah, bh) + (dg(ah, bl) + dg(al, bh))


def _div_pow2(x, n):
    shift = n.bit_length() - 1
    assert 1 << shift == n
    return jnp.right_shift(x, shift)


def _dot_exact_lhs(m01, x):
    hi, lo = _split(x)
    return (jnp.dot(m01, hi, preferred_element_type=F32)
            + jnp.dot(m01, lo, preferred_element_type=F32))


def _row_tile(rows):
    tm = TM_PROMPT if rows % TM_PROMPT == 0 else TM_SAMPLE
    assert rows % tm == 0
    return tm


def _mixer_mlp_body(x_ref, z_ref, wo_ref, g1_ref, g2_ref, g3_ref, wup_ref, wdn_ref, o_ref, x1_ref, xn_ref, acc_ref):
    f = pl.program_id(1)

    @pl.when(f == 0)
    def _():
        m = jnp.dot(z_ref[...], wo_ref[...], preferred_element_type=F32)
        x1 = x_ref[...] + _rms(m, g1_ref[...])
        x1_ref[...] = x1
        xn_ref[...] = _rms(x1, g2_ref[...]).astype(BF16)
        acc_ref[...] = jnp.zeros_like(acc_ref)

    h = jnp.dot(xn_ref[...], wup_ref[...], preferred_element_type=F32)
    a = jnp.maximum(h, 0.0)
    acc_ref[...] += jnp.dot((a * a).astype(BF16), wdn_ref[...], preferred_element_type=F32)

    @pl.when(f == pl.num_programs(1) - 1)
    def _():
        o_ref[...] = x1_ref[...] + _rms(acc_ref[...], g3_ref[...])


def _mixer_mlp(x, z, w_o, g1, g2, g3, wup, wdn, layer):
    rows = x.shape[0]
    tm = _row_tile(rows)
    vec = pl.BlockSpec((1, D_MODEL), lambda i, f: (0, 0))
    wo_spec, w_o = _weight_spec(w_o)
    return pl.pallas_call(
        _mixer_mlp_body,
        grid=(rows // tm, D_FF // TF),
        in_specs=[
            pl.BlockSpec((tm, D_MODEL), lambda i, f: (i, 0)),
            pl.BlockSpec((tm, D_MODEL), lambda i, f: (i, 0)),
            wo_spec,
            vec, vec, vec,
            pl.BlockSpec((None, D_MODEL, TF), lambda i, f: (layer, 0, f)),
            pl.BlockSpec((None, TF, D_MODEL), lambda i, f: (layer, f, 0)),
        ],
        out_specs=pl.BlockSpec((tm, D_MODEL), lambda i, f: (i, 0)),
        out_shape=jax.ShapeDtypeStruct((rows, D_MODEL), F32),
        scratch_shapes=[pltpu.VMEM((tm, D_MODEL), F32), pltpu.VMEM((tm, D_MODEL), BF16),
                        pltpu.VMEM((tm, D_MODEL), F32)],
        compiler_params=_cparams(("parallel", "arbitrary")),
        name="mixer_mlp",
    )(x, z, w_o, g1, g2, g3, wup, wdn)


def _qkv_body(x_ref, g_ref, w_ref, q_ref, kv_ref):
    xn = _rms(x_ref[...], g_ref[...]).astype(BF16)
    res = jnp.dot(xn, w_ref[...], preferred_element_type=F32)
    q_ref[...] = (res[:, :D_MODEL] * (HEAD ** -0.5)).astype(BF16)
    kv_ref[...] = res[:, D_MODEL:]


def _qkv(x, g, w):
    rows = x.shape[0]
    tm = _row_tile(rows)
    w_spec, w = _weight_spec(w)
    return pl.pallas_call(
        _qkv_body,
        grid=(rows // tm,),
        in_specs=[
            pl.BlockSpec((tm, D_MODEL), lambda i: (i, 0)),
            pl.BlockSpec((1, D_MODEL), lambda i: (0, 0)),
            w_spec,
        ],
        out_specs=[
            pl.BlockSpec((tm, D_MODEL), lambda i: (i, 0)),
            pl.BlockSpec((tm, 2 * KV_WIDTH), lambda i: (i, 0)),
        ],
        out_shape=[
            jax.ShapeDtypeStruct((rows, D_MODEL), BF16),
            jax.ShapeDtypeStruct((rows, 2 * KV_WIDTH), F32),
        ],
        compiler_params=_cparams(("parallel",)),
        name="qkv",
    )(x, g, w)


def _alibi_slope(h):
    return 2.0 ** (-8.0 * (h + 1) / HEADS)


def _group_queries(q, kvh):
    return jnp.concatenate([q[:, h * HEAD:(h + 1) * HEAD] for h in range(kvh * GROUP, (kvh + 1) * GROUP)], axis=0)


def _group_slopes(kvh, tq):
    g = _div_pow2(lax.broadcasted_iota(jnp.int32, (GROUP * tq, 1), 0), tq)
    slope = jnp.zeros((GROUP * tq, 1), F32)
    for i in range(GROUP):
        slope = jnp.where(g == i, _alibi_slope(kvh * GROUP + i), slope)
    return slope


def _group_sinks(sink_ref, kvh):
    return [sink_ref[kvh * GROUP + i] for i in range(GROUP)]


def _alibi_bias(ok, mind, slope):
    return jnp.where(ok, slope * mind, -NEG_INF)


def _attend_groups(qs, ks, vs, biases, sinks):
    scores = [lax.dot_general(q, k, _NT, preferred_element_type=F32) for q, k in zip(qs, ks)]
    probs = []
    for s, bias, sink in zip(scores, biases, sinks):
        logits = s - bias
        tq = logits.shape[0] // GROUP
        parts = []
        for i in range(GROUP):
            lg = logits[i * tq:(i + 1) * tq]
            m = jnp.maximum(jnp.max(lg, axis=-1, keepdims=True), sink[i])
            p = jnp.exp(lg - m)
            den = jnp.sum(p, axis=-1, keepdims=True) + jnp.exp(sink[i] - m)
            parts.append(p / den)
        probs.append(jnp.concatenate(parts, axis=0).astype(BF16))
    return [jnp.dot(p, v, preferred_element_type=F32) for p, v in zip(probs, vs)]


def _store_group(o_ref, rows, kvh, o, tq):
    for i in range(GROUP):
        h = kvh * GROUP + i
        o_ref[rows, h * HEAD:(h + 1) * HEAD] = o[i * tq:(i + 1) * tq].astype(o_ref.dtype)


_BLOCKS_PER_PROMPT = PROMPT_ROWS // WINDOW


def _attn_prompt_body(sink_ref, q_ref, kv0_ref, kvp_ref, kvc_ref, o_ref, bias_ref):
    i = pl.program_id(1)
    kvhs = range(N_KV_HEADS)

    @pl.when(i <= 2)
    def _():
        shp = (GROUP * WINDOW, 3 * WINDOW)
        r = jnp.bitwise_and(lax.broadcasted_iota(jnp.int32, shp, 0), WINDOW - 1)
        c = lax.broadcasted_iota(jnp.int32, shp, 1)
        kblk = _div_pow2(c, WINDOW)
        cc = c - kblk * WINDOW
        qpos = jnp.where(i == 0, r - PAD_ROWS, N_META + (i - 1) * WINDOW + r)
        kpos = jnp.where(kblk == 0, cc - PAD_ROWS, N_META + (i - 3 + kblk) * WINDOW + cc)
        kmeta = kblk == 0
        kvalid = jnp.where(kmeta, cc - PAD_ROWS, jnp.where(kblk == 1, i - 2, i - 1)) >= 0
        dist = qpos - kpos
        ok = kvalid & (dist >= 0) & (kmeta | (dist < WINDOW))
        mind = jnp.minimum(dist, WINDOW).astype(F32)
        for j in kvhs:
            bias_ref[j] = _alibi_bias(ok, mind, _group_slopes(j, WINDOW))

    kv = jnp.concatenate([kv0_ref[...], kvp_ref[...], kvc_ref[...]], axis=0).astype(BF16)
    q = q_ref[...]
    outs = _attend_groups(
        [_group_queries(q, j) for j in kvhs],
        [kv[:, j * HEAD:(j + 1) * HEAD] for j in kvhs],
        [kv[:, KV_WIDTH + j * HEAD:KV_WIDTH + (j + 1) * HEAD] for j in kvhs],
        [bias_ref[j] for j in kvhs],
        [_group_sinks(sink_ref, j) for j in kvhs])
    for j in kvhs:
        _store_group(o_ref, slice(None), j, outs[j], WINDOW)


def _attn_prompt(sinks, q, kv):
    nb = _BLOCKS_PER_PROMPT
    return pl.pallas_call(
        _attn_prompt_body,
        grid=(BATCH, nb),
        in_specs=[
            pl.BlockSpec(memory_space=pltpu.SMEM),
            pl.BlockSpec((WINDOW, D_MODEL), lambda b, i: (b * nb + i, 0)),
            pl.BlockSpec((WINDOW, 2 * KV_WIDTH), lambda b, i: (b * nb, 0)),
            pl.BlockSpec((WINDOW, 2 * KV_WIDTH), lambda b, i: (b * nb + jnp.maximum(i - 1, 0), 0)),
            pl.BlockSpec((WINDOW, 2 * KV_WIDTH), lambda b, i: (b * nb + i, 0)),
        ],
        out_specs=pl.BlockSpec((WINDOW, D_MODEL), lambda b, i: (b * nb + i, 0)),
        out_shape=jax.ShapeDtypeStruct((PROMPT_TOTAL, D_MODEL), BF16),
        scratch_shapes=[pltpu.VMEM((N_KV_HEADS, GROUP * WINDOW, 3 * WINDOW), F32)],
        compiler_params=_cparams(("arbitrary", "arbitrary")),
        name="attn_prompt",
    )(sinks, q, kv, kv, kv)


_ATT_SEQS = 8
_SAMPLE_KEYS = N_META + WINDOW + DEC_SEQ


def _attn_sample_body(sink_ref, q_ref, kv_ref, mk_ref, mv_ref, wk_ref, wv_ref, o_ref, nwk_ref, nwv_ref):
    shp = (GROUP * DEC_SEQ, _SAMPLE_KEYS)
    t = jnp.bitwise_and(lax.broadcasted_iota(jnp.int32, shp, 0), DEC_SEQ - 1)
    c = lax.broadcasted_iota(jnp.int32, shp, 1)
    qpos = PAST_LEN + t
    kpos = jnp.where(c < N_META, c,
                     jnp.where(c < N_META + WINDOW, PAST_LEN - WINDOW + (c - N_META),
                               PAST_LEN + (c - N_META - WINDOW)))
    kmeta = c < N_META
    kvalid = kmeta | (kpos >= N_META)
    dist = qpos - kpos
    ok = kvalid & (dist >= 0) & (kmeta | (dist < WINDOW))
    mind = jnp.minimum(dist, WINDOW).astype(F32)
    kvhs = range(N_KV_HEADS)
    biases = [_alibi_bias(ok, mind, _group_slopes(j, DEC_SEQ)) for j in kvhs]
    sinks = [_group_sinks(sink_ref, j) for j in kvhs]
    qs, ks, vs, bs, cs = [], [], [], [], []
    for s in range(_ATT_SEQS):
        rows = slice(s * DEC_SEQ, (s + 1) * DEC_SEQ)
        knew = kv_ref[rows, :KV_WIDTH]
        vnew = kv_ref[rows, KV_WIDTH:]
        wk = wk_ref[s]
        wv = wv_ref[s]
        nwk_ref[s] = jnp.concatenate([wk[DEC_SEQ:], knew], axis=0)
        nwv_ref[s] = jnp.concatenate([wv[DEC_SEQ:], vnew], axis=0)
        kall = jnp.concatenate([mk_ref[s], wk, knew], axis=0).astype(BF16)
        vall = jnp.concatenate([mv_ref[s], wv, vnew], axis=0).astype(BF16)
        q = q_ref[rows, :]
        for j in kvhs:
            qs.append(_group_queries(q, j))
            ks.append(kall[:, j * HEAD:(j + 1) * HEAD])
            vs.append(vall[:, j * HEAD:(j + 1) * HEAD])
            bs.append(biases[j])
            cs.append(sinks[j])
    outs = _attend_groups(qs, ks, vs, bs, cs)
    for s in range(_ATT_SEQS):
        for j in kvhs:
            _store_group(o_ref, slice(s * DEC_SEQ, (s + 1) * DEC_SEQ), j, outs[s * N_KV_HEADS + j], DEC_SEQ)


def _attn_sample(sinks, q, kv, cmk, cmv, cwk, cwv, layer):
    rows = _ATT_SEQS * DEC_SEQ
    ng = DEC_BATCH // _ATT_SEQS
    cache_spec = lambda n: pl.BlockSpec((_ATT_SEQS, None, n, KV_WIDTH), lambda g: (g, layer, 0, 0))
    return pl.pallas_call(
        _attn_sample_body,
        grid=(ng,),
        in_specs=[
            pl.BlockSpec(memory_space=pltpu.SMEM),
            pl.BlockSpec((rows, D_MODEL), lambda g: (g, 0)),
            pl.BlockSpec((rows, 2 * KV_WIDTH), lambda g: (g, 0)),
            cache_spec(N_META), cache_spec(N_META), cache_spec(WINDOW), cache_spec(WINDOW),
        ],
        out_specs=[
            pl.BlockSpec((rows, D_MODEL), lambda g: (g, 0)),
            pl.BlockSpec((_ATT_SEQS, WINDOW, KV_WIDTH), lambda g: (g, 0, 0)),
            pl.BlockSpec((_ATT_SEQS, WINDOW, KV_WIDTH), lambda g: (g, 0, 0)),
        ],
        out_shape=[
            jax.ShapeDtypeStruct((SAMPLE_TOTAL, D_MODEL), BF16),
            jax.ShapeDtypeStruct((DEC_BATCH, WINDOW, KV_WIDTH), F32),
            jax.ShapeDtypeStruct((DEC_BATCH, WINDOW, KV_WIDTH), F32),
        ],
        compiler_params=_cparams(("parallel",)),
        name="attn_sample",
    )(sinks, q, kv, cmk, cmv, cwk, cwv)


_PREP_TILES_PER_PROMPT = PROMPT_ROWS // TM_PREP
_DECAY_SCALE = 0.6065306597126334


def _prep_body(*refs, prompt, vres):
    it = iter(refs)
    x_ref = next(it)
    side_ref = next(it)
    vf_ref = next(it) if vres else None
    g0_ref, mu_ref = next(it), next(it)
    wr_ref, wk_ref, wv_ref = next(it), next(it), next(it)
    w0_ref, w1_ref, w2_ref = next(it), next(it), next(it)
    a0_ref, a1_ref, a2_ref = next(it), next(it), next(it)
    if vres:
        v0_ref, v1_ref, v2_ref = next(it), next(it), next(it)
    g1_ref, g2_ref = next(it), next(it)
    r_ref, k_ref, v_ref, a_ref, lw_ref, g_ref, xn_ref = (next(it) for _ in range(7))

    tm = x_ref.shape[0]
    g0 = g0_ref[...]
    xn = _rms(x_ref[...], g0)
    row = lax.broadcasted_iota(jnp.int32, (tm, 1), 0)
    if prompt:
        ib = lax.rem(pl.program_id(0), _PREP_TILES_PER_PROMPT)
        xn = jnp.where(ib * tm + row < PAD_ROWS, 0.0, xn)
        before = _rms(side_ref[...], g0)[7:8, :]
        before = jnp.where(ib == 0, 0.0, before)
        x_prev = jnp.where(row == 0, before, pltpu.roll(xn, 1, 0))
        xn_ref[...] = xn[tm - 8:, :]
    else:
        x_prev = jnp.where(jnp.bitwise_and(row, DEC_SEQ - 1) == 0, side_ref[...], pltpu.roll(xn, 1, 0))
        xn_ref[...] = xn
    xx = x_prev - xn
    xr, xw, xk, xv, xa, xg = (xn + xx * mu_ref[c:c + 1, :] for c in range(6))

    r_ref[...] = _bdot(xr, wr_ref[...])
    k_ref[...] = _bdot(xk, wk_ref[...])
    v = _bdot(xv, wv_ref[...])
    wl = w0_ref[...] + _bdot(jnp.tanh(_bdot(xw, w1_ref[...])), w2_ref[...])
    lw_ref[...] = -_DECAY_SCALE * _sigmoid(wl)
    if vres:
        mix = _sigmoid(v0_ref[...] + _bdot(_bdot(xv, v1_ref[...]), v2_ref[...]))
        v = v + (vf_ref[...] - v) * mix
    v_ref[...] = v
    a_ref[...] = _sigmoid(a0_ref[...] + _bdot(_bdot(xa, a1_ref[...]), a2_ref[...]))
    g_ref[...] = _bdot(_sigmoid(_bdot(xg, g1_ref[...])), g2_ref[...])


def _prep(x, side, vfirst, p, prompt):
    vres = vfirst is not None
    if prompt:
        tm, nt, nrows = TM_PREP, PROMPT_TOTAL // TM_PREP, PROMPT_TOTAL
        side_spec = pl.BlockSpec((8, D_MODEL), lambda i: (jnp.maximum(i * (TM_PREP // 8) - 1, 0), 0))
        xn_spec = pl.BlockSpec((8, D_MODEL), lambda i: (i, 0))
        xn_shape = jax.ShapeDtypeStruct((nt * 8, D_MODEL), F32)
    else:
        tm, nt, nrows = TM_PREP_SAMPLE, SAMPLE_TOTAL // TM_PREP_SAMPLE, SAMPLE_TOTAL
        side_spec = pl.BlockSpec((tm, D_MODEL), lambda i: (i, 0))
        xn_spec = pl.BlockSpec((tm, D_MODEL), lambda i: (i, 0))
        xn_shape = jax.ShapeDtypeStruct((nrows, D_MODEL), F32)
    row_spec = pl.BlockSpec((tm, D_MODEL), lambda i: (i, 0))

    args = [x, side]
    specs = [row_spec, side_spec]
    if vres:
        args.append(vfirst)
        specs.append(row_spec)
    names = ["g0", "mu", "w_r", "w_k", "w_v", "w0", "w1", "w2", "a0", "a1", "a2"]
    if vres:
        names += ["v0", "v1", "v2"]
    names += ["g1", "g2"]
    for n in names:
        spec, arr = _weight_spec(p[n])
        args.append(arr)
        specs.append(spec)
    big = jax.ShapeDtypeStruct((nrows, D_MODEL), F32)
    return pl.pallas_call(
        functools.partial(_prep_body, prompt=prompt, vres=vres),
        grid=(nt,),
        in_specs=specs,
        out_specs=[row_spec] * 6 + [xn_spec],
        out_shape=[big] * 6 + [xn_shape],
        compiler_params=_cparams(("parallel",)),
        name="rwkv_prep_prompt" if prompt else "rwkv_prep_sample",
    )(*args)


def _eye(n):
    r = lax.broadcasted_iota(jnp.int32, (n, n), 0)
    c = lax.broadcasted_iota(jnp.int32, (n, n), 1)
    return (r == c).astype(F32)


def _mms(a_list, b_list, dims=_NN):
    return [_mm(a, b, dims) for a, b in zip(a_list, b_list)]


def _unit_lower_inverse(n_list, sub, nblocks):
    n = n_list[0].shape[0]
    eye = _eye(n)
    r = lax.broadcasted_iota(jnp.int32, (n, n), 0)
    c = lax.broadcasted_iota(jnp.int32, (n, n), 1)
    diag_blk = _div_pow2(r, sub) == _div_pow2(c, sub)
    ms = [jnp.where(diag_blk, -x, 0.0) for x in n_list]
    tds = [eye + m for m in ms]
    span = 1
    while 2 * span < sub:
        ms = _mms(ms, ms)
        tds = [td + p for td, p in zip(tds, _mms(tds, ms))]
        span *= 2
    if nblocks == 1:
        return tds
    xs = _mms(tds, [jnp.where(diag_blk, 0.0, x) for x in n_list])
    zs = [eye - x for x in xs]
    if nblocks > 2:
        assert nblocks <= 4
        zs = [z + p for z, p in zip(zs, _mms(zs, _mms(xs, xs)))]
    return _mms(zs, tds)


def _stacked_causal_mask(c, seq_len):
    row = lax.broadcasted_iota(jnp.int32, (2 * c, 2 * c), 0)
    t = jnp.bitwise_and(row, c - 1)
    j = jnp.bitwise_and(lax.broadcasted_iota(jnp.int32, (2 * c, 2 * c), 1), c - 1)
    same = _div_pow2(t, seq_len) == _div_pow2(j, seq_len)
    return same & (t - j >= jnp.where(row < c, 1, 0))


def _chunk_prologue(refs, seq_len, lead=None):
    r_ref, k_ref, v_ref, a_ref, lw_ref, g_ref = (x if lead is None else x.at[lead] for x in refs[:6])
    kk_ref, ka_ref = refs[6:]
    c = CHUNK
    row = lax.broadcasted_iota(jnp.int32, (c, c), 0)
    col = lax.broadcasted_iota(jnp.int32, (c, c), 1)
    same = _div_pow2(row, seq_len) == _div_pow2(col, seq_len)
    strict = same & (row > col)
    incl = same & (row >= col)
    lw = lw_ref[...]
    one01 = lambda m: jnp.where(m, 1.0, 0.0).astype(BF16)
    cum = _dot_exact_lhs(one01(incl), lw)
    cum_end = _dot_exact_lhs(one01(same), lw)
    k_raw = k_ref[...]
    a = a_ref[...]
    q = dict(
        mask2=_stacked_causal_mask(c, seq_len),
        r=r_ref[...], v=v_ref[...], a=a, g=g_ref[...],
        k=k_raw * (1.0 + (a - 1.0) * ka_ref[...]),
        kk_un=k_raw * kk_ref[...],
        e_cum=jnp.exp(cum), e_neg=jnp.exp(-cum), e_excl=jnp.exp(cum - lw),
        e_end=jnp.exp(cum_end - cum), p_end=jnp.exp(cum_end),
    )
    return q


def _head_terms(qs, sub, nblocks):
    hs = range(len(qs) * HEADS)
    col = lambda name: [q[name][:, h * HEAD:(h + 1) * HEAD] for q in qs for h in range(HEADS)]
    rh, kh, vh, ah, kk = col("r"), col("k"), col("v"), col("a"), col("kk_un")
    e_excl, e_cum, e_neg, e_end = col("e_excl"), col("e_cum"), col("e_neg"), col("e_end")
    kk = [x / jnp.maximum(jnp.sqrt(jnp.sum(x * x, axis=-1, keepdims=True)), 1e-12) for x in kk]
    bh = [kk[h] * ah[h] for h in hs]
    at = [kk[h] * e_excl[h] for h in hs]
    rt = [rh[h] * e_cum[h] for h in hs]
    bt = [bh[h] * e_neg[h] for h in hs]
    kt = [kh[h] * e_neg[h] for h in hs]
    bk_end = [jnp.concatenate([bh[h] * e_end[h], kh[h] * e_end[h]], axis=0) for h in hs]
    a_all = _mms([jnp.concatenate([at[h], rt[h]], axis=0) for h in hs],
                 [jnp.concatenate([bt[h], kt[h]], axis=0) for h in hs], _NT)
    a_all = [jnp.where(qs[0]["mask2"], x, 0.0) for x in a_all]
    t = _unit_lower_inverse([x[:CHUNK, :CHUNK] for x in a_all], sub, nblocks)
    w = _mms(t, at)
    ubar = [-x for x in _mms(t, _mms([x[:CHUNK, CHUNK:] for x in a_all], vh))]
    return dict(rh=rh, kh=kh, vh=vh, w_rt=[jnp.concatenate([w[h], rt[h]], axis=0) for h in hs], ubar=ubar,
                a_r=[x[CHUNK:, :] for x in a_all], bk_end=bk_end, p_end=col("p_end"), g=col("g"))


def _head_outputs(t, ys, rk_ref, lnw_ref, lnb_ref, z_refs):
    ns = range(len(ys))
    sls = [slice((n % HEADS) * HEAD, (n % HEADS + 1) * HEAD) for n in ns]
    bonus = [jnp.sum(t["rh"][n] * t["kh"][n] * rk_ref[:, sls[n]], axis=-1, keepdims=True) for n in ns]
    mean = [jnp.mean(y, axis=-1, keepdims=True) for y in ys]
    d = [ys[n] - mean[n] for n in ns]
    var = [jnp.mean(x * x, axis=-1, keepdims=True) for x in d]
    for n in ns:
        yn = d[n] * lax.rsqrt(var[n] + GN_EPS) * lnw_ref[:, sls[n]] + lnb_ref[:, sls[n]]
        z_refs[n][:, sls[n]] = ((yn + bonus[n] * t["vh"][n]) * t["g"][n]).astype(BF16)


QUAD = 4
QW = QUAD * HEAD
NQ = HEADS // QUAD
CHUNKS_PER_STEP = 5


def _bf(x):
    return x.astype(BF16)


def _dg(a, b, dims=_NN):
    return lax.dot_general(_bf(a), _bf(b), dims, preferred_element_type=F32)


def _seg_sum(x, e_seg):
    return jnp.dot(_bf(x), e_seg, preferred_element_type=F32)


def _chunk_prompt_body(r_ref, k_ref, v_ref, a_ref, lw_ref, g_ref, kk_ref, ka_ref, rk_ref, lnw_ref, lnb_ref,
                       z_ref, sout_ref, s_ref):
    ci = pl.program_id(1)

    @pl.when(ci == 0)
    def _():
        s_ref[...] = jnp.zeros_like(s_ref)

    rows = r_ref.shape[1]
    nsub = rows // CHUNK
    i32 = jnp.int32
    t64 = lax.broadcasted_iota(i32, (CHUNK, QW), 0)
    j64 = jnp.bitwise_and(lax.broadcasted_iota(i32, (CHUNK, QW), 1), HEAD - 1)
    strict = t64 > j64
    diag_blk = _div_pow2(t64, SUB) == _div_pow2(j64, SUB)
    eye_q = jnp.where(t64 == j64, 1.0, 0.0)
    t2 = lax.broadcasted_iota(i32, (CHUNK, 2 * QW), 0)
    j2 = jnp.bitwise_and(lax.broadcasted_iota(i32, (CHUNK, 2 * QW), 1), HEAD - 1)
    incl2 = t2 >= j2
    same_head = (_div_pow2(lax.broadcasted_iota(i32, (QW, QW), 0), HEAD)
                 == _div_pow2(lax.broadcasted_iota(i32, (QW, QW), 1), HEAD))
    bd_f32 = jnp.where(same_head, 1.0, 0.0)
    bd_bf = bd_f32.astype(BF16)

    lane = lax.broadcasted_iota(i32, (CHUNK, 2 * HEAD), 1)
    keep_lo = jnp.where(lane < HEAD, 1.0, 0.0).astype(BF16)
    keep_hi = jnp.where(lane >= HEAD, 1.0, 0.0).astype(BF16)
    zero_t = jnp.zeros((CHUNK, 2 * HEAD), BF16)

    def bd(x):
        xb = _bf(x)
        x0, x1 = xb[:, :2 * HEAD], xb[:, 2 * HEAD:]
        return jnp.concatenate([
            jnp.concatenate([x0 * keep_lo, zero_t], axis=1), jnp.concatenate([x0 * keep_hi, zero_t], axis=1),
            jnp.concatenate([zero_t, x1 * keep_lo], axis=1), jnp.concatenate([zero_t, x1 * keep_hi], axis=1),
        ], axis=0)

    rr = lax.broadcasted_iota(i32, (rows, rows), 0)
    cc = lax.broadcasted_iota(i32, (rows, rows), 1)
    same_chunk = _div_pow2(rr, CHUNK) == _div_pow2(cc, CHUNK)
    one01 = lambda m: jnp.where(m, 1.0, 0.0).astype(BF16)
    lw = lw_ref[0]
    cum = _dot_exact_lhs(one01(same_chunk & (rr >= cc)), lw)
    cum_end = _dot_exact_lhs(one01(same_chunk), lw)
    r, v, a, k_raw = r_ref[0], v_ref[0], a_ref[0], k_ref[0]
    k_mod = k_raw * (1.0 + (a - 1.0) * ka_ref[...])
    kk_un = k_raw * kk_ref[...]
    sq = kk_un * kk_un
    rkk = r * k_mod * rk_ref[...]
    qsl = [slice(q * QW, (q + 1) * QW) for q in range(NQ)]
    sums = [_seg_sum(jnp.concatenate([sq[:, s], rkk[:, s]], axis=0), bd_bf) for s in qsl]
    ss = jnp.concatenate([x[:rows] for x in sums], axis=1)
    bonus = jnp.concatenate([x[rows:] for x in sums], axis=1) * v
    kk = kk_un / jnp.maximum(jnp.sqrt(ss), 1e-12)
    b = kk * a
    e_neg = jnp.exp(-cum)
    e_end = jnp.exp(cum_end - cum)
    at = kk * jnp.exp(cum - lw)
    rt = r * jnp.exp(cum)
    bt = b * e_neg
    kt = k_mod * e_neg
    b_end = b * e_end
    k_end = k_mod * e_end
    p_end = jnp.exp(cum_end)

    probs = [(g, q) for g in range(nsub) for q in range(NQ)]
    blk = lambda x, g, q: x[g * CHUNK:(g + 1) * CHUNK, qsl[q]]
    np_ = range(len(probs))

    a_all = [_dg(jnp.concatenate([blk(at, g, q), blk(rt, g, q)], axis=0),
                 jnp.concatenate([bd(blk(bt, g, q)), bd(blk(kt, g, q))], axis=0), _NT) for g, q in probs]
    n_ab = [jnp.where(strict, x[:CHUNK, :QW], 0.0) for x in a_all]
    a_ak = [_bf(jnp.where(strict, x[:CHUNK, QW:], 0.0)) for x in a_all]
    a_r = [_bf(jnp.where(incl2, x[CHUNK:, :], 0.0)) for x in a_all]
    bd_v = [bd(blk(v, g, q)) for g, q in probs]
    akv = [_dg(a_ak[i], bd_v[i]) for i in np_]
    m = [jnp.where(diag_blk, -x, 0.0) for x in n_ab]
    td = [eye_q + x for x in m]
    m = [_dg(x, bd(x)) for x in m]
    for _ in range(2):
        both = [_dg(jnp.concatenate([m[i], td[i]], axis=0), bd(m[i])) for i in np_]
        m = [x[:CHUNK] for x in both]
        td = [td[i] + both[i][CHUNK:] for i in np_]
    td = [td[i] + _dg(td[i], bd(m[i])) for i in np_]
    xo = [_dg(td[i], bd(jnp.where(diag_blk, 0.0, n_ab[i]))) for i in np_]
    x2 = [_dg(x, bd(x)) for x in xo]
    zz = [eye_q - x for x in xo]
    zz = [zz[i] + _dg(zz[i], bd(x2[i])) for i in np_]
    t_inv = [_bf(_dg(zz[i], bd(td[i]))) for i in np_]
    wu = [_dg(t_inv[i], jnp.concatenate([bd(blk(at, *probs[i])), bd(akv[i])], axis=1)) for i in np_]
    w_b = [_bf(x[:, :QW]) for x in wu]
    ubar_neg = [x[:, QW:] for x in wu]

    y_rows = []
    for g in range(nsub):
        ids = [g * NQ + q for q in range(NQ)]
        s0 = [s_ref[q] for q in range(NQ)]
        x = [_dg(jnp.concatenate([w_b[i], _bf(blk(rt, g, q))], axis=0), s0[q], _NT)
             for q, i in enumerate(ids)]
        u = [-ubar_neg[i] - x[q][:CHUNK] for q, i in enumerate(ids)]
        y = [x[q][CHUNK:] + _dg(a_r[i], jnp.concatenate([bd(u[q]), bd_v[i]], axis=0))
             for q, i in enumerate(ids)]
        ds = [_dg(jnp.concatenate([u[q], blk(v, g, q)], axis=0),
                  jnp.concatenate([blk(b_end, g, q), blk(k_end, g, q)], axis=0), _TN) for q in range(NQ)]
        for q in range(NQ):
            s_ref[q] = s0[q] * p_end[g * CHUNK:g * CHUNK + 1, qsl[q]] + ds[q] * bd_f32
        y_rows.append(jnp.concatenate(y, axis=1))
    y = jnp.concatenate(y_rows, axis=0)

    inv_n = 1.0 / HEAD
    mean = jnp.concatenate([_seg_sum(y[:, s], bd_bf) for s in qsl], axis=1) * inv_n
    d = y - mean
    var = jnp.concatenate([_seg_sum((d * d)[:, s], bd_bf) for s in qsl], axis=1) * inv_n
    yn = d * lax.rsqrt(var + GN_EPS) * lnw_ref[...] + lnb_ref[...]
    z_ref[0] = ((yn + bonus) * g_ref[0]).astype(BF16)

    @pl.when(ci == pl.num_programs(1) - 1)
    def _():
        for h in range(HEADS):
            o = (h % QUAD) * HEAD
            sout_ref[0, h] = s_ref[h // QUAD, o:o + HEAD, o:o + HEAD]


def _chunk_prompt(proj, p):
    rows = CHUNKS_PER_STEP * CHUNK
    nc = PROMPT_ROWS // rows
    row_spec = pl.BlockSpec((1, rows, D_MODEL), lambda b, c: (b, c, 0))
    vec_spec = pl.BlockSpec((1, D_MODEL), lambda b, c: (0, 0))
    z, st = pl.pallas_call(
        _chunk_prompt_body,
        grid=(BATCH, nc),
        in_specs=[row_spec] * 6 + [vec_spec] * 5,
        out_specs=[
            row_spec,
            pl.BlockSpec((1, HEADS, HEAD, HEAD), lambda b, c: (b, 0, 0, 0)),
        ],
        out_shape=[
            jax.ShapeDtypeStruct((BATCH, PROMPT_ROWS, D_MODEL), BF16),
            jax.ShapeDtypeStruct((BATCH, HEADS, HEAD, HEAD), F32),
        ],
        scratch_shapes=[pltpu.VMEM((NQ, QW, QW), F32)],
        compiler_params=_cparams(("parallel", "arbitrary")),
        name="rwkv_chunk_prompt",
    )(*[x.reshape(BATCH, PROMPT_ROWS, D_MODEL) for x in proj],
      p["k_k"], p["k_a"], p["r_k"], p["lnx_w"], p["lnx_b"])
    return z.reshape(PROMPT_TOTAL, D_MODEL), st


def _chunk_sample_body(*refs, layer, n_layers):
    first = layer == 0
    (r_ref, k_ref, v_ref, a_ref, lw_ref, g_ref, kk_ref, ka_ref, rk_ref, lnw_ref, lnb_ref, sin_ref) = refs[:12]
    z_ref, sout_ref = refs[-2:]
    q = _chunk_prologue((r_ref, k_ref, v_ref, a_ref, lw_ref, g_ref, kk_ref, ka_ref), DEC_SEQ)
    seq_of_row2 = _div_pow2(jnp.bitwise_and(lax.broadcasted_iota(jnp.int32, (2 * CHUNK, 1), 0), CHUNK - 1),
                            DEC_SEQ)
    hs = range(HEADS)
    seqs = range(SEQ_GROUP)
    rows = [slice(s * DEC_SEQ, (s + 1) * DEC_SEQ) for s in seqs]
    t = _head_terms([q], DEC_SEQ, 1)
    rows2 = [slice(CHUNK + s * DEC_SEQ, CHUNK + (s + 1) * DEC_SEQ) for s in seqs]
    ws = [jnp.concatenate([_mm(t["w_rt"][h][rows[s]], sin_ref[s, h], _NT) for s in seqs], axis=0) for h in hs]
    rs = [jnp.concatenate([_mm(t["w_rt"][h][rows2[s]], sin_ref[s, h], _NT) for s in seqs], axis=0) for h in hs]
    uv = [jnp.concatenate([t["ubar"][h] - ws[h], t["vh"][h]], axis=0) for h in hs]
    y = _mms(t["a_r"], uv)
    if first:
        for other in range(1, n_layers):
            sout_ref[:, other] = jnp.zeros((SEQ_GROUP, HEADS, HEAD, HEAD), F32)
    for h in hs:
        for s in seqs:
            mine = jnp.where(seq_of_row2 == s, uv[h], 0.0)
            p_end = t["p_end"][h][s * DEC_SEQ:s * DEC_SEQ + 1, :]
            new = sin_ref[s, h] * p_end + _mm(mine, t["bk_end"][h], _TN)
            if first:
                sout_ref[s, 0, h] = new
            else:
                sout_ref[s, h] = new
    _head_outputs(t, [rs[h] + y[h] for h in hs], rk_ref, lnw_ref, lnb_ref, [z_ref] * HEADS)


def _chunk_sample(proj, p, state, new_states, layer, n_layers):
    ng = DEC_BATCH // SEQ_GROUP
    row_spec = pl.BlockSpec((CHUNK, D_MODEL), lambda g: (g, 0))
    vec_spec = pl.BlockSpec((1, D_MODEL), lambda g: (0, 0))
    in_specs = [row_spec] * 6 + [vec_spec] * 5 + [pl.BlockSpec((SEQ_GROUP, HEADS, HEAD, HEAD), lambda g: (g, 0, 0, 0))]
    args = [*proj, p["k_k"], p["k_a"], p["r_k"], p["lnx_w"], p["lnx_b"], state]
    if layer == 0:
        state_spec = pl.BlockSpec((SEQ_GROUP, n_layers, HEADS, HEAD, HEAD), lambda g: (g, 0, 0, 0, 0))
        aliases = {}
    else:
        state_spec = pl.BlockSpec((SEQ_GROUP, None, HEADS, HEAD, HEAD), lambda g: (g, layer, 0, 0, 0))
        in_specs.append(pl.BlockSpec(memory_space=pl.ANY))
        args.append(new_states)
        aliases = {len(args) - 1: 1}
    return pl.pallas_call(
        functools.partial(_chunk_sample_body, layer=layer, n_layers=n_layers),
        grid=(ng,),
        in_specs=in_specs,
        out_specs=[row_spec, state_spec],
        out_shape=[
            jax.ShapeDtypeStruct((SAMPLE_TOTAL, D_MODEL), BF16),
            jax.ShapeDtypeStruct((DEC_BATCH, n_layers, HEADS, HEAD, HEAD), F32),
        ],
        input_output_aliases=aliases,
        compiler_params=_cparams(("parallel",)),
        name="rwkv_chunk_sample",
    )(*args)


def _pad_cols(w):
    return jnp.pad(w, ((0, 0), (0, LORA_PAD - w.shape[1]))).astype(BF16)


def _pad_rows(w):
    return jnp.pad(w, ((0, LORA_PAD - w.shape[0]), (0, 0))).astype(BF16)


def kernel(x_prompt, x_sample, state_wkv, state_shift, cache_win_k, cache_win_v, cache_meta_k,
           cache_meta_v, meta_tokens, norm_gains, rwkv_mu, rwkv_w_r, rwkv_w_k, rwkv_w_v, rwkv_w_o,
           rwkv_w0, rwkv_w1, rwkv_w2, rwkv_a0, rwkv_a1, rwkv_a2, rwkv_v0, rwkv_v1, rwkv_v2,
           rwkv_g1, rwkv_g2, rwkv_k_k, rwkv_k_a, rwkv_r_k, rwkv_lnx_w, rwkv_lnx_b,
           attn_w_qkv, attn_w_o, attn_sinks, mlp_w_up, mlp_w_down):
    assert x_prompt.shape == (BATCH, SEQ, D_MODEL) and x_sample.shape == (DEC_BATCH, DEC_SEQ, D_MODEL)
    n_swa = cache_win_k.shape[1]
    row = lambda v: v.reshape(1, D_MODEL).astype(F32)

    head = jnp.concatenate([jnp.zeros((PAD_ROWS, D_MODEL), F32), meta_tokens.astype(F32)], axis=0)
    xp = jnp.concatenate([jnp.broadcast_to(head[None], (BATCH, WINDOW, D_MODEL)), x_prompt], axis=1)
    xp = xp.reshape(PROMPT_TOTAL, D_MODEL)
    xs = x_sample.reshape(SAMPLE_TOTAL, D_MODEL).astype(F32)

    cmk = cache_meta_k.reshape(DEC_BATCH, n_swa, N_META, KV_WIDTH)
    cmv = cache_meta_v.reshape(DEC_BATCH, n_swa, N_META, KV_WIDTH)
    cwk = cache_win_k.reshape(DEC_BATCH, n_swa, WINDOW, KV_WIDTH)
    cwv = cache_win_v.reshape(DEC_BATCH, n_swa, WINDOW, KV_WIDTH)

    w_up_all, w_down_all = mlp_w_up.astype(BF16), mlp_w_down.astype(BF16)
    w_r_all, w_k_all, w_v_all = rwkv_w_r.astype(BF16), rwkv_w_k.astype(BF16), rwkv_w_v.astype(BF16)
    rwkv_wo_all, attn_wo_all, w_qkv_all = rwkv_w_o.astype(BF16), attn_w_o.astype(BF16), attn_w_qkv.astype(BF16)
    p_wkv, p_shift, s_shift = [], [], []
    s_wkv = None
    p_wk, p_wv, p_mk, p_mv, s_wk, s_wv = [], [], [], [], [], []
    vfirst_p = vfirst_s = None
    for i in range(DEPTH):
        gains = norm_gains[i]
        j = i // 2
        if i % 2 == 0:
            p = dict(
                g0=row(gains[0]), mu=rwkv_mu[j],
                w_r=(w_r_all, j), w_k=(w_k_all, j), w_v=(w_v_all, j),
                w0=row(rwkv_w0[j]), w1=_pad_cols(rwkv_w1[j]), w2=_pad_rows(rwkv_w2[j]),
                a0=row(rwkv_a0[j]), a1=_pad_cols(rwkv_a1[j]), a2=_pad_rows(rwkv_a2[j]),
                g1=_pad_cols(rwkv_g1[j]), g2=_pad_rows(rwkv_g2[j]),
                k_k=row(rwkv_k_k[j]), k_a=row(rwkv_k_a[j]), r_k=row(rwkv_r_k[j]),
                lnx_w=row(rwkv_lnx_w[j]), lnx_b=row(rwkv_lnx_b[j]),
            )
            if j > 0:
                p.update(v0=row(rwkv_v0[j - 1]), v1=_pad_cols(rwkv_v1[j - 1]), v2=_pad_rows(rwkv_v2[j - 1]))
            shift_rows = jnp.zeros((DEC_BATCH, DEC_SEQ, D_MODEL), F32).at[:, 0].set(state_shift[:, j])
            outs_p = _prep(xp, xp, vfirst_p if j > 0 else None, p, prompt=True)
            outs_s = _prep(xs, shift_rows.reshape(SAMPLE_TOTAL, D_MODEL), vfirst_s if j > 0 else None, p,
                           prompt=False)
            if j == 0:
                vfirst_p, vfirst_s = outs_p[2], outs_s[2]
            zp, st_p = _chunk_prompt(outs_p[:6], p)
            zs, s_wkv = _chunk_sample(outs_s[:6], p, state_wkv[:, j], s_wkv, j, state_wkv.shape[1])
            xn_tail = outs_p[6].reshape(BATCH, _PREP_TILES_PER_PROMPT, 8, D_MODEL)
            p_wkv.append(st_p)
            p_shift.append(xn_tail[:, -1, -1])
            s_shift.append(outs_s[6].reshape(DEC_BATCH, DEC_SEQ, D_MODEL)[:, -1])
            w_o = (rwkv_wo_all, j)
        else:
            w_qkv = (w_qkv_all, j)
            q, kv = _qkv(xp, row(gains[0]), w_qkv)
            q_s, kv_s = _qkv(xs, row(gains[0]), w_qkv)
            sinks = attn_sinks[j].astype(F32)
            zp = _attn_prompt(sinks, q, kv)
            zs, nwk, nwv = _attn_sample(sinks, q_s, kv_s, cmk, cmv, cwk, cwv, j)
            def kv_rows(lo, hi, which):
                parts = [kv[b * PROMPT_ROWS + lo:b * PROMPT_ROWS + hi, which * KV_WIDTH:(which + 1) * KV_WIDTH]
                         for b in range(BATCH)]
                return jnp.stack(parts).reshape(BATCH, hi - lo, N_KV_HEADS, HEAD)

            p_mk.append(kv_rows(PAD_ROWS, WINDOW, 0))
            p_mv.append(kv_rows(PAD_ROWS, WINDOW, 1))
            p_wk.append(kv_rows(PROMPT_ROWS - WINDOW, PROMPT_ROWS, 0))
            p_wv.append(kv_rows(PROMPT_ROWS - WINDOW, PROMPT_ROWS, 1))
            s_wk.append(nwk.reshape(DEC_BATCH, WINDOW, N_KV_HEADS, HEAD))
            s_wv.append(nwv.reshape(DEC_BATCH, WINDOW, N_KV_HEADS, HEAD))
            w_o = (attn_wo_all, j)
        tail = (w_o, row(gains[1]), row(gains[2]), row(gains[3]), w_up_all, w_down_all, i)
        xp = _mixer_mlp(xp, zp, *tail)
        xs = _mixer_mlp(xs, zs, *tail)

    y_prompt = xp.reshape(BATCH, PROMPT_ROWS, D_MODEL)[:, WINDOW:]
    y_sample = xs.reshape(DEC_BATCH, DEC_SEQ, D_MODEL)
    st = lambda xs: jnp.stack(xs, axis=1)
    return (y_prompt, y_sample, st(p_wkv), st(p_shift), st(p_wk), st(p_wv), st(p_mk), st(p_mv),
            s_wkv, st(s_shift), st(s_wk), st(s_wv))
```
